```python
import math
import jax, jax.numpy as jnp
from jax import lax
import numpy as np

D_MODEL = 2048
BATCH = 4
SEQ = 2048
DEPTH = 2

HEAD_DIM = 128
ROPE_THETA = 10000.0
NORM_EPS = 1e-6
A_HEADS = 8
A_WIDTH = A_HEADS * HEAD_DIM
DIL_CONFIGS = ((128, 1), (512, 4), (2048, 16))
B_HEADS = 4
B_VDIM = 2 * HEAD_DIM
B_WIDTH = B_HEADS * B_VDIM
Q_BLOCK = 128
C_HEADS = D_MODEL // HEAD_DIM
C_WIDTH = C_HEADS * HEAD_DIM
MOBA_BLOCK = 256
MOBA_TOPK = 3
MOBA_Q_CHUNK = 16
AB_IN = 3 * A_WIDTH + 3 * B_WIDTH + (A_WIDTH + B_WIDTH)
C_IN = 4 * C_WIDTH
N_AB = (DEPTH + 1) // 2
N_C = DEPTH // 2

kernel_name = "hybrid_dilated_diff_moba_gated"


def _rmsnorm(x, g):
    x32 = x.astype(jnp.float32)
    y = x32 * lax.rsqrt(jnp.mean(x32 * x32, axis=-1, keepdims=True) + NORM_EPS)
    return y * g.astype(jnp.float32)


def _rope(t, pos):
    dh = t.shape[-1]
    inv = 1.0 / (ROPE_THETA ** (jnp.arange(0, dh, 2, dtype=jnp.float32) / dh))
    ang = pos.astype(jnp.float32)[:, None] * inv[None, :]
    cos = jnp.concatenate([jnp.cos(ang), jnp.cos(ang)], axis=-1)
    sin = jnp.concatenate([jnp.sin(ang), jnp.sin(ang)], axis=-1)
    x1, x2 = jnp.split(t, 2, axis=-1)
    rot = jnp.concatenate([-x2, x1], axis=-1)
    return t * cos + rot * sin


def _heads(t, n_heads):
    b, s, w = t.shape
    return t.reshape(b, s, n_heads, w // n_heads).transpose(0, 2, 1, 3)


def _merge(t):
    b, h, s, d = t.shape
    return t.transpose(0, 2, 1, 3).reshape(b, s, h * d)


def _dilated_group(q, k, v, dil, nback):
    B, H, S, Dh = q.shape
    blk = nback
    L = -(-S // dil)
    L = -(-L // blk) * blk
    Sp = L * dil
    nb = L // blk
    pad = ((0, 0), (0, 0), (0, Sp - S), (0, 0))

    def split(t):
        t = jnp.pad(t, pad).reshape(B, H, L, dil, Dh).transpose(0, 1, 3, 2, 4)
        return t.reshape(B, H, dil, nb, blk, Dh)

    def with_prev(t):
        prev = jnp.pad(t[:, :, :, :-1], ((0, 0), (0, 0), (0, 0), (1, 0), (0, 0), (0, 0)))
        return jnp.concatenate([prev, t], axis=4)

    qs = split(q)
    kk = with_prev(split(k))
    vv = with_prev(split(v))
    s = jnp.einsum('bhrnid,bhrnjd->bhrnij', qs, kk) * (Dh ** -0.5)
    nbi = jnp.arange(nb)[:, None, None]
    i = jnp.arange(blk)[None, :, None]
    j = jnp.arange(2 * blk)[None, None, :]
    rel = blk + i - j
    kidx = (nbi - 1) * blk + j
    mask = (rel >= 0) & (rel <= nback) & (kidx >= 0)
    s = jnp.where(mask, s, -jnp.inf)
    m = jnp.max(s, axis=-1, keepdims=True)
    p = jnp.exp(s - m)
    den = jnp.sum(p, axis=-1, keepdims=True)
    o = jnp.einsum('bhrnij,bhrnjd->bhrnid', p, vv) / den
    lse = m + jnp.log(den)

    def unsplit(t):
        c = t.shape[-1]
        t = t.reshape(B, H, dil, L, c).transpose(0, 1, 3, 2, 4).reshape(B, H, Sp, c)
        return t[:, :, :S]

    return unsplit(o), unsplit(lse)


def _dilated_mixture(q, k, v):
    outs, lses = [], []
    for window, dil in DIL_CONFIGS:
        o, lse = _dilated_group(q, k, v, dil, window // dil)
        outs.append(o)
        lses.append(lse)
    w = jax.nn.softmax(jnp.stack(lses, axis=0), axis=0)
    return jnp.sum(w * jnp.stack(outs, axis=0), axis=0)


def _diff_attention(q1, q2, k1, k2, v, lam):
    B, H, S, Dh = q1.shape
    scale = Dh ** -0.5
    kpos = jnp.arange(S)

    def blk(c):
        start = c * Q_BLOCK
        qa = lax.dynamic_slice_in_dim(q1, start, Q_BLOCK, axis=2)
        qb = lax.dynamic_slice_in_dim(q2, start, Q_BLOCK, axis=2)
        qpos = start + jnp.arange(Q_BLOCK)
        mask = kpos[None, :] <= qpos[:, None]
        p1 = jax.nn.softmax(jnp.where(mask, jnp.einsum('bhid,bhjd->bhij', qa, k1) * scale, -jnp.inf), axis=-1)
        p2 = jax.nn.softmax(jnp.where(mask, jnp.einsum('bhid,bhjd->bhij', qb, k2) * scale, -jnp.inf), axis=-1)
        return jnp.einsum('bhij,bhjd->bhid', p1 - lam * p2, v)

    o = lax.map(blk, jnp.arange(S // Q_BLOCK))
    return o.transpose(1, 2, 0, 3, 4).reshape(B, H, S, v.shape[-1])


def _moba_attention(q, k, v):
    B, H, S, Dh = q.shape
    scale = Dh ** -0.5
    nb = -(-S // MOBA_BLOCK)
    Sp = nb * MOBA_BLOCK
    pad = ((0, 0), (0, 0), (0, Sp - S), (0, 0))
    q, k, v = jnp.pad(q, pad), jnp.pad(k, pad), jnp.pad(v, pad)
    kb = k.reshape(B, H, nb, MOBA_BLOCK, Dh)
    vb = v.reshape(B, H, nb, MOBA_BLOCK, Dh)
    n_sel = min(MOBA_TOPK, nb - 1)
    if n_sel > 0:
        kmean = jnp.mean(kb, axis=3)
        gate = jnp.einsum('bhsd,bhnd->bhsn', q, kmean)
        qblk = jnp.arange(Sp) // MOBA_BLOCK
        past = jnp.arange(nb)[None, :] < qblk[:, None]
        gate = jnp.where(past, gate, -jnp.inf)
        top_vals, top_idx = lax.top_k(gate, n_sel)
        top_valid = jnp.isfinite(top_vals)
    bi = jnp.arange(B)[:, None, None, None]
    hi = jnp.arange(H)[None, :, None, None]

    def step(c):
        start = c * MOBA_Q_CHUNK
        qc = lax.dynamic_slice_in_dim(q, start, MOBA_Q_CHUNK, axis=2)
        b_own = start // MOBA_BLOCK
        ko = lax.dynamic_index_in_dim(kb, b_own, axis=2, keepdims=False)
        vo = lax.dynamic_index_in_dim(vb, b_own, axis=2, keepdims=False)
        qpos = start + jnp.arange(MOBA_Q_CHUNK)
        kpos = b_own * MOBA_BLOCK + jnp.arange(MOBA_BLOCK)
        s_own = jnp.einsum('bhid,bhjd->bhij', qc, ko) * scale
        s_own = jnp.where(kpos[None, :] <= qpos[:, None], s_own, -jnp.inf)
        if n_sel == 0:
            p = jax.nn.softmax(s_own, axis=-1)
            return jnp.einsum('bhij,bhjd->bhid', p, vo)
        ic = lax.dynamic_slice_in_dim(top_idx, start, MOBA_Q_CHUNK, axis=2)
        vc = lax.dynamic_slice_in_dim(top_valid, start, MOBA_Q_CHUNK, axis=2)
        kg = kb[bi, hi, ic]
        vg = vb[bi, hi, ic]
        s_sel = jnp.einsum('bhid,bhikjd->bhikj', qc, kg) * scale
        s_sel = jnp.where(vc[..., None], s_sel, -jnp.inf)
        s_all = jnp.concatenate([s_own, s_sel.reshape(B, H, MOBA_Q_CHUNK, n_sel * MOBA_BLOCK)], axis=-1)
        p = jax.nn.softmax(s_all, axis=-1)
        p_own = p[..., :MOBA_BLOCK]
        p_sel = p[..., MOBA_BLOCK:].reshape(B, H, MOBA_Q_CHUNK, n_sel, MOBA_BLOCK)
        return jnp.einsum('bhij,bhjd->bhid', p_own, vo) + jnp.einsum('bhikj,bhikjd->bhid', p_sel, vg)

    o = lax.map(step, jnp.arange(Sp // MOBA_Q_CHUNK))
    o = o.transpose(1, 2, 0, 3, 4).reshape(B, H, Sp, Dh)
    return o[:, :, :S]


def _layer_ab(x, norm_g, w_in, w_out, lam_vec, subln_g, layer_idx, pos):
    h = jnp.matmul(_rmsnorm(x, norm_g).astype(x.dtype), w_in).astype(jnp.float32)
    cuts = np.cumsum([A_WIDTH, A_WIDTH, A_WIDTH, B_WIDTH, B_WIDTH, B_WIDTH]).tolist()
    aq, ak, av, bq, bk, bv, z = jnp.split(h, cuts, axis=-1)
    ya = _dilated_mixture(_rope(_heads(aq, A_HEADS), pos), _rope(_heads(ak, A_HEADS), pos), _heads(av, A_HEADS))
    ya = _merge(ya)
    Bn, S = x.shape[0], x.shape[1]
    bq = _rope(bq.reshape(Bn, S, B_HEADS, 2, HEAD_DIM).transpose(0, 2, 3, 1, 4), pos)
    bk = _rope(bk.reshape(Bn, S, B_HEADS, 2, HEAD_DIM).transpose(0, 2, 3, 1, 4), pos)
    bv = _heads(bv, B_HEADS)
    lam_vec = lam_vec.astype(jnp.float32)
    lam_init = 0.8 - 0.6 * math.exp(-0.3 * layer_idx)
    lam = jnp.exp(jnp.sum(lam_vec[0] * lam_vec[1])) - jnp.exp(jnp.sum(lam_vec[2] * lam_vec[3])) + lam_init
    yb = _diff_attention(bq[:, :, 0], bq[:, :, 1], bk[:, :, 0], bk[:, :, 1], bv, lam)
    yb = _rmsnorm(yb, subln_g) * (1.0 - lam_init)
    yb = _merge(yb)
    y = jnp.concatenate([ya, yb], axis=-1) * jax.nn.silu(z)
    return x + jnp.matmul(y.astype(x.dtype), w_out)


def _layer_c(x, norm_g, w_in, w_out, pos):
    h = jnp.matmul(_rmsnorm(x, norm_g).astype(x.dtype), w_in).astype(jnp.float32)
    cq, ck, cv, z = jnp.split(h, [C_WIDTH, 2 * C_WIDTH, 3 * C_WIDTH], axis=-1)
    yc = _moba_attention(_rope(_heads(cq, C_HEADS), pos), _rope(_heads(ck, C_HEADS), pos), _heads(cv, C_HEADS))
    y = _merge(yc) * jax.nn.silu(z)
    return x + jnp.matmul(y.astype(x.dtype), w_out)


def setup_inputs(seed: int = 0) -> dict:
    key = jax.random.key(seed)
    ks = jax.random.split(key, 10)
    f32 = jnp.float32
    x = jax.random.normal(ks[0], (BATCH, SEQ, D_MODEL), f32)
    norm_ab = 1.0 + 0.02 * jax.random.normal(ks[1], (N_AB, D_MODEL), f32)
    w_in_ab = jax.random.normal(ks[2], (N_AB, D_MODEL, AB_IN), f32) * D_MODEL ** -0.5
    w_out_ab = jax.random.normal(ks[3], (N_AB, A_WIDTH + B_WIDTH, D_MODEL), f32) * (A_WIDTH + B_WIDTH) ** -0.5
    lam_ab = 0.1 * jax.random.normal(ks[4], (N_AB, 4, HEAD_DIM), f32)
    subln_ab = 1.0 + 0.02 * jax.random.normal(ks[5], (N_AB, B_VDIM), f32)
    norm_c = 1.0 + 0.02 * jax.random.normal(ks[6], (N_C, D_MODEL), f32)
    w_in_c = jax.random.normal(ks[7], (N_C, D_MODEL, C_IN), f32) * D_MODEL ** -0.5
    w_out_c = jax.random.normal(ks[8], (N_C, C_WIDTH, D_MODEL), f32) * C_WIDTH ** -0.5
    final_norm = 1.0 + 0.02 * jax.random.normal(ks[9], (D_MODEL,), f32)
    return {"x": x, "norm_ab": norm_ab, "w_in_ab": w_in_ab, "w_out_ab": w_out_ab,
            "lam_ab": lam_ab, "subln_ab": subln_ab, "norm_c": norm_c, "w_in_c": w_in_c,
            "w_out_c": w_out_c, "final_norm": final_norm}


def reference(x, norm_ab, w_in_ab, w_out_ab, lam_ab, subln_ab, norm_c, w_in_c, w_out_c, final_norm):
    pos = jnp.arange(x.shape[1], dtype=jnp.int32)
    h = x
    for i in range(DEPTH):
        j = i // 2
        if i % 2 == 0:
            h = _layer_ab(h, norm_ab[j], w_in_ab[j], w_out_ab[j], lam_ab[j], subln_ab[j], i, pos)
        else:
            h = _layer_c(h, norm_c[j], w_in_c[j], w_out_c[j], pos)
    return _rmsnorm(h, final_norm).astype(x.dtype)
```

```python
import functools
import math

import numpy as np
import jax
import jax.numpy as jnp
from jax import lax
from jax.experimental import pallas as pl
from jax.experimental.pallas import tpu as pltpu

D_MODEL = 2048
SEQ = 2048
HEAD_DIM = 128
ROPE_THETA = 10000.0
NORM_EPS = 1e-6
A_HEADS = 8
DIL_CONFIGS = ((128, 1), (512, 4), (2048, 16))
B_HEADS = 4
B_VDIM = 2 * HEAD_DIM
C_HEADS = 16
MOBA_BLOCK = 256
MOBA_TOPK = 3
N_MOBA_BLOCKS = SEQ // MOBA_BLOCK
IN_WIDTH = 8192
N_COL_BLOCKS = IN_WIDTH // HEAD_DIM
SCALE = HEAD_DIM ** -0.5
NEG = -1e30

BF16 = jnp.bfloat16
F32 = jnp.float32

KIND_PLAIN, KIND_ROPE, KIND_ROPE_SCALED = 0, 1, 2

VMEM_LIMIT = 56 * 1024 * 1024


def _nt_dot(a, b):
    return lax.dot_general(a, b, (((1,), (1,)), ((), ())), preferred_element_type=F32)


IN_TM, IN_TN = 1024, 512


def _inproj_kernel(kinds_ref, x_ref, g_ref, w_ref, cos_ref, sin_ref, o_ref, xn_ref):
    j = pl.program_id(1)

    @pl.when(j == 0)
    def _():
        x = x_ref[...]
        ms = jnp.mean(x * x, axis=-1, keepdims=True)
        xn_ref[...] = (x * lax.rsqrt(ms + NORM_EPS) * g_ref[...]).astype(BF16)

    acc = jnp.dot(xn_ref[...], w_ref[...], preferred_element_type=F32)
    kind = kinds_ref[j]
    n_blk = o_ref.shape[0]

    @pl.when(kind == KIND_PLAIN)
    def _():
        for c in range(n_blk):
            o_ref[c] = acc[:, c * HEAD_DIM:(c + 1) * HEAD_DIM].astype(BF16)

    @pl.when(kind != KIND_PLAIN)
    def _():
        scale = jnp.where(kind == KIND_ROPE_SCALED, SCALE, 1.0).astype(F32)
        cos = cos_ref[...] * scale
        sin = sin_ref[...] * scale
        for c in range(n_blk):
            t = acc[:, c * HEAD_DIM:(c + 1) * HEAD_DIM]
            r = t * cos + pltpu.roll(t, HEAD_DIM // 2, 1) * sin
            o_ref[c] = r.astype(BF16)


def _inproj(x2d, g, w_bf16, cos, sin_signed, kinds, batch):
    rows = x2d.shape[0]
    s_tiles = SEQ // IN_TM
    grid = (rows // IN_TM, IN_WIDTH // IN_TN)
    blk_per_tile = IN_TN // HEAD_DIM
    return pl.pallas_call(
        _inproj_kernel,
        out_shape=jax.ShapeDtypeStruct((batch, N_COL_BLOCKS, SEQ, HEAD_DIM), BF16),
        grid_spec=pltpu.PrefetchScalarGridSpec(
            num_scalar_prefetch=1,
            grid=grid,
            in_specs=[
                pl.BlockSpec((IN_TM, D_MODEL), lambda i, j, k: (i, 0)),
                pl.BlockSpec((1, D_MODEL), lambda i, j, k: (0, 0)),
                pl.BlockSpec((D_MODEL, IN_TN), lambda i, j, k: (0, j)),
                pl.BlockSpec((IN_TM, HEAD_DIM), lambda i, j, k: (i % s_tiles, 0)),
                pl.BlockSpec((IN_TM, HEAD_DIM), lambda i, j, k: (i % s_tiles, 0)),
            ],
            out_specs=pl.BlockSpec((None, blk_per_tile, IN_TM, HEAD_DIM),
                                   lambda i, j, k: (i // s_tiles, j, i % s_tiles, 0)),
            scratch_shapes=[pltpu.VMEM((IN_TM, D_MODEL), BF16)],
        ),
        compiler_params=pltpu.CompilerParams(
            dimension_semantics=("parallel", "arbitrary"),
            vmem_limit_bytes=VMEM_LIMIT),
        name="norm_inproj",
    )(kinds, x2d, g, w_bf16, cos, sin_signed)


A_TQ = A_TK = 256
A_N_BIAS = 4


def _dilated_bias_tables():
    r = np.arange(A_TQ)[:, None]
    c = np.arange(A_TK)[None, :]
    tabs = []
    for off in range(A_N_BIAS):
        d = off * A_TK + r - c
        mult = np.zeros(d.shape, np.float64)
        for window, dil in DIL_CONFIGS:
            mult += (d >= 0) & (d % dil == 0) & (d <= window)
        tabs.append(np.where(mult > 0, np.log(np.maximum(mult, 1.0)), NEG))
    assert (A_N_BIAS - 1) * A_TK - (A_TQ - 1) > DIL_CONFIGS[1][0] and A_TK % DIL_CONFIGS[2][1] == 0
    assert DIL_CONFIGS[2][0] >= SEQ
    return jnp.asarray(np.stack(tabs), F32)


def _attn_a_kernel(q_ref, k_ref, v_ref, bias_ref, o_ref):
    i = pl.program_id(2)
    q = q_ref[...]

    def body(j, carry):
        m, l, acc = carry
        kj = k_ref[pl.ds(pl.multiple_of(j * A_TK, A_TK), A_TK), :]
        vj = v_ref[pl.ds(pl.multiple_of(j * A_TK, A_TK), A_TK), :]
        s = _nt_dot(q, kj) + bias_ref[jnp.minimum(i - j, A_N_BIAS - 1)]
        m_new = jnp.maximum(m, jnp.max(s, axis=-1, keepdims=True))
        alpha = jnp.exp(m - m_new)
        p = jnp.exp(s - m_new)
        l = alpha * l + jnp.sum(p, axis=-1, keepdims=True)
        acc = alpha * acc + jnp.dot(p.astype(BF16), vj, preferred_element_type=F32)
        return m_new, l, acc

    init = (jnp.full((A_TQ, 1), NEG, F32), jnp.zeros((A_TQ, 1), F32),
            jnp.zeros((A_TQ, HEAD_DIM), F32))
    _, l, acc = lax.fori_loop(0, i + 1, body, init)
    o_ref[...] = (acc / l).astype(BF16)


def _attn_a(h_ab, bias):
    batch = h_ab.shape[0]
    grid = (batch, A_HEADS, SEQ // A_TQ)
    kv_spec = lambda off: pl.BlockSpec((None, None, SEQ, HEAD_DIM),
                                       lambda b, h, i: (b, off + h, 0, 0))
    return pl.pallas_call(
        _attn_a_kernel,
        out_shape=jax.ShapeDtypeStruct((batch, A_HEADS, SEQ, HEAD_DIM), BF16),
        grid=grid,
        in_specs=[
            pl.BlockSpec((None, None, A_TQ, HEAD_DIM), lambda b, h, i: (b, h, i, 0)),
            kv_spec(A_HEADS),
            kv_spec(2 * A_HEADS),
            pl.BlockSpec((A_N_BIAS, A_TQ, A_TK), lambda b, h, i: (0, 0, 0)),
        ],
        out_specs=pl.BlockSpec((None, None, A_TQ, HEAD_DIM), lambda b, h, i: (b, h, i, 0)),
        compiler_params=pltpu.CompilerParams(
            dimension_semantics=("parallel", "parallel", "arbitrary"),
            vmem_limit_bytes=VMEM_LIMIT),
        name="dilated_attn",
    )(h_ab, h_ab, h_ab, bias)


B_TQ = B_TK = 256
B_Q_OFF, B_K_OFF, B_V_OFF = 12, 16, 20


def _attn_b_kernel(lam_ref, g_ref, q_ref, k_ref, v_ref, o_ref, *, lam_init):
    i = pl.program_id(2)
    q1 = q_ref[0]
    q2 = q_ref[1]

    def step(j, carry, diagonal):
        m1, l1, a1, m2, l2, a2 = carry
        rows = pl.ds(pl.multiple_of(j * B_TK, B_TK), B_TK)
        v = jnp.concatenate([v_ref[0, rows, :], v_ref[1, rows, :]], axis=-1)
        s1 = _nt_dot(q1, k_ref[0, rows, :])
        s2 = _nt_dot(q2, k_ref[1, rows, :])
        if diagonal:
            r = lax.broadcasted_iota(jnp.int32, (B_TQ, B_TK), 0)
            c = lax.broadcasted_iota(jnp.int32, (B_TQ, B_TK), 1)
            s1 = jnp.where(c <= r, s1, NEG)
            s2 = jnp.where(c <= r, s2, NEG)

        def upd(s, m, l, a):
            m_new = jnp.maximum(m, jnp.max(s, axis=-1, keepdims=True))
            alpha = jnp.exp(m - m_new)
            p = jnp.exp(s - m_new)
            l = alpha * l + jnp.sum(p, axis=-1, keepdims=True)
            a = alpha * a + jnp.dot(p.astype(BF16), v, preferred_element_type=F32)
            return m_new, l, a

        m1, l1, a1 = upd(s1, m1, l1, a1)
        m2, l2, a2 = upd(s2, m2, l2, a2)
        return m1, l1, a1, m2, l2, a2

    def init():
        return (jnp.full((B_TQ, 1), NEG, F32), jnp.zeros((B_TQ, 1), F32),
                jnp.zeros((B_TQ, B_VDIM), F32))

    carry = lax.fori_loop(0, i, lambda j, c: step(j, c, False), init() + init())
    _, l1, a1, _, l2, a2 = step(i, carry, True)

    lam_v = lam_ref[...]
    lam = (jnp.exp(jnp.sum(lam_v[0:1] * lam_v[1:2], axis=-1, keepdims=True))
           - jnp.exp(jnp.sum(lam_v[2:3] * lam_v[3:4], axis=-1, keepdims=True)) + lam_init)
    o = a1 / l1 - lam * (a2 / l2)
    ms = jnp.mean(o * o, axis=-1, keepdims=True)
    y = o * lax.rsqrt(ms + NORM_EPS) * g_ref[...] * (1.0 - lam_init)
    o_ref[0] = y[:, :HEAD_DIM].astype(BF16)
    o_ref[1] = y[:, HEAD_DIM:].astype(BF16)


def _attn_b(h_ab, lam_vec, subln_g, lam_init):
    batch = h_ab.shape[0]
    grid = (batch, B_HEADS, SEQ // B_TQ)
    kv_spec = lambda off: pl.BlockSpec((None, 2, SEQ, HEAD_DIM),
                                       lambda b, h, i: (b, off + h, 0, 0))
    return pl.pallas_call(
        functools.partial(_attn_b_kernel, lam_init=lam_init),
        out_shape=jax.ShapeDtypeStruct((batch, 2 * B_HEADS, SEQ, HEAD_DIM), BF16),
        grid=grid,
        in_specs=[
            pl.BlockSpec((4, HEAD_DIM), lambda b, h, i: (0, 0)),
            pl.BlockSpec((1, B_VDIM), lambda b, h, i: (0, 0)),
            pl.BlockSpec((None, 2, B_TQ, HEAD_DIM), lambda b, h, i: (b, B_Q_OFF + h, i, 0)),
            kv_spec(B_K_OFF),
            kv_spec(B_V_OFF),
        ],
        out_specs=pl.BlockSpec((None, 2, B_TQ, HEAD_DIM), lambda b, h, i: (b, h, i, 0)),
        compiler_params=pltpu.CompilerParams(
            dimension_semantics=("parallel", "parallel", "arbitrary"),
            vmem_limit_bytes=VMEM_LIMIT),
        name="diff_attn",
    )(lam_vec, subln_g, h_ab, h_ab, h_ab)


C_T = MOBA_BLOCK
KM_ROWS = 16


def _attn_c_kernel(q_ref, k_ref, v_ref, o_ref, kmean_ref, m_ref, l_ref, acc_ref):
    i = pl.program_id(2)
    q = q_ref[...]

    @pl.when(i == 0)
    def _():
        kmean_ref[...] = jnp.zeros_like(kmean_ref)
        for n in range(N_MOBA_BLOCKS):
            kb = k_ref[n * C_T:(n + 1) * C_T, :].astype(F32)
            kmean_ref[n:n + 1, :] = jnp.mean(kb, axis=0, keepdims=True)

    km = kmean_ref[...]
    km_hi = km.astype(BF16)
    km_lo = (km - km_hi.astype(F32)).astype(BF16)
    gate = _nt_dot(km_hi, q) + _nt_dot(km_lo, q)
    row = lax.broadcasted_iota(jnp.int32, gate.shape, 0)
    past = row < i
    gate = jnp.where(past, gate, -jnp.inf)
    rank = jnp.zeros(gate.shape, F32)
    for j in range(N_MOBA_BLOCKS):
        gj = gate[j:j + 1, :]
        beats = (gj > gate) | ((gj == gate) & (j < row))
        rank = rank + beats.astype(F32)
    sel_t = (past & (rank < MOBA_TOPK)).astype(F32)
    sel_t = jnp.concatenate([sel_t, jnp.zeros((HEAD_DIM - KM_ROWS, C_T), F32)], axis=0)
    sel = sel_t.T

    own = pl.ds(pl.multiple_of(i * C_T, C_T), C_T)
    s = _nt_dot(q, k_ref[own, :])
    r = lax.broadcasted_iota(jnp.int32, (C_T, C_T), 0)
    c = lax.broadcasted_iota(jnp.int32, (C_T, C_T), 1)
    s = jnp.where(c <= r, s, NEG)
    m0 = jnp.max(s, axis=-1, keepdims=True)
    p = jnp.exp(s - m0)
    m_ref[...] = m0
    l_ref[...] = jnp.sum(p, axis=-1, keepdims=True)
    acc_ref[...] = jnp.dot(p.astype(BF16), v_ref[own, :], preferred_element_type=F32)

    for n in range(N_MOBA_BLOCKS - 1):
        @pl.when(n < i)
        def _():
            rows = pl.ds(n * C_T, C_T)
            s = _nt_dot(q, k_ref[rows, :])
            s = jnp.where(sel[:, n:n + 1] > 0.5, s, NEG)
            m_prev = m_ref[...]
            m_new = jnp.maximum(m_prev, jnp.max(s, axis=-1, keepdims=True))
            alpha = jnp.exp(m_prev - m_new)
            p = jnp.exp(s - m_new)
            l_ref[...] = alpha * l_ref[...] + jnp.sum(p, axis=-1, keepdims=True)
            acc_ref[...] = alpha * acc_ref[...] + jnp.dot(
                p.astype(BF16), v_ref[rows, :], preferred_element_type=F32)
            m_ref[...] = m_new

    o_ref[...] = (acc_ref[...] / l_ref[...]).astype(BF16)


def _attn_c(h_c):
    batch = h_c.shape[0]
    grid = (batch, C_HEADS, N_MOBA_BLOCKS)
    kv_spec = lambda off: pl.BlockSpec((None, None, SEQ, HEAD_DIM),
                                       lambda b, h, i: (b, off + h, 0, 0))
    return pl.pallas_call(
        _attn_c_kernel,
        out_shape=jax.ShapeDtypeStruct((batch, C_HEADS, SEQ, HEAD_DIM), BF16),
        grid=grid,
        in_specs=[
            pl.BlockSpec((None, None, C_T, HEAD_DIM), lambda b, h, i: (b, h, i, 0)),
            kv_spec(C_HEADS),
            kv_spec(2 * C_HEADS),
        ],
        out_specs=pl.BlockSpec((None, None, C_T, HEAD_DIM), lambda b, h, i: (b, h, i, 0)),
        scratch_shapes=[
            pltpu.VMEM((KM_ROWS, HEAD_DIM), F32),
            pltpu.VMEM((C_T, 1), F32),
            pltpu.VMEM((C_T, 1), F32),
            pltpu.VMEM((C_T, HEAD_DIM), F32),
        ],
        compiler_params=pltpu.CompilerParams(
            dimension_semantics=("parallel", "parallel", "arbitrary"),
            vmem_limit_bytes=VMEM_LIMIT),
        name="moba_attn",
    )(h_c, h_c, h_c)


OUT_TM, OUT_TN = 512, 1024
Z_BLOCK_OFF = 3


def _outproj_kernel(*refs, n_y):
    y_refs = refs[:n_y]
    z_ref, x_ref, w_ref, o_ref, ysc_ref = refs[n_y:]
    j = pl.program_id(1)

    @pl.when(j == 0)
    def _():
        c = 0
        for y_ref in y_refs:
            for hh in range(y_ref.shape[0]):
                z = z_ref[c].astype(F32)
                gate = z / (1.0 + jnp.exp(-z))
                ysc_ref[:, c * HEAD_DIM:(c + 1) * HEAD_DIM] = (
                    y_ref[hh].astype(F32) * gate).astype(BF16)
                c += 1

    o_ref[...] = x_ref[...] + jnp.dot(ysc_ref[...], w_ref[...], preferred_element_type=F32)


def _outproj(ys, h_in, x2d, w_bf16, batch):
    rows = x2d.shape[0]
    s_tiles = SEQ // OUT_TM
    grid = (rows // OUT_TM, D_MODEL // OUT_TN)
    n_z = D_MODEL // HEAD_DIM
    y_specs = [pl.BlockSpec((None, y.shape[1], OUT_TM, HEAD_DIM),
                            lambda i, j: (i // s_tiles, 0, i % s_tiles, 0)) for y in ys]
    assert sum(y.shape[1] for y in ys) == n_z
    return pl.pallas_call(
        functools.partial(_outproj_kernel, n_y=len(ys)),
        out_shape=jax.ShapeDtypeStruct((rows, D_MODEL), F32),
        grid=grid,
        in_specs=y_specs + [
            pl.BlockSpec((None, n_z, OUT_TM, HEAD_DIM),
                         lambda i, j: (i // s_tiles, Z_BLOCK_OFF, i % s_tiles, 0)),
            pl.BlockSpec((OUT_TM, OUT_TN), lambda i, j: (i, j)),
            pl.BlockSpec((D_MODEL, OUT_TN), lambda i, j: (0, j)),
        ],
        out_specs=pl.BlockSpec((OUT_TM, OUT_TN), lambda i, j: (i, j)),
        scratch_shapes=[pltpu.VMEM((OUT_TM, D_MODEL), BF16)],
        compiler_params=pltpu.CompilerParams(
            dimension_semantics=("parallel", "arbitrary"),
            vmem_limit_bytes=VMEM_LIMIT),
        name="gate_outproj",
    )(*ys, h_in, x2d, w_bf16)


FN_TM = 512


def _final_norm_kernel(x_ref, g_ref, o_ref):
    x = x_ref[...]
    ms = jnp.mean(x * x, axis=-1, keepdims=True)
    o_ref[...] = x * lax.rsqrt(ms + NORM_EPS) * g_ref[...]


def _final_norm(x2d, g):
    rows = x2d.shape[0]
    return pl.pallas_call(
        _final_norm_kernel,
        out_shape=jax.ShapeDtypeStruct(x2d.shape, F32),
        grid=(rows // FN_TM,),
        in_specs=[pl.BlockSpec((FN_TM, D_MODEL), lambda i: (i, 0)),
                  pl.BlockSpec((1, D_MODEL), lambda i: (0, 0))],
        out_specs=pl.BlockSpec((FN_TM, D_MODEL), lambda i: (i, 0)),
        compiler_params=pltpu.CompilerParams(dimension_semantics=("parallel",)),
        name="final_norm",
    )(x2d, g)


def _rope_tables():
    inv = 1.0 / (ROPE_THETA ** (jnp.arange(0, HEAD_DIM, 2, dtype=F32) / HEAD_DIM))
    ang = jnp.arange(SEQ, dtype=F32)[:, None] * inv[None, :]
    cos = jnp.concatenate([jnp.cos(ang), jnp.cos(ang)], axis=-1)
    sin = jnp.concatenate([-jnp.sin(ang), jnp.sin(ang)], axis=-1)
    return cos, sin


def _tile_kinds(group_kinds, cols_per_group):
    kinds = []
    for kind, cols in zip(group_kinds, cols_per_group):
        assert cols % IN_TN == 0
        kinds += [kind] * (cols // IN_TN)
    assert len(kinds) == IN_WIDTH // IN_TN
    return jnp.asarray(kinds, jnp.int32)


def kernel(x, norm_ab, w_in_ab, w_out_ab, lam_ab, subln_ab, norm_c, w_in_c, w_out_c, final_norm):
    batch, seq, d = x.shape
    assert (seq, d) == (SEQ, D_MODEL) and x.dtype == F32
    assert w_in_ab.shape == (1, D_MODEL, IN_WIDTH) and w_in_c.shape == (1, D_MODEL, IN_WIDTH)
    cos, sin = _rope_tables()
    aw, bw = A_HEADS * HEAD_DIM, B_HEADS * B_VDIM
    kinds_ab = _tile_kinds(
        [KIND_ROPE_SCALED, KIND_ROPE, KIND_PLAIN, KIND_ROPE_SCALED, KIND_ROPE, KIND_PLAIN, KIND_PLAIN],
        [aw, aw, aw, bw, bw, bw, aw + bw])
    cw = C_HEADS * HEAD_DIM
    kinds_c = _tile_kinds([KIND_ROPE_SCALED, KIND_ROPE, KIND_PLAIN, KIND_PLAIN], [cw, cw, cw, cw])

    x2d = x.reshape(batch * seq, d)

    h_ab = _inproj(x2d, norm_ab[0][None, :], w_in_ab[0].astype(BF16), cos, sin, kinds_ab, batch)
    ya = _attn_a(h_ab, _dilated_bias_tables())
    lam_init = 0.8 - 0.6 * math.exp(-0.3 * 0)
    yb = _attn_b(h_ab, lam_ab[0], subln_ab[0][None, :], lam_init)
    x1 = _outproj([ya, yb], h_ab, x2d, w_out_ab[0].astype(BF16), batch)

    h_c = _inproj(x1, norm_c[0][None, :], w_in_c[0].astype(BF16), cos, sin, kinds_c, batch)
    yc = _attn_c(h_c)
    x2 = _outproj([yc], h_c, x1, w_out_c[0].astype(BF16), batch)

    return _final_norm(x2, final_norm[None, :]).reshape(batch, seq, d)
```

```python
import functools
import math

import numpy as np
import jax
import jax.numpy as jnp
from jax import lax
from jax.experimental import pallas as pl
from jax.experimental.pallas import tpu as pltpu

D_MODEL = 2048
SEQ = 2048
HEAD_DIM = 128
ROPE_THETA = 10000.0
NORM_EPS = 1e-6
A_HEADS = 8
DIL_CONFIGS = ((128, 1), (512, 4), (2048, 16))
B_HEADS = 4
B_VDIM = 2 * HEAD_DIM
C_HEADS = 16
MOBA_BLOCK = 256
MOBA_TOPK = 3
N_MOBA_BLOCKS = SEQ // MOBA_BLOCK
IN_WIDTH = 8192
N_COL_BLOCKS = IN_WIDTH // HEAD_DIM
SCALE = HEAD_DIM ** -0.5
Q_SCALE = SCALE * math.log2(math.e)
NEG = -1e30

BF16 = jnp.bfloat16
F32 = jnp.float32

KIND_PLAIN, KIND_ROPE, KIND_ROPE_SCALED = 0, 1, 2

VMEM_LIMIT = 56 * 1024 * 1024


def _nt_dot(a, b):
    return lax.dot_general(a, b, (((1,), (1,)), ((), ())), preferred_element_type=F32)


IN_TM, IN_TN = 1024, 512


def _inproj_kernel(kinds_ref, x_ref, g_ref, w_ref, cos_ref, sin_ref, o_ref, xn_ref):
    j = pl.program_id(1)

    @pl.when(j == 0)
    def _():
        x = x_ref[...]
        ms = jnp.mean(x * x, axis=-1, keepdims=True)
        xn_ref[...] = (x * lax.rsqrt(ms + NORM_EPS) * g_ref[...]).astype(BF16)

    acc = jnp.dot(xn_ref[...], w_ref[...], preferred_element_type=F32)
    kind = kinds_ref[j]
    n_blk = o_ref.shape[0]

    @pl.when(kind == KIND_PLAIN)
    def _():
        for c in range(n_blk):
            o_ref[c] = acc[:, c * HEAD_DIM:(c + 1) * HEAD_DIM].astype(BF16)

    @pl.when(kind != KIND_PLAIN)
    def _():
        scale = jnp.where(kind == KIND_ROPE_SCALED, Q_SCALE, 1.0).astype(F32)
        cos = cos_ref[...] * scale
        sin = sin_ref[...] * scale
        for c in range(n_blk):
            t = acc[:, c * HEAD_DIM:(c + 1) * HEAD_DIM]
            r = t * cos + pltpu.roll(t, HEAD_DIM // 2, 1) * sin
            o_ref[c] = r.astype(BF16)


def _inproj(x2d, g, w_bf16, cos, sin_signed, kinds, batch):
    rows = x2d.shape[0]
    s_tiles = SEQ // IN_TM
    grid = (rows // IN_TM, IN_WIDTH // IN_TN)
    blk_per_tile = IN_TN // HEAD_DIM
    return pl.pallas_call(
        _inproj_kernel,
        out_shape=jax.ShapeDtypeStruct((batch, N_COL_BLOCKS, SEQ, HEAD_DIM), BF16),
        grid_spec=pltpu.PrefetchScalarGridSpec(
            num_scalar_prefetch=1,
            grid=grid,
            in_specs=[
                pl.BlockSpec((IN_TM, D_MODEL), lambda i, j, k: (i, 0)),
                pl.BlockSpec((1, D_MODEL), lambda i, j, k: (0, 0)),
                pl.BlockSpec((D_MODEL, IN_TN), lambda i, j, k: (0, j)),
                pl.BlockSpec((IN_TM, HEAD_DIM), lambda i, j, k: (i % s_tiles, 0)),
                pl.BlockSpec((IN_TM, HEAD_DIM), lambda i, j, k: (i % s_tiles, 0)),
            ],
            out_specs=pl.BlockSpec((None, blk_per_tile, IN_TM, HEAD_DIM),
                                   lambda i, j, k: (i // s_tiles, j, i % s_tiles, 0)),
            scratch_shapes=[pltpu.VMEM((IN_TM, D_MODEL), BF16)],
        ),
        compiler_params=pltpu.CompilerParams(
            dimension_semantics=("parallel", "arbitrary"),
            vmem_limit_bytes=VMEM_LIMIT),
        name="norm_inproj",
    )(kinds, x2d, g, w_bf16, cos, sin_signed)


TQ = 256
N_QT = SEQ // TQ


def _softmax_pv(s, v):
    m = jnp.max(s, axis=-1, keepdims=True)
    p = jnp.exp2(s - m)
    l = jnp.sum(p, axis=-1, keepdims=True)
    o = jnp.dot(p.astype(BF16), v, preferred_element_type=F32)
    return o, 1.0 / l


def _causal_mask():
    r = lax.broadcasted_iota(jnp.int32, (TQ, TQ), 0)
    c = lax.broadcasted_iota(jnp.int32, (TQ, TQ), 1)
    return c <= r


def _dilated_bias_table():
    r = np.arange(TQ)[:, None]
    c = np.arange(SEQ)[None, :]
    d = (SEQ - TQ) + r - c
    mult = np.zeros(d.shape, np.float64)
    for window, dil in DIL_CONFIGS:
        mult += (d >= 0) & (d % dil == 0) & (d <= window)
    return jnp.asarray(np.where(mult > 0, np.log2(np.maximum(mult, 1.0)), NEG), F32)


def _attn_a_kernel(q_ref, k_ref, v_ref, bias_ref, o_ref):
    for i in range(N_QT):
        ext = (i + 1) * TQ
        rows = slice(i * TQ, ext)
        s = _nt_dot(q_ref[rows, :], k_ref[0:ext, :]) + bias_ref[:, SEQ - ext:SEQ]
        o, inv_l = _softmax_pv(s, v_ref[0:ext, :])
        o_ref[rows, :] = (o * inv_l).astype(BF16)


def _head_spec(off):
    return pl.BlockSpec((None, None, SEQ, HEAD_DIM), lambda b, h: (b, off + h, 0, 0))


def _attn_a(h_ab, bias):
    batch = h_ab.shape[0]
    return pl.pallas_call(
        _attn_a_kernel,
        out_shape=jax.ShapeDtypeStruct((batch, A_HEADS, SEQ, HEAD_DIM), BF16),
        grid=(batch, A_HEADS),
        in_specs=[_head_spec(0), _head_spec(A_HEADS), _head_spec(2 * A_HEADS),
                  pl.BlockSpec((TQ, SEQ), lambda b, h: (0, 0))],
        out_specs=_head_spec(0),
        compiler_params=pltpu.CompilerParams(
            dimension_semantics=("parallel", "parallel"),
            vmem_limit_bytes=VMEM_LIMIT),
        name="dilated_attn",
    )(h_ab, h_ab, h_ab, bias)


B_Q_OFF, B_K_OFF, B_V_OFF = 12, 16, 20


def _attn_b_kernel(lam_ref, g_ref, q_ref, k_ref, v_ref, o_ref, vcat_ref, *, lam_init):
    vcat_ref[:, :HEAD_DIM] = v_ref[0]
    vcat_ref[:, HEAD_DIM:] = v_ref[1]
    lam_v = lam_ref[...]
    lam = (jnp.exp(jnp.sum(lam_v[0:1] * lam_v[1:2], axis=-1, keepdims=True))
           - jnp.exp(jnp.sum(lam_v[2:3] * lam_v[3:4], axis=-1, keepdims=True)) + lam_init)
    causal = _causal_mask()
    for i in range(N_QT):
        ext = (i + 1) * TQ
        rows = slice(i * TQ, ext)
        v = vcat_ref[0:ext, :]
        outs = []
        for t in range(2):
            s = _nt_dot(q_ref[t, rows, :], k_ref[t, 0:ext, :])
            own = jnp.where(causal, s[:, i * TQ:], NEG)
            s = own if i == 0 else jnp.concatenate([s[:, :i * TQ], own], axis=1)
            o, inv_l = _softmax_pv(s, v)
            outs.append(o * inv_l)
        o = outs[0] - lam * outs[1]
        ms = jnp.mean(o * o, axis=-1, keepdims=True)
        y = o * lax.rsqrt(ms + NORM_EPS) * g_ref[...] * (1.0 - lam_init)
        o_ref[0, rows, :] = y[:, :HEAD_DIM].astype(BF16)
        o_ref[1, rows, :] = y[:, HEAD_DIM:].astype(BF16)


def _attn_b(h_ab, lam_vec, subln_g, lam_init):
    batch = h_ab.shape[0]
    pair_spec = lambda off: pl.BlockSpec((None, 2, SEQ, HEAD_DIM), lambda b, h: (b, off + h, 0, 0))
    return pl.pallas_call(
        functools.partial(_attn_b_kernel, lam_init=lam_init),
        out_shape=jax.ShapeDtypeStruct((batch, 2 * B_HEADS, SEQ, HEAD_DIM), BF16),
        grid=(batch, B_HEADS),
        in_specs=[
            pl.BlockSpec((4, HEAD_DIM), lambda b, h: (0, 0)),
            pl.BlockSpec((1, B_VDIM), lambda b, h: (0, 0)),
            pair_spec(B_Q_OFF), pair_spec(B_K_OFF), pair_spec(B_V_OFF),
        ],
        out_specs=pair_spec(0),
        scratch_shapes=[pltpu.VMEM((SEQ, B_VDIM), BF16)],
        compiler_params=pltpu.CompilerParams(
            dimension_semantics=("parallel", "parallel"),
            vmem_limit_bytes=VMEM_LIMIT),
        name="diff_attn",
    )(lam_vec, subln_g, h_ab, h_ab, h_ab)


KM_ROWS = 16


def _attn_c_kernel(q_ref, k_ref, v_ref, o_ref):
    assert TQ == MOBA_BLOCK
    row = lax.broadcasted_iota(jnp.int32, (KM_ROWS, SEQ), 0)
    blk = lax.broadcasted_iota(jnp.int32, (KM_ROWS, SEQ), 1) // MOBA_BLOCK
    pool = (row == blk).astype(BF16)
    km = jnp.dot(pool, k_ref[...], preferred_element_type=F32) * (1.0 / MOBA_BLOCK)
    km_hi = km.astype(BF16)
    km_lo = (km - km_hi.astype(F32)).astype(BF16)
    q_all = q_ref[...]
    gate = _nt_dot(km_hi, q_all) + _nt_dot(km_lo, q_all)
    past = row < blk
    gate = jnp.where(past, gate, -jnp.inf)
    rank = jnp.zeros(gate.shape, F32)
    for j in range(N_MOBA_BLOCKS - 1):
        gj = gate[j:j + 1, :]
        beats = (gj > gate) | ((gj == gate) & (j < row))
        rank = rank + beats.astype(F32)
    sel_t = (past & (rank < MOBA_TOPK)).astype(F32)
    pad_rows = jnp.zeros((HEAD_DIM - KM_ROWS, TQ), F32)
    causal = _causal_mask()

    for i in range(N_QT):
        ext = (i + 1) * TQ
        rows = slice(i * TQ, ext)
        s = _nt_dot(q_ref[rows, :], k_ref[0:ext, :])
        parts = []
        if i > 0:
            sel = jnp.concatenate([sel_t[:, rows], pad_rows], axis=0).T
            for n in range(i):
                parts.append(jnp.where(sel[:, n:n + 1] > 0.5, s[:, n * TQ:(n + 1) * TQ], NEG))
        parts.append(jnp.where(causal, s[:, i * TQ:], NEG))
        s = parts[0] if i == 0 else jnp.concatenate(parts, axis=1)
        o, inv_l = _softmax_pv(s, v_ref[0:ext, :])
        o_ref[rows, :] = (o * inv_l).astype(BF16)


def _attn_c(h_c):
    batch = h_c.shape[0]
    return pl.pallas_call(
        _attn_c_kernel,
        out_shape=jax.ShapeDtypeStruct((batch, C_HEADS, SEQ, HEAD_DIM), BF16),
        grid=(batch, C_HEADS),
        in_specs=[_head_spec(0), _head_spec(C_HEADS), _head_spec(2 * C_HEADS)],
        out_specs=_head_spec(0),
        compiler_params=pltpu.CompilerParams(
            dimension_semantics=("parallel", "parallel"),
            vmem_limit_bytes=VMEM_LIMIT),
        name="moba_attn",
    )(h_c, h_c, h_c)


OUT_TM, OUT_TN = 512, 1024
Z_BLOCK_OFF = 3


def _outproj_kernel(*refs, n_y):
    y_refs = refs[:n_y]
    z_ref, x_ref, w_ref, o_ref, ysc_ref = refs[n_y:]
    j = pl.program_id(1)

    @pl.when(j == 0)
    def _():
        c = 0
        for y_ref in y_refs:
            for hh in range(y_ref.shape[0]):
                z = z_ref[c].astype(F32)
                gate = z / (1.0 + jnp.exp(-z))
                ysc_ref[:, c * HEAD_DIM:(c + 1) * HEAD_DIM] = (
                    y_ref[hh].astype(F32) * gate).astype(BF16)
                c += 1

    o_ref[...] = x_ref[...] + jnp.dot(ysc_ref[...], w_ref[...], preferred_element_type=F32)


def _outproj(ys, h_in, x2d, w_bf16, batch):
    rows = x2d.shape[0]
    s_tiles = SEQ // OUT_TM
    grid = (rows // OUT_TM, D_MODEL // OUT_TN)
    n_z = D_MODEL // HEAD_DIM
    y_specs = [pl.BlockSpec((None, y.shape[1], OUT_TM, HEAD_DIM),
                            lambda i, j: (i // s_tiles, 0, i % s_tiles, 0)) for y in ys]
    assert sum(y.shape[1] for y in ys) == n_z
    return pl.pallas_call(
        functools.partial(_outproj_kernel, n_y=len(ys)),
        out_shape=jax.ShapeDtypeStruct((rows, D_MODEL), F32),
        grid=grid,
        in_specs=y_specs + [
            pl.BlockSpec((None, n_z, OUT_TM, HEAD_DIM),
                         lambda i, j: (i // s_tiles, Z_BLOCK_OFF, i % s_tiles, 0)),
            pl.BlockSpec((OUT_TM, OUT_TN), lambda i, j: (i, j)),
            pl.BlockSpec((D_MODEL, OUT_TN), lambda i, j: (0, j)),
        ],
        out_specs=pl.BlockSpec((OUT_TM, OUT_TN), lambda i, j: (i, j)),
        scratch_shapes=[pltpu.VMEM((OUT_TM, D_MODEL), BF16)],
        compiler_params=pltpu.CompilerParams(
            dimension_semantics=("parallel", "arbitrary"),
            vmem_limit_bytes=VMEM_LIMIT),
        name="gate_outproj",
    )(*ys, h_in, x2d, w_bf16)


FN_TM = 512


def _final_norm_kernel(x_ref, g_ref, o_ref):
    x = x_ref[...]
    ms = jnp.mean(x * x, axis=-1, keepdims=True)
    o_ref[...] = x * lax.rsqrt(ms + NORM_EPS) * g_ref[...]


def _final_norm(x2d, g):
    rows = x2d.shape[0]
    return pl.pallas_call(
        _final_norm_kernel,
        out_shape=jax.ShapeDtypeStruct(x2d.shape, F32),
        grid=(rows // FN_TM,),
        in_specs=[pl.BlockSpec((FN_TM, D_MODEL), lambda i: (i, 0)),
                  pl.BlockSpec((1, D_MODEL), lambda i: (0, 0))],
        out_specs=pl.BlockSpec((FN_TM, D_MODEL), lambda i: (i, 0)),
        compiler_params=pltpu.CompilerParams(dimension_semantics=("parallel",)),
        name="final_norm",
    )(x2d, g)


def _rope_tables():
    inv = 1.0 / (ROPE_THETA ** (jnp.arange(0, HEAD_DIM, 2, dtype=F32) / HEAD_DIM))
    ang = jnp.arange(SEQ, dtype=F32)[:, None] * inv[None, :]
    cos = jnp.concatenate([jnp.cos(ang), jnp.cos(ang)], axis=-1)
    sin = jnp.concatenate([-jnp.sin(ang), jnp.sin(ang)], axis=-1)
    return cos, sin


def _tile_kinds(group_kinds, cols_per_group):
    kinds = []
    for kind, cols in zip(group_kinds, cols_per_group):
        assert cols % IN_TN == 0
        kinds += [kind] * (cols // IN_TN)
    assert len(kinds) == IN_WIDTH // IN_TN
    return jnp.asarray(kinds, jnp.int32)


def kernel(x, norm_ab, w_in_ab, w_out_ab, lam_ab, subln_ab, norm_c, w_in_c, w_out_c, final_norm):
    batch, seq, d = x.shape
    assert (seq, d) == (SEQ, D_MODEL) and x.dtype == F32
    assert w_in_ab.shape == (1, D_MODEL, IN_WIDTH) and w_in_c.shape == (1, D_MODEL, IN_WIDTH)
    cos, sin = _rope_tables()
    aw, bw = A_HEADS * HEAD_DIM, B_HEADS * B_VDIM
    kinds_ab = _tile_kinds(
        [KIND_ROPE_SCALED, KIND_ROPE, KIND_PLAIN, KIND_ROPE_SCALED, KIND_ROPE, KIND_PLAIN, KIND_PLAIN],
        [aw, aw, aw, bw, bw, bw, aw + bw])
    cw = C_HEADS * HEAD_DIM
    kinds_c = _tile_kinds([KIND_ROPE_SCALED, KIND_ROPE, KIND_PLAIN, KIND_PLAIN], [cw, cw, cw, cw])

    x2d = x.reshape(batch * seq, d)

    h_ab = _inproj(x2d, norm_ab[0][None, :], w_in_ab[0].astype(BF16), cos, sin, kinds_ab, batch)
    ya = _attn_a(h_ab, _dilated_bias_table())
    lam_init = 0.8 - 0.6 * math.exp(-0.3 * 0)
    yb = _attn_b(h_ab, lam_ab[0], subln_ab[0][None, :], lam_init)
    x1 = _outproj([ya, yb], h_ab, x2d, w_out_ab[0].astype(BF16), batch)

    h_c = _inproj(x1, norm_c[0][None, :], w_in_c[0].astype(BF16), cos, sin, kinds_c, batch)
    yc = _attn_c(h_c)
    x2 = _outproj([yc], h_c, x1, w_out_c[0].astype(BF16), batch)

    return _final_norm(x2, final_norm[None, :]).reshape(batch, seq, d)
```

```python
import functools
import math

import numpy as np
import jax
import jax.numpy as jnp
from jax import lax
from jax.experimental import pallas as pl
from jax.experimental.pallas import tpu as pltpu

D_MODEL = 2048
SEQ = 2048
HEAD_DIM = 128
ROPE_THETA = 10000.0
NORM_EPS = 1e-6
A_HEADS = 8
DIL_CONFIGS = ((128, 1), (512, 4), (2048, 16))
B_HEADS = 4
B_VDIM = 2 * HEAD_DIM
C_HEADS = 16
MOBA_BLOCK = 256
MOBA_TOPK = 3
N_MOBA_BLOCKS = SEQ // MOBA_BLOCK
IN_WIDTH = 8192
N_COL_BLOCKS = IN_WIDTH // HEAD_DIM
SCALE = HEAD_DIM ** -0.5
Q_SCALE = SCALE * math.log2(math.e)
NEG = -1e30

BF16 = jnp.bfloat16
F32 = jnp.float32

KIND_PLAIN, KIND_ROPE, KIND_ROPE_SCALED = 0, 1, 2

VMEM_LIMIT = 56 * 1024 * 1024


def _nt_dot(a, b):
    return lax.dot_general(a, b, (((1,), (1,)), ((), ())), preferred_element_type=F32)


IN_TM, IN_TN = 1024, 512
IN_SUB_M, IN_SUB_N = 256, 256


def _inproj_kernel(kinds_ref, x_ref, g_ref, w_ref, cos_ref, sin_ref, o_ref, xn_ref):
    j = pl.program_id(1)

    @pl.when(j == 0)
    def _():
        x = x_ref[...]
        ms = jnp.mean(x * x, axis=-1, keepdims=True)
        xn_ref[...] = (x * lax.rsqrt(ms + NORM_EPS) * g_ref[...]).astype(BF16)

    kind = kinds_ref[j]
    scale = jnp.where(kind == KIND_ROPE_SCALED, Q_SCALE, 1.0).astype(F32)
    for nh in range(IN_TN // IN_SUB_N):
        w = w_ref[:, nh * IN_SUB_N:(nh + 1) * IN_SUB_N].astype(BF16)
        for mh in range(IN_TM // IN_SUB_M):
            rows = slice(mh * IN_SUB_M, (mh + 1) * IN_SUB_M)
            acc = jnp.dot(xn_ref[rows, :], w, preferred_element_type=F32)
            cos = jnp.where(kind == KIND_PLAIN, 1.0, cos_ref[rows, :] * scale)
            sin = jnp.where(kind == KIND_PLAIN, 0.0, sin_ref[rows, :] * scale)
            for c in range(IN_SUB_N // HEAD_DIM):
                t = acc[:, c * HEAD_DIM:(c + 1) * HEAD_DIM]
                o_ref[nh * (IN_SUB_N // HEAD_DIM) + c, rows, :] = (
                    t * cos + pltpu.roll(t, HEAD_DIM // 2, 1) * sin).astype(BF16)


def _inproj(x2d, g, w, cos, sin_signed, kinds, batch):
    rows = x2d.shape[0]
    s_tiles = SEQ // IN_TM
    grid = (rows // IN_TM, IN_WIDTH // IN_TN)
    blk_per_tile = IN_TN // HEAD_DIM
    return pl.pallas_call(
        _inproj_kernel,
        out_shape=jax.ShapeDtypeStruct((batch, N_COL_BLOCKS, SEQ, HEAD_DIM), BF16),
        grid_spec=pltpu.PrefetchScalarGridSpec(
            num_scalar_prefetch=1,
            grid=grid,
            in_specs=[
                pl.BlockSpec((IN_TM, D_MODEL), lambda i, j, k: (i, 0)),
                pl.BlockSpec((1, D_MODEL), lambda i, j, k: (0, 0)),
                pl.BlockSpec((D_MODEL, IN_TN), lambda i, j, k: (0, j)),
                pl.BlockSpec((IN_TM, HEAD_DIM), lambda i, j, k: (i % s_tiles, 0)),
                pl.BlockSpec((IN_TM, HEAD_DIM), lambda i, j, k: (i % s_tiles, 0)),
            ],
            out_specs=pl.BlockSpec((None, blk_per_tile, IN_TM, HEAD_DIM),
                                   lambda i, j, k: (i // s_tiles, j, i % s_tiles, 0)),
            scratch_shapes=[pltpu.VMEM((IN_TM, D_MODEL), BF16)],
        ),
        compiler_params=pltpu.CompilerParams(
            dimension_semantics=("parallel", "arbitrary"),
            vmem_limit_bytes=VMEM_LIMIT),
        name="norm_inproj",
    )(kinds, x2d, g, w, cos, sin_signed)


TQ = 256
N_QT = SEQ // TQ


def _softmax_pv(s, v):
    m = jnp.max(s, axis=-1, keepdims=True)
    p = jnp.exp2(s - m)
    l = jnp.sum(p, axis=-1, keepdims=True)
    o = jnp.dot(p.astype(BF16), v, preferred_element_type=F32)
    return o, 1.0 / l


def _causal_mask():
    r = lax.broadcasted_iota(jnp.int32, (TQ, TQ), 0)
    c = lax.broadcasted_iota(jnp.int32, (TQ, TQ), 1)
    return c <= r


def _dilated_bias_table():
    r = np.arange(TQ)[:, None]
    c = np.arange(SEQ)[None, :]
    d = (SEQ - TQ) + r - c
    mult = np.zeros(d.shape, np.float64)
    for window, dil in DIL_CONFIGS:
        mult += (d >= 0) & (d % dil == 0) & (d <= window)
    return jnp.asarray(np.where(mult > 0, np.log2(np.maximum(mult, 1.0)), NEG), F32)


def _attn_a_kernel(q_ref, k_ref, v_ref, bias_ref, o_ref):
    for i in range(N_QT):
        ext = (i + 1) * TQ
        rows = slice(i * TQ, ext)
        s = _nt_dot(q_ref[rows, :], k_ref[0:ext, :]) + bias_ref[:, SEQ - ext:SEQ]
        o, inv_l = _softmax_pv(s, v_ref[0:ext, :])
        o_ref[rows, :] = (o * inv_l).astype(BF16)


def _head_spec(off):
    return pl.BlockSpec((None, None, SEQ, HEAD_DIM), lambda b, h: (b, off + h, 0, 0))


def _attn_a(h_ab, bias):
    batch = h_ab.shape[0]
    return pl.pallas_call(
        _attn_a_kernel,
        out_shape=jax.ShapeDtypeStruct((batch, A_HEADS, SEQ, HEAD_DIM), BF16),
        grid=(batch, A_HEADS),
        in_specs=[_head_spec(0), _head_spec(A_HEADS), _head_spec(2 * A_HEADS),
                  pl.BlockSpec((TQ, SEQ), lambda b, h: (0, 0))],
        out_specs=_head_spec(0),
        compiler_params=pltpu.CompilerParams(
            dimension_semantics=("parallel", "parallel"),
            vmem_limit_bytes=VMEM_LIMIT),
        name="dilated_attn",
    )(h_ab, h_ab, h_ab, bias)


B_Q_OFF, B_K_OFF, B_V_OFF = 12, 16, 20


def _attn_b_kernel(lam_ref, g_ref, q_ref, k_ref, v_ref, o_ref, vcat_ref, *, lam_init):
    vcat_ref[:, :HEAD_DIM] = v_ref[0]
    vcat_ref[:, HEAD_DIM:] = v_ref[1]
    lam_v = lam_ref[...]
    lam = (jnp.exp(jnp.sum(lam_v[0:1] * lam_v[1:2], axis=-1, keepdims=True))
           - jnp.exp(jnp.sum(lam_v[2:3] * lam_v[3:4], axis=-1, keepdims=True)) + lam_init)
    causal = _causal_mask()
    for i in range(N_QT):
        ext = (i + 1) * TQ
        rows = slice(i * TQ, ext)
        v = vcat_ref[0:ext, :]
        outs = []
        for t in range(2):
            s = _nt_dot(q_ref[t, rows, :], k_ref[t, 0:ext, :])
            own = jnp.where(causal, s[:, i * TQ:], NEG)
            s = own if i == 0 else jnp.concatenate([s[:, :i * TQ], own], axis=1)
            o, inv_l = _softmax_pv(s, v)
            outs.append(o * inv_l)
        o = outs[0] - lam * outs[1]
        ms = jnp.mean(o * o, axis=-1, keepdims=True)
        y = o * lax.rsqrt(ms + NORM_EPS) * g_ref[...] * (1.0 - lam_init)
        o_ref[0, rows, :] = y[:, :HEAD_DIM].astype(BF16)
        o_ref[1, rows, :] = y[:, HEAD_DIM:].astype(BF16)


def _attn_b(h_ab, lam_vec, subln_g, lam_init):
    batch = h_ab.shape[0]
    pair_spec = lambda off: pl.BlockSpec((None, 2, SEQ, HEAD_DIM), lambda b, h: (b, off + h, 0, 0))
    return pl.pallas_call(
        functools.partial(_attn_b_kernel, lam_init=lam_init),
        out_shape=jax.ShapeDtypeStruct((batch, 2 * B_HEADS, SEQ, HEAD_DIM), BF16),
        grid=(batch, B_HEADS),
        in_specs=[
            pl.BlockSpec((4, HEAD_DIM), lambda b, h: (0, 0)),
            pl.BlockSpec((1, B_VDIM), lambda b, h: (0, 0)),
            pair_spec(B_Q_OFF), pair_spec(B_K_OFF), pair_spec(B_V_OFF),
        ],
        out_specs=pair_spec(0),
        scratch_shapes=[pltpu.VMEM((SEQ, B_VDIM), BF16)],
        compiler_params=pltpu.CompilerParams(
            dimension_semantics=("parallel", "parallel"),
            vmem_limit_bytes=VMEM_LIMIT),
        name="diff_attn",
    )(lam_vec, subln_g, h_ab, h_ab, h_ab)


KM_ROWS = 16


def _attn_c_kernel(q_ref, k_ref, v_ref, o_ref):
    assert TQ == MOBA_BLOCK
    row = lax.broadcasted_iota(jnp.int32, (KM_ROWS, SEQ), 0)
    blk = lax.broadcasted_iota(jnp.int32, (KM_ROWS, SEQ), 1) // MOBA_BLOCK
    pool = (row == blk).astype(BF16)
    km = jnp.dot(pool, k_ref[...], preferred_element_type=F32) * (1.0 / MOBA_BLOCK)
    km_hi = km.astype(BF16)
    km_lo = (km - km_hi.astype(F32)).astype(BF16)
    q_all = q_ref[...]
    gate = _nt_dot(km_hi, q_all) + _nt_dot(km_lo, q_all)
    past = row < blk
    gate = jnp.where(past, gate, -jnp.inf)
    rank = jnp.zeros(gate.shape, F32)
    for j in range(N_MOBA_BLOCKS - 1):
        gj = gate[j:j + 1, :]
        beats = (gj > gate) | ((gj == gate) & (j < row))
        rank = rank + beats.astype(F32)
    sel_t = (past & (rank < MOBA_TOPK)).astype(F32)
    pad_rows = jnp.zeros((HEAD_DIM - KM_ROWS, TQ), F32)
    causal = _causal_mask()

    for i in range(N_QT):
        ext = (i + 1) * TQ
        rows = slice(i * TQ, ext)
        s = _nt_dot(q_ref[rows, :], k_ref[0:ext, :])
        parts = []
        if i > 0:
            sel = jnp.concatenate([sel_t[:, rows], pad_rows], axis=0).T
            for n in range(i):
                parts.append(jnp.where(sel[:, n:n + 1] > 0.5, s[:, n * TQ:(n + 1) * TQ], NEG))
        parts.append(jnp.where(causal, s[:, i * TQ:], NEG))
        s = parts[0] if i == 0 else jnp.concatenate(parts, axis=1)
        o, inv_l = _softmax_pv(s, v_ref[0:ext, :])
        o_ref[rows, :] = (o * inv_l).astype(BF16)


def _attn_c(h_c):
    batch = h_c.shape[0]
    return pl.pallas_call(
        _attn_c_kernel,
        out_shape=jax.ShapeDtypeStruct((batch, C_HEADS, SEQ, HEAD_DIM), BF16),
        grid=(batch, C_HEADS),
        in_specs=[_head_spec(0), _head_spec(C_HEADS), _head_spec(2 * C_HEADS)],
        out_specs=_head_spec(0),
        compiler_params=pltpu.CompilerParams(
            dimension_semantics=("parallel", "parallel"),
            vmem_limit_bytes=VMEM_LIMIT),
        name="moba_attn",
    )(h_c, h_c, h_c)


OUT_TM = 512
Z_BLOCK_OFF = 3


def _outproj_kernel(*refs, n_y, final_norm):
    y_refs = refs[:n_y]
    z_ref, x_ref, w_ref, g_ref, o_ref, ysc_ref = refs[n_y:]
    c = 0
    for y_ref in y_refs:
        for hh in range(y_ref.shape[0]):
            z = z_ref[c].astype(F32)
            gate = z / (1.0 + jnp.exp(-z))
            ysc_ref[:, c * HEAD_DIM:(c + 1) * HEAD_DIM] = (y_ref[hh].astype(F32) * gate).astype(BF16)
            c += 1
    h = x_ref[...] + jnp.dot(ysc_ref[...], w_ref[...], preferred_element_type=F32)
    if final_norm:
        ms = jnp.mean(h * h, axis=-1, keepdims=True)
        h = h * lax.rsqrt(ms + NORM_EPS) * g_ref[...]
    o_ref[...] = h


def _outproj(ys, h_in, x2d, w_bf16, g_final, batch, final_norm):
    rows = x2d.shape[0]
    s_tiles = SEQ // OUT_TM
    n_z = D_MODEL // HEAD_DIM
    y_specs = [pl.BlockSpec((None, y.shape[1], OUT_TM, HEAD_DIM),
                            lambda i: (i // s_tiles, 0, i % s_tiles, 0)) for y in ys]
    assert sum(y.shape[1] for y in ys) == n_z
    return pl.pallas_call(
        functools.partial(_outproj_kernel, n_y=len(ys), final_norm=final_norm),
        out_shape=jax.ShapeDtypeStruct((rows, D_MODEL), F32),
        grid=(rows // OUT_TM,),
        in_specs=y_specs + [
            pl.BlockSpec((None, n_z, OUT_TM, HEAD_DIM),
                         lambda i: (i // s_tiles, Z_BLOCK_OFF, i % s_tiles, 0)),
            pl.BlockSpec((OUT_TM, D_MODEL), lambda i: (i, 0)),
            pl.BlockSpec((D_MODEL, D_MODEL), lambda i: (0, 0), pipeline_mode=pl.Buffered(1)),
            pl.BlockSpec((1, D_MODEL), lambda i: (0, 0)),
        ],
        out_specs=pl.BlockSpec((OUT_TM, D_MODEL), lambda i: (i, 0)),
        scratch_shapes=[pltpu.VMEM((OUT_TM, D_MODEL), BF16)],
        compiler_params=pltpu.CompilerParams(
            dimension_semantics=("parallel",),
            vmem_limit_bytes=VMEM_LIMIT),
        name="gate_outproj",
    )(*ys, h_in, x2d, w_bf16, g_final)


def _rope_tables():
    inv = 1.0 / (ROPE_THETA ** (jnp.arange(0, HEAD_DIM, 2, dtype=F32) / HEAD_DIM))
    ang = jnp.arange(SEQ, dtype=F32)[:, None] * inv[None, :]
    cos = jnp.concatenate([jnp.cos(ang), jnp.cos(ang)], axis=-1)
    sin = jnp.concatenate([-jnp.sin(ang), jnp.sin(ang)], axis=-1)
    return cos, sin


def _tile_kinds(group_kinds, cols_per_group):
    kinds = []
    for kind, cols in zip(group_kinds, cols_per_group):
        assert cols % IN_TN == 0
        kinds += [kind] * (cols // IN_TN)
    assert len(kinds) == IN_WIDTH // IN_TN
    return jnp.asarray(kinds, jnp.int32)


def kernel(x, norm_ab, w_in_ab, w_out_ab, lam_ab, subln_ab, norm_c, w_in_c, w_out_c, final_norm):
    batch, seq, d = x.shape
    assert (seq, d) == (SEQ, D_MODEL) and x.dtype == F32
    assert w_in_ab.shape == (1, D_MODEL, IN_WIDTH) and w_in_c.shape == (1, D_MODEL, IN_WIDTH)
    cos, sin = _rope_tables()
    aw, bw = A_HEADS * HEAD_DIM, B_HEADS * B_VDIM
    kinds_ab = _tile_kinds(
        [KIND_ROPE_SCALED, KIND_ROPE, KIND_PLAIN, KIND_ROPE_SCALED, KIND_ROPE, KIND_PLAIN, KIND_PLAIN],
        [aw, aw, aw, bw, bw, bw, aw + bw])
    cw = C_HEADS * HEAD_DIM
    kinds_c = _tile_kinds([KIND_ROPE_SCALED, KIND_ROPE, KIND_PLAIN, KIND_PLAIN], [cw, cw, cw, cw])

    x2d = x.reshape(batch * seq, d)

    g_final = final_norm[None, :]
    h_ab = _inproj(x2d, norm_ab[0][None, :], w_in_ab[0], cos, sin, kinds_ab, batch)
    ya = _attn_a(h_ab, _dilated_bias_table())
    lam_init = 0.8 - 0.6 * math.exp(-0.3 * 0)
    yb = _attn_b(h_ab, lam_ab[0], subln_ab[0][None, :], lam_init)
    x1 = _outproj([ya, yb], h_ab, x2d, w_out_ab[0].astype(BF16), g_final, batch, final_norm=False)

    h_c = _inproj(x1, norm_c[0][None, :], w_in_c[0], cos, sin, kinds_c, batch)
    yc = _attn_c(h_c)
    out = _outproj([yc], h_c, x1, w_out_c[0].astype(BF16), g_final, batch, final_norm=True)
    return out.reshape(batch, seq, d)
```

```python
import functools
import math

import numpy as np
import jax
import jax.numpy as jnp
from jax import lax
from jax.experimental import pallas as pl
from jax.experimental.pallas import tpu as pltpu

D_MODEL = 2048
SEQ = 2048
HEAD_DIM = 128
ROPE_THETA = 10000.0
NORM_EPS = 1e-6
A_HEADS = 8
DIL_CONFIGS = ((128, 1), (512, 4), (2048, 16))
B_HEADS = 4
B_VDIM = 2 * HEAD_DIM
C_HEADS = 16
MOBA_BLOCK = 256
MOBA_TOPK = 3
N_MOBA_BLOCKS = SEQ // MOBA_BLOCK
IN_WIDTH = 8192
N_COL_BLOCKS = IN_WIDTH // HEAD_DIM
SCALE = HEAD_DIM ** -0.5
Q_SCALE = SCALE * math.log2(math.e)
NEG = -1e30

BF16 = jnp.bfloat16
F32 = jnp.float32

KIND_PLAIN, KIND_ROPE, KIND_ROPE_SCALED = 0, 1, 2

VMEM_LIMIT = 56 * 1024 * 1024


def _nt_dot(a, b):
    return lax.dot_general(a, b, (((1,), (1,)), ((), ())), preferred_element_type=F32)


IN_TM, IN_TN = 1024, 512
IN_SUB_M, IN_SUB_N = 256, 256


def _inproj_kernel(kinds_ref, x_ref, g_ref, w_ref, cos_ref, sin_ref, o_ref, xn_ref):
    j = pl.program_id(1)

    @pl.when(j == 0)
    def _():
        x = x_ref[...]
        ms = jnp.mean(x * x, axis=-1, keepdims=True)
        xn_ref[...] = (x * lax.rsqrt(ms + NORM_EPS) * g_ref[...]).astype(BF16)

    kind = kinds_ref[j]
    scale = jnp.where(kind == KIND_ROPE_SCALED, Q_SCALE, 1.0).astype(F32)
    for nh in range(IN_TN // IN_SUB_N):
        w = w_ref[:, nh * IN_SUB_N:(nh + 1) * IN_SUB_N].astype(BF16)
        for mh in range(IN_TM // IN_SUB_M):
            rows = slice(mh * IN_SUB_M, (mh + 1) * IN_SUB_M)
            acc = jnp.dot(xn_ref[rows, :], w, preferred_element_type=F32)
            cos = jnp.where(kind == KIND_PLAIN, 1.0, cos_ref[rows, :] * scale)
            sin = jnp.where(kind == KIND_PLAIN, 0.0, sin_ref[rows, :] * scale)
            for c in range(IN_SUB_N // HEAD_DIM):
                t = acc[:, c * HEAD_DIM:(c + 1) * HEAD_DIM]
                o_ref[nh * (IN_SUB_N // HEAD_DIM) + c, rows, :] = (
                    t * cos + pltpu.roll(t, HEAD_DIM // 2, 1) * sin).astype(BF16)


def _inproj(x2d, g, w, cos, sin_signed, kinds, batch):
    rows = x2d.shape[0]
    s_tiles = SEQ // IN_TM
    grid = (rows // IN_TM, IN_WIDTH // IN_TN)
    blk_per_tile = IN_TN // HEAD_DIM
    return pl.pallas_call(
        _inproj_kernel,
        out_shape=jax.ShapeDtypeStruct((batch, N_COL_BLOCKS, SEQ, HEAD_DIM), BF16),
        grid_spec=pltpu.PrefetchScalarGridSpec(
            num_scalar_prefetch=1,
            grid=grid,
            in_specs=[
                pl.BlockSpec((IN_TM, D_MODEL), lambda i, j, k: (i, 0)),
                pl.BlockSpec((1, D_MODEL), lambda i, j, k: (0, 0)),
                pl.BlockSpec((D_MODEL, IN_TN), lambda i, j, k: (0, j)),
                pl.BlockSpec((IN_TM, HEAD_DIM), lambda i, j, k: (i % s_tiles, 0)),
                pl.BlockSpec((IN_TM, HEAD_DIM), lambda i, j, k: (i % s_tiles, 0)),
            ],
            out_specs=pl.BlockSpec((None, blk_per_tile, IN_TM, HEAD_DIM),
                                   lambda i, j, k: (i // s_tiles, j, i % s_tiles, 0)),
            scratch_shapes=[pltpu.VMEM((IN_TM, D_MODEL), BF16)],
        ),
        compiler_params=pltpu.CompilerParams(
            dimension_semantics=("parallel", "arbitrary"),
            vmem_limit_bytes=VMEM_LIMIT),
        name="norm_inproj",
    )(kinds, x2d, g, w, cos, sin_signed)


TQ = 256
N_QT = SEQ // TQ


def _softmax_pv(s, v):
    m = jnp.max(s, axis=-1, keepdims=True)
    p = jnp.exp2(s - m)
    l = jnp.sum(p, axis=-1, keepdims=True)
    o = jnp.dot(p.astype(BF16), v, preferred_element_type=F32)
    return o, 1.0 / l


def _pipelined(units, scores, consume):
    s_next = scores(units[0])
    for idx, u in enumerate(units):
        s = s_next
        if idx + 1 < len(units):
            s_next = scores(units[idx + 1])
        consume(u, s)


def _causal_mask():
    r = lax.broadcasted_iota(jnp.int32, (TQ, TQ), 0)
    c = lax.broadcasted_iota(jnp.int32, (TQ, TQ), 1)
    return c <= r


def _dilated_bias_table():
    r = np.arange(TQ)[:, None]
    c = np.arange(SEQ)[None, :]
    d = (SEQ - TQ) + r - c
    mult = np.zeros(d.shape, np.float64)
    for window, dil in DIL_CONFIGS:
        mult += (d >= 0) & (d % dil == 0) & (d <= window)
    return jnp.asarray(np.where(mult > 0, np.log2(np.maximum(mult, 1.0)), NEG), F32)


def _attn_a_kernel(q_ref, k_ref, v_ref, bias_ref, o_ref):
    def scores(i):
        return _nt_dot(q_ref[i * TQ:(i + 1) * TQ, :], k_ref[0:(i + 1) * TQ, :])

    def consume(i, s):
        ext = (i + 1) * TQ
        o, inv_l = _softmax_pv(s + bias_ref[:, SEQ - ext:SEQ], v_ref[0:ext, :])
        o_ref[i * TQ:ext, :] = (o * inv_l).astype(BF16)

    _pipelined(list(range(N_QT)), scores, consume)


def _head_spec(off):
    return pl.BlockSpec((None, None, SEQ, HEAD_DIM), lambda b, h: (b, off + h, 0, 0))


def _attn_a(h_ab, bias):
    batch = h_ab.shape[0]
    return pl.pallas_call(
        _attn_a_kernel,
        out_shape=jax.ShapeDtypeStruct((batch, A_HEADS, SEQ, HEAD_DIM), BF16),
        grid=(batch, A_HEADS),
        in_specs=[_head_spec(0), _head_spec(A_HEADS), _head_spec(2 * A_HEADS),
                  pl.BlockSpec((TQ, SEQ), lambda b, h: (0, 0))],
        out_specs=_head_spec(0),
        compiler_params=pltpu.CompilerParams(
            dimension_semantics=("parallel", "parallel"),
            vmem_limit_bytes=VMEM_LIMIT),
        name="dilated_attn",
    )(h_ab, h_ab, h_ab, bias)


B_Q_OFF, B_K_OFF, B_V_OFF = 12, 16, 20


def _attn_b_kernel(lam_ref, g_ref, q_ref, k_ref, v_ref, o_ref, vcat_ref, *, lam_init):
    vcat_ref[:, :HEAD_DIM] = v_ref[0]
    vcat_ref[:, HEAD_DIM:] = v_ref[1]
    lam_v = lam_ref[...]
    lam = (jnp.exp(jnp.sum(lam_v[0:1] * lam_v[1:2], axis=-1, keepdims=True))
           - jnp.exp(jnp.sum(lam_v[2:3] * lam_v[3:4], axis=-1, keepdims=True)) + lam_init)
    causal = _causal_mask()
    first = {}

    def scores(u):
        i, t = u
        return _nt_dot(q_ref[t, i * TQ:(i + 1) * TQ, :], k_ref[t, 0:(i + 1) * TQ, :])

    def consume(u, s):
        i, t = u
        ext = (i + 1) * TQ
        rows = slice(i * TQ, ext)
        own = jnp.where(causal, s[:, i * TQ:], NEG)
        s = own if i == 0 else jnp.concatenate([s[:, :i * TQ], own], axis=1)
        o, inv_l = _softmax_pv(s, vcat_ref[0:ext, :])
        if t == 0:
            first[i] = o * inv_l
            return
        o = first.pop(i) - lam * (o * inv_l)
        ms = jnp.mean(o * o, axis=-1, keepdims=True)
        y = o * lax.rsqrt(ms + NORM_EPS) * g_ref[...] * (1.0 - lam_init)
        o_ref[0, rows, :] = y[:, :HEAD_DIM].astype(BF16)
        o_ref[1, rows, :] = y[:, HEAD_DIM:].astype(BF16)

    _pipelined([(i, t) for i in range(N_QT) for t in range(2)], scores, consume)


def _attn_b(h_ab, lam_vec, subln_g, lam_init):
    batch = h_ab.shape[0]
    pair_spec = lambda off: pl.BlockSpec((None, 2, SEQ, HEAD_DIM), lambda b, h: (b, off + h, 0, 0))
    return pl.pallas_call(
        functools.partial(_attn_b_kernel, lam_init=lam_init),
        out_shape=jax.ShapeDtypeStruct((batch, 2 * B_HEADS, SEQ, HEAD_DIM), BF16),
        grid=(batch, B_HEADS),
        in_specs=[
            pl.BlockSpec((4, HEAD_DIM), lambda b, h: (0, 0)),
            pl.BlockSpec((1, B_VDIM), lambda b, h: (0, 0)),
            pair_spec(B_Q_OFF), pair_spec(B_K_OFF), pair_spec(B_V_OFF),
        ],
        out_specs=pair_spec(0),
        scratch_shapes=[pltpu.VMEM((SEQ, B_VDIM), BF16)],
        compiler_params=pltpu.CompilerParams(
            dimension_semantics=("parallel", "parallel"),
            vmem_limit_bytes=VMEM_LIMIT),
        name="diff_attn",
    )(lam_vec, subln_g, h_ab, h_ab, h_ab)


KM_ROWS = 16


def _attn_c_kernel(q_ref, k_ref, v_ref, o_ref):
    assert TQ == MOBA_BLOCK
    key_blk = lax.broadcasted_iota(jnp.int32, (KM_ROWS, SEQ), 1) // MOBA_BLOCK
    pool = (lax.broadcasted_iota(jnp.int32, (KM_ROWS, SEQ), 0) == key_blk).astype(BF16)
    km = jnp.dot(pool, k_ref[...], preferred_element_type=F32) * (1.0 / MOBA_BLOCK)
    km_hi = km.astype(BF16)
    km_lo = (km - km_hi.astype(F32)).astype(BF16)
    km_hl = jnp.concatenate([km_hi, km_lo], axis=0)
    row = lax.broadcasted_iota(jnp.int32, (KM_ROWS, TQ), 0)
    pad_rows = jnp.zeros((HEAD_DIM - KM_ROWS, TQ), F32)
    causal = _causal_mask()

    def scores(i):
        q = q_ref[i * TQ:(i + 1) * TQ, :]
        s = _nt_dot(q, k_ref[0:(i + 1) * TQ, :])
        g = _nt_dot(km_hl, q) if i > 0 else None
        return s, g

    def consume(i, sg):
        s, g = sg
        ext = (i + 1) * TQ
        parts = []
        if i > 0:
            past = row < i
            gate = jnp.where(past, g[:KM_ROWS] + g[KM_ROWS:], -jnp.inf)
            rank = jnp.zeros(gate.shape, F32)
            for j in range(i):
                gj = gate[j:j + 1, :]
                beats = (gj > gate) | ((gj == gate) & (j < row))
                rank = rank + beats.astype(F32)
            sel_t = (past & (rank < MOBA_TOPK)).astype(F32)
            sel = jnp.concatenate([sel_t, pad_rows], axis=0).T
            for n in range(i):
                parts.append(jnp.where(sel[:, n:n + 1] > 0.5, s[:, n * TQ:(n + 1) * TQ], NEG))
        parts.append(jnp.where(causal, s[:, i * TQ:], NEG))
        s = parts[0] if i == 0 else jnp.concatenate(parts, axis=1)
        o, inv_l = _softmax_pv(s, v_ref[0:ext, :])
        o_ref[i * TQ:ext, :] = (o * inv_l).astype(BF16)

    _pipelined(list(range(N_QT)), scores, consume)


def _attn_c(h_c):
    batch = h_c.shape[0]
    return pl.pallas_call(
        _attn_c_kernel,
        out_shape=jax.ShapeDtypeStruct((batch, C_HEADS, SEQ, HEAD_DIM), BF16),
        grid=(batch, C_HEADS),
        in_specs=[_head_spec(0), _head_spec(C_HEADS), _head_spec(2 * C_HEADS)],
        out_specs=_head_spec(0),
        compiler_params=pltpu.CompilerParams(
            dimension_semantics=("parallel", "parallel"),
            vmem_limit_bytes=VMEM_LIMIT),
        name="moba_attn",
    )(h_c, h_c, h_c)


OUT_TM = 512
Z_BLOCK_OFF = 3


def _outproj_kernel(*refs, n_y, final_norm):
    y_refs = refs[:n_y]
    z_ref, x_ref, w_ref, g_ref, o_ref, ysc_ref = refs[n_y:]
    c = 0
    for y_ref in y_refs:
        for hh in range(y_ref.shape[0]):
            z = z_ref[c].astype(F32)
            gate = z / (1.0 + jnp.exp(-z))
            ysc_ref[:, c * HEAD_DIM:(c + 1) * HEAD_DIM] = (y_ref[hh].astype(F32) * gate).astype(BF16)
            c += 1
    h = x_ref[...] + jnp.dot(ysc_ref[...], w_ref[...], preferred_element_type=F32)
    if final_norm:
        ms = jnp.mean(h * h, axis=-1, keepdims=True)
        h = h * lax.rsqrt(ms + NORM_EPS) * g_ref[...]
    o_ref[...] = h


def _outproj(ys, h_in, x2d, w_bf16, g_final, batch, final_norm):
    rows = x2d.shape[0]
    s_tiles = SEQ // OUT_TM
    n_z = D_MODEL // HEAD_DIM
    y_specs = [pl.BlockSpec((None, y.shape[1], OUT_TM, HEAD_DIM),
                            lambda i: (i // s_tiles, 0, i % s_tiles, 0)) for y in ys]
    assert sum(y.shape[1] for y in ys) == n_z
    return pl.pallas_call(
        functools.partial(_outproj_kernel, n_y=len(ys), final_norm=final_norm),
        out_shape=jax.ShapeDtypeStruct((rows, D_MODEL), F32),
        grid=(rows // OUT_TM,),
        in_specs=y_specs + [
            pl.BlockSpec((None, n_z, OUT_TM, HEAD_DIM),
                         lambda i: (i // s_tiles, Z_BLOCK_OFF, i % s_tiles, 0)),
            pl.BlockSpec((OUT_TM, D_MODEL), lambda i: (i, 0)),
            pl.BlockSpec((D_MODEL, D_MODEL), lambda i: (0, 0), pipeline_mode=pl.Buffered(1)),
            pl.BlockSpec((1, D_MODEL), lambda i: (0, 0)),
        ],
        out_specs=pl.BlockSpec((OUT_TM, D_MODEL), lambda i: (i, 0)),
        scratch_shapes=[pltpu.VMEM((OUT_TM, D_MODEL), BF16)],
        compiler_params=pltpu.CompilerParams(
            dimension_semantics=("parallel",),
            vmem_limit_bytes=VMEM_LIMIT),
        name="gate_outproj",
    )(*ys, h_in, x2d, w_bf16, g_final)


def _rope_tables():
    inv = 1.0 / (ROPE_THETA ** (jnp.arange(0, HEAD_DIM, 2, dtype=F32) / HEAD_DIM))
    ang = jnp.arange(SEQ, dtype=F32)[:, None] * inv[None, :]
    cos = jnp.concatenate([jnp.cos(ang), jnp.cos(ang)], axis=-1)
    sin = jnp.concatenate([-jnp.sin(ang), jnp.sin(ang)], axis=-1)
    return cos, sin


def _tile_kinds(group_kinds, cols_per_group):
    kinds = []
    for kind, cols in zip(group_kinds, cols_per_group):
        assert cols % IN_TN == 0
        kinds += [kind] * (cols // IN_TN)
    assert len(kinds) == IN_WIDTH // IN_TN
    return jnp.asarray(kinds, jnp.int32)


def kernel(x, norm_ab, w_in_ab, w_out_ab, lam_ab, subln_ab, norm_c, w_in_c, w_out_c, final_norm):
    batch, seq, d = x.shape
    assert (seq, d) == (SEQ, D_MODEL) and x.dtype == F32
    assert w_in_ab.shape == (1, D_MODEL, IN_WIDTH) and w_in_c.shape == (1, D_MODEL, IN_WIDTH)
    cos, sin = _rope_tables()
    aw, bw = A_HEADS * HEAD_DIM, B_HEADS * B_VDIM
    kinds_ab = _tile_kinds(
        [KIND_ROPE_SCALED, KIND_ROPE, KIND_PLAIN, KIND_ROPE_SCALED, KIND_ROPE, KIND_PLAIN, KIND_PLAIN],
        [aw, aw, aw, bw, bw, bw, aw + bw])
    cw = C_HEADS * HEAD_DIM
    kinds_c = _tile_kinds([KIND_ROPE_SCALED, KIND_ROPE, KIND_PLAIN, KIND_PLAIN], [cw, cw, cw, cw])

    x2d = x.reshape(batch * seq, d)

    g_final = final_norm[None, :]
    h_ab = _inproj(x2d, norm_ab[0][None, :], w_in_ab[0], cos, sin, kinds_ab, batch)
    ya = _attn_a(h_ab, _dilated_bias_table())
    lam_init = 0.8 - 0.6 * math.exp(-0.3 * 0)
    yb = _attn_b(h_ab, lam_ab[0], subln_ab[0][None, :], lam_init)
    x1 = _outproj([ya, yb], h_ab, x2d, w_out_ab[0].astype(BF16), g_final, batch, final_norm=False)

    h_c = _inproj(x1, norm_c[0][None, :], w_in_c[0], cos, sin, kinds_c, batch)
    yc = _attn_c(h_c)
    out = _outproj([yc], h_c, x1, w_out_c[0].astype(BF16), g_final, batch, final_norm=True)
    return out.reshape(batch, seq, d)
```

```python
import functools
import math

import numpy as np
import jax
import jax.numpy as jnp
from jax import lax
from jax.experimental import pallas as pl
from jax.experimental.pallas import tpu as pltpu

D_MODEL = 2048
SEQ = 2048
HEAD_DIM = 128
ROPE_THETA = 10000.0
NORM_EPS = 1e-6
A_HEADS = 8
DIL_CONFIGS = ((128, 1), (512, 4), (2048, 16))
B_HEADS = 4
B_VDIM = 2 * HEAD_DIM
C_HEADS = 16
MOBA_BLOCK = 256
MOBA_TOPK = 3
N_MOBA_BLOCKS = SEQ // MOBA_BLOCK
IN_WIDTH = 8192
N_COL_BLOCKS = IN_WIDTH // HEAD_DIM
SCALE = HEAD_DIM ** -0.5
Q_SCALE = SCALE * math.log2(math.e)
NEG = -1e30

BF16 = jnp.bfloat16
F32 = jnp.float32

KIND_PLAIN, KIND_ROPE, KIND_ROPE_SCALED = 0, 1, 2

VMEM_LIMIT = 56 * 1024 * 1024


def _nt_dot(a, b):
    return lax.dot_general(a, b, (((1,), (1,)), ((), ())), preferred_element_type=F32)


IN_TM, IN_TN = 2048, 1024
IN_SUB_M, IN_SUB_N = 256, 256


def _inproj_kernel(kinds_ref, x_ref, g_ref, w_ref, cos_ref, sin_ref, o_ref, xn_ref):
    j = pl.program_id(1)

    @pl.when(j == 0)
    def _():
        x = x_ref[...]
        ms = jnp.mean(x * x, axis=-1, keepdims=True)
        xn_ref[...] = (x * lax.rsqrt(ms + NORM_EPS) * g_ref[...]).astype(BF16)

    kind = kinds_ref[j]
    scale = jnp.where(kind == KIND_ROPE_SCALED, Q_SCALE, 1.0).astype(F32)
    for nh in range(IN_TN // IN_SUB_N):
        w = w_ref[:, nh * IN_SUB_N:(nh + 1) * IN_SUB_N].astype(BF16)
        for mh in range(IN_TM // IN_SUB_M):
            rows = slice(mh * IN_SUB_M, (mh + 1) * IN_SUB_M)
            acc = jnp.dot(xn_ref[rows, :], w, preferred_element_type=F32)
            cos = jnp.where(kind == KIND_PLAIN, 1.0, cos_ref[rows, :] * scale)
            sin = jnp.where(kind == KIND_PLAIN, 0.0, sin_ref[rows, :] * scale)
            for c in range(IN_SUB_N // HEAD_DIM):
                t = acc[:, c * HEAD_DIM:(c + 1) * HEAD_DIM]
                o_ref[nh * (IN_SUB_N // HEAD_DIM) + c, rows, :] = (
                    t * cos + pltpu.roll(t, HEAD_DIM // 2, 1) * sin).astype(BF16)


def _inproj(x2d, g, w, cos, sin_signed, kinds, batch):
    rows = x2d.shape[0]
    s_tiles = SEQ // IN_TM
    grid = (rows // IN_TM, IN_WIDTH // IN_TN)
    blk_per_tile = IN_TN // HEAD_DIM
    return pl.pallas_call(
        _inproj_kernel,
        out_shape=jax.ShapeDtypeStruct((batch, N_COL_BLOCKS, SEQ, HEAD_DIM), BF16),
        grid_spec=pltpu.PrefetchScalarGridSpec(
            num_scalar_prefetch=1,
            grid=grid,
            in_specs=[
                pl.BlockSpec((IN_TM, D_MODEL), lambda i, j, k: (i, 0), pipeline_mode=pl.Buffered(1)),
                pl.BlockSpec((1, D_MODEL), lambda i, j, k: (0, 0)),
                pl.BlockSpec((D_MODEL, IN_TN), lambda i, j, k: (0, j)),
                pl.BlockSpec((IN_TM, HEAD_DIM), lambda i, j, k: (i % s_tiles, 0)),
                pl.BlockSpec((IN_TM, HEAD_DIM), lambda i, j, k: (i % s_tiles, 0)),
            ],
            out_specs=pl.BlockSpec((None, blk_per_tile, IN_TM, HEAD_DIM),
                                   lambda i, j, k: (i // s_tiles, j, i % s_tiles, 0)),
            scratch_shapes=[pltpu.VMEM((IN_TM, D_MODEL), BF16)],
        ),
        compiler_params=pltpu.CompilerParams(
            dimension_semantics=("parallel", "arbitrary"),
            vmem_limit_bytes=VMEM_LIMIT),
        name="norm_inproj",
    )(kinds, x2d, g, w, cos, sin_signed)


TQ = 256
N_QT = SEQ // TQ


def _softmax_pv(s, v):
    m = jnp.max(s, axis=-1, keepdims=True)
    p = jnp.exp2(s - m)
    l = jnp.sum(p, axis=-1, keepdims=True)
    o = jnp.dot(p.astype(BF16), v, preferred_element_type=F32)
    return o, 1.0 / l


def _pipelined(units, scores, consume):
    s_next = scores(units[0])
    for idx, u in enumerate(units):
        s = s_next
        if idx + 1 < len(units):
            s_next = scores(units[idx + 1])
        consume(u, s)


def _causal_mask():
    r = lax.broadcasted_iota(jnp.int32, (TQ, TQ), 0)
    c = lax.broadcasted_iota(jnp.int32, (TQ, TQ), 1)
    return c <= r


def _dilated_bias_table():
    r = np.arange(TQ)[:, None]
    c = np.arange(SEQ)[None, :]
    d = (SEQ - TQ) + r - c
    mult = np.zeros(d.shape, np.float64)
    for window, dil in DIL_CONFIGS:
        mult += (d >= 0) & (d % dil == 0) & (d <= window)
    return jnp.asarray(np.where(mult > 0, np.log2(np.maximum(mult, 1.0)), NEG), F32)


def _attn_a_kernel(q_ref, k_ref, v_ref, bias_ref, o_ref):
    def scores(i):
        return _nt_dot(q_ref[i * TQ:(i + 1) * TQ, :], k_ref[0:(i + 1) * TQ, :])

    def consume(i, s):
        ext = (i + 1) * TQ
        o, inv_l = _softmax_pv(s + bias_ref[:, SEQ - ext:SEQ], v_ref[0:ext, :])
        o_ref[i * TQ:ext, :] = (o * inv_l).astype(BF16)

    _pipelined(list(range(N_QT)), scores, consume)


def _head_spec(off):
    return pl.BlockSpec((None, None, SEQ, HEAD_DIM), lambda b, h: (b, off + h, 0, 0))


def _attn_a(h_ab, bias):
    batch = h_ab.shape[0]
    return pl.pallas_call(
        _attn_a_kernel,
        out_shape=jax.ShapeDtypeStruct((batch, A_HEADS, SEQ, HEAD_DIM), BF16),
        grid=(batch, A_HEADS),
        in_specs=[_head_spec(0), _head_spec(A_HEADS), _head_spec(2 * A_HEADS),
                  pl.BlockSpec((TQ, SEQ), lambda b, h: (0, 0))],
        out_specs=_head_spec(0),
        compiler_params=pltpu.CompilerParams(
            dimension_semantics=("parallel", "parallel"),
            vmem_limit_bytes=VMEM_LIMIT),
        name="dilated_attn",
    )(h_ab, h_ab, h_ab, bias)


B_Q_OFF, B_K_OFF, B_V_OFF = 12, 16, 20


def _attn_b_kernel(lam_ref, g_ref, q_ref, k_ref, v_ref, o_ref, vcat_ref, *, lam_init):
    vcat_ref[:, :HEAD_DIM] = v_ref[0]
    vcat_ref[:, HEAD_DIM:] = v_ref[1]
    lam_v = lam_ref[...]
    lam = (jnp.exp(jnp.sum(lam_v[0:1] * lam_v[1:2], axis=-1, keepdims=True))
           - jnp.exp(jnp.sum(lam_v[2:3] * lam_v[3:4], axis=-1, keepdims=True)) + lam_init)
    causal = _causal_mask()
    first = {}

    def scores(u):
        i, t = u
        return _nt_dot(q_ref[t, i * TQ:(i + 1) * TQ, :], k_ref[t, 0:(i + 1) * TQ, :])

    def consume(u, s):
        i, t = u
        ext = (i + 1) * TQ
        rows = slice(i * TQ, ext)
        own = jnp.where(causal, s[:, i * TQ:], NEG)
        s = own if i == 0 else jnp.concatenate([s[:, :i * TQ], own], axis=1)
        o, inv_l = _softmax_pv(s, vcat_ref[0:ext, :])
        if t == 0:
            first[i] = o * inv_l
            return
        o = first.pop(i) - lam * (o * inv_l)
        ms = jnp.mean(o * o, axis=-1, keepdims=True)
        y = o * lax.rsqrt(ms + NORM_EPS) * g_ref[...] * (1.0 - lam_init)
        o_ref[0, rows, :] = y[:, :HEAD_DIM].astype(BF16)
        o_ref[1, rows, :] = y[:, HEAD_DIM:].astype(BF16)

    _pipelined([(i, t) for i in range(N_QT) for t in range(2)], scores, consume)


def _attn_b(h_ab, lam_vec, subln_g, lam_init):
    batch = h_ab.shape[0]
    pair_spec = lambda off: pl.BlockSpec((None, 2, SEQ, HEAD_DIM), lambda b, h: (b, off + h, 0, 0))
    return pl.pallas_call(
        functools.partial(_attn_b_kernel, lam_init=lam_init),
        out_shape=jax.ShapeDtypeStruct((batch, 2 * B_HEADS, SEQ, HEAD_DIM), BF16),
        grid=(batch, B_HEADS),
        in_specs=[
            pl.BlockSpec((4, HEAD_DIM), lambda b, h: (0, 0)),
            pl.BlockSpec((1, B_VDIM), lambda b, h: (0, 0)),
            pair_spec(B_Q_OFF), pair_spec(B_K_OFF), pair_spec(B_V_OFF),
        ],
        out_specs=pair_spec(0),
        scratch_shapes=[pltpu.VMEM((SEQ, B_VDIM), BF16)],
        compiler_params=pltpu.CompilerParams(
            dimension_semantics=("parallel", "parallel"),
            vmem_limit_bytes=VMEM_LIMIT),
        name="diff_attn",
    )(lam_vec, subln_g, h_ab, h_ab, h_ab)


KM_ROWS = 16


def _attn_c_kernel(q_ref, k_ref, v_ref, o_ref):
    assert TQ == MOBA_BLOCK
    key_blk = lax.broadcasted_iota(jnp.int32, (KM_ROWS, SEQ), 1) // MOBA_BLOCK
    pool = (lax.broadcasted_iota(jnp.int32, (KM_ROWS, SEQ), 0) == key_blk).astype(BF16)
    km = jnp.dot(pool, k_ref[...], preferred_element_type=F32) * (1.0 / MOBA_BLOCK)
    km_hi = km.astype(BF16)
    km_lo = (km - km_hi.astype(F32)).astype(BF16)
    km_hl = jnp.concatenate([km_hi, km_lo], axis=0)
    row = lax.broadcasted_iota(jnp.int32, (KM_ROWS, TQ), 0)
    pad_rows = jnp.zeros((HEAD_DIM - KM_ROWS, TQ), F32)
    causal = _causal_mask()

    def scores(i):
        q = q_ref[i * TQ:(i + 1) * TQ, :]
        s = _nt_dot(q, k_ref[0:(i + 1) * TQ, :])
        g = _nt_dot(km_hl, q) if i > 0 else None
        return s, g

    def consume(i, sg):
        s, g = sg
        ext = (i + 1) * TQ
        parts = []
        if i > 0:
            past = row < i
            gate = jnp.where(past, g[:KM_ROWS] + g[KM_ROWS:], -jnp.inf)
            rank = jnp.zeros(gate.shape, F32)
            for j in range(i):
                gj = gate[j:j + 1, :]
                beats = (gj > gate) | ((gj == gate) & (j < row))
                rank = rank + beats.astype(F32)
            sel_t = (past & (rank < MOBA_TOPK)).astype(F32)
            sel = jnp.concatenate([sel_t, pad_rows], axis=0).T
            for n in range(i):
                parts.append(jnp.where(sel[:, n:n + 1] > 0.5, s[:, n * TQ:(n + 1) * TQ], NEG))
        parts.append(jnp.where(causal, s[:, i * TQ:], NEG))
        s = parts[0] if i == 0 else jnp.concatenate(parts, axis=1)
        o, inv_l = _softmax_pv(s, v_ref[0:ext, :])
        o_ref[i * TQ:ext, :] = (o * inv_l).astype(BF16)

    _pipelined(list(range(N_QT)), scores, consume)


def _attn_c(h_c):
    batch = h_c.shape[0]
    return pl.pallas_call(
        _attn_c_kernel,
        out_shape=jax.ShapeDtypeStruct((batch, C_HEADS, SEQ, HEAD_DIM), BF16),
        grid=(batch, C_HEADS),
        in_specs=[_head_spec(0), _head_spec(C_HEADS), _head_spec(2 * C_HEADS)],
        out_specs=_head_spec(0),
        compiler_params=pltpu.CompilerParams(
            dimension_semantics=("parallel", "parallel"),
            vmem_limit_bytes=VMEM_LIMIT),
        name="moba_attn",
    )(h_c, h_c, h_c)


OUT_TM = 512
Z_BLOCK_OFF = 3


def _outproj_kernel(*refs, n_y, final_norm):
    y_refs = refs[:n_y]
    z_ref, x_ref, w_ref, g_ref, o_ref, ysc_ref = refs[n_y:]
    c = 0
    for y_ref in y_refs:
        for hh in range(y_ref.shape[0]):
            z = z_ref[c].astype(F32)
            gate = z / (1.0 + jnp.exp(-z))
            ysc_ref[:, c * HEAD_DIM:(c + 1) * HEAD_DIM] = (y_ref[hh].astype(F32) * gate).astype(BF16)
            c += 1
    h = x_ref[...] + jnp.dot(ysc_ref[...], w_ref[...], preferred_element_type=F32)
    if final_norm:
        ms = jnp.mean(h * h, axis=-1, keepdims=True)
        h = h * lax.rsqrt(ms + NORM_EPS) * g_ref[...]
    o_ref[...] = h


def _outproj(ys, h_in, x2d, w_bf16, g_final, batch, final_norm):
    rows = x2d.shape[0]
    s_tiles = SEQ // OUT_TM
    n_z = D_MODEL // HEAD_DIM
    y_specs = [pl.BlockSpec((None, y.shape[1], OUT_TM, HEAD_DIM),
                            lambda i: (i // s_tiles, 0, i % s_tiles, 0)) for y in ys]
    assert sum(y.shape[1] for y in ys) == n_z
    return pl.pallas_call(
        functools.partial(_outproj_kernel, n_y=len(ys), final_norm=final_norm),
        out_shape=jax.ShapeDtypeStruct((rows, D_MODEL), F32),
        grid=(rows // OUT_TM,),
        in_specs=y_specs + [
            pl.BlockSpec((None, n_z, OUT_TM, HEAD_DIM),
                         lambda i: (i // s_tiles, Z_BLOCK_OFF, i % s_tiles, 0)),
            pl.BlockSpec((OUT_TM, D_MODEL), lambda i: (i, 0)),
            pl.BlockSpec((D_MODEL, D_MODEL), lambda i: (0, 0), pipeline_mode=pl.Buffered(1)),
            pl.BlockSpec((1, D_MODEL), lambda i: (0, 0)),
        ],
        out_specs=pl.BlockSpec((OUT_TM, D_MODEL), lambda i: (i, 0)),
        scratch_shapes=[pltpu.VMEM((OUT_TM, D_MODEL), BF16)],
        compiler_params=pltpu.CompilerParams(
            dimension_semantics=("parallel",),
            vmem_limit_bytes=VMEM_LIMIT),
        name="gate_outproj",
    )(*ys, h_in, x2d, w_bf16, g_final)


def _rope_tables():
    inv = 1.0 / (ROPE_THETA ** (jnp.arange(0, HEAD_DIM, 2, dtype=F32) / HEAD_DIM))
    ang = jnp.arange(SEQ, dtype=F32)[:, None] * inv[None, :]
    cos = jnp.concatenate([jnp.cos(ang), jnp.cos(ang)], axis=-1)
    sin = jnp.concatenate([-jnp.sin(ang), jnp.sin(ang)], axis=-1)
    return cos, sin


def _tile_kinds(group_kinds, cols_per_group):
    kinds = []
    for kind, cols in zip(group_kinds, cols_per_group):
        assert cols % IN_TN == 0
        kinds += [kind] * (cols // IN_TN)
    assert len(kinds) == IN_WIDTH // IN_TN
    return jnp.asarray(kinds, jnp.int32)


def kernel(x, norm_ab, w_in_ab, w_out_ab, lam_ab, subln_ab, norm_c, w_in_c, w_out_c, final_norm):
    batch, seq, d = x.shape
    assert (seq, d) == (SEQ, D_MODEL) and x.dtype == F32
    assert w_in_ab.shape == (1, D_MODEL, IN_WIDTH) and w_in_c.shape == (1, D_MODEL, IN_WIDTH)
    cos, sin = _rope_tables()
    aw, bw = A_HEADS * HEAD_DIM, B_HEADS * B_VDIM
    kinds_ab = _tile_kinds(
        [KIND_ROPE_SCALED, KIND_ROPE, KIND_PLAIN, KIND_ROPE_SCALED, KIND_ROPE, KIND_PLAIN, KIND_PLAIN],
        [aw, aw, aw, bw, bw, bw, aw + bw])
    cw = C_HEADS * HEAD_DIM
    kinds_c = _tile_kinds([KIND_ROPE_SCALED, KIND_ROPE, KIND_PLAIN, KIND_PLAIN], [cw, cw, cw, cw])

    x2d = x.reshape(batch * seq, d)

    g_final = final_norm[None, :]
    h_ab = _inproj(x2d, norm_ab[0][None, :], w_in_ab[0], cos, sin, kinds_ab, batch)
    ya = _attn_a(h_ab, _dilated_bias_table())
    lam_init = 0.8 - 0.6 * math.exp(-0.3 * 0)
    yb = _attn_b(h_ab, lam_ab[0], subln_ab[0][None, :], lam_init)
    x1 = _outproj([ya, yb], h_ab, x2d, w_out_ab[0].astype(BF16), g_final, batch, final_norm=False)

    h_c = _inproj(x1, norm_c[0][None, :], w_in_c[0], cos, sin, kinds_c, batch)
    yc = _attn_c(h_c)
    out = _outproj([yc], h_c, x1, w_out_c[0].astype(BF16), g_final, batch, final_norm=True)
    return out.reshape(batch, seq, d)
```

```python
import functools
import math

import numpy as np
import jax
import jax.numpy as jnp
from jax import lax
from jax.experimental import pallas as pl
from jax.experimental.pallas import tpu as pltpu

D_MODEL = 2048
SEQ = 2048
HEAD_DIM = 128
ROPE_THETA = 10000.0
NORM_EPS = 1e-6
A_HEADS = 8
DIL_CONFIGS = ((128, 1), (512, 4), (2048, 16))
B_HEADS = 4
B_VDIM = 2 * HEAD_DIM
C_HEADS = 16
MOBA_BLOCK = 256
MOBA_TOPK = 3
N_MOBA_BLOCKS = SEQ // MOBA_BLOCK
IN_WIDTH = 8192
N_COL_BLOCKS = IN_WIDTH // HEAD_DIM
SCALE = HEAD_DIM ** -0.5
Q_SCALE = SCALE * math.log2(math.e)
NEG = -1e30

BF16 = jnp.bfloat16
F32 = jnp.float32

KIND_PLAIN, KIND_ROPE, KIND_ROPE_SCALED = 0, 1, 2

VMEM_LIMIT = 56 * 1024 * 1024


def _nt_dot(a, b):
    return lax.dot_general(a, b, (((1,), (1,)), ((), ())), preferred_element_type=F32)


IN_TM, IN_TN = 2048, 1024
IN_SUB_M, IN_SUB_N = 256, 256


def _inproj_kernel(kinds_ref, x_ref, g_ref, w_ref, cos_ref, sin_ref, o_ref, xn_ref):
    j = pl.program_id(1)

    @pl.when(j == 0)
    def _():
        x = x_ref[...]
        ms = jnp.mean(x * x, axis=-1, keepdims=True)
        xn_ref[...] = (x * lax.rsqrt(ms + NORM_EPS) * g_ref[...]).astype(BF16)

    kind = kinds_ref[j]
    scale = jnp.where(kind == KIND_ROPE_SCALED, Q_SCALE, 1.0).astype(F32)
    for nh in range(IN_TN // IN_SUB_N):
        w = w_ref[:, nh * IN_SUB_N:(nh + 1) * IN_SUB_N].astype(BF16)
        for mh in range(IN_TM // IN_SUB_M):
            rows = slice(mh * IN_SUB_M, (mh + 1) * IN_SUB_M)
            acc = jnp.dot(xn_ref[rows, :], w, preferred_element_type=F32)
            cos = jnp.where(kind == KIND_PLAIN, 1.0, cos_ref[rows, :] * scale)
            sin = jnp.where(kind == KIND_PLAIN, 0.0, sin_ref[rows, :] * scale)
            for c in range(IN_SUB_N // HEAD_DIM):
                t = acc[:, c * HEAD_DIM:(c + 1) * HEAD_DIM]
                o_ref[nh * (IN_SUB_N // HEAD_DIM) + c, rows, :] = (
                    t * cos + pltpu.roll(t, HEAD_DIM // 2, 1) * sin).astype(BF16)


def _inproj(x2d, g, w, cos, sin_signed, kinds, batch):
    rows = x2d.shape[0]
    s_tiles = SEQ // IN_TM
    grid = (rows // IN_TM, IN_WIDTH // IN_TN)
    blk_per_tile = IN_TN // HEAD_DIM
    return pl.pallas_call(
        _inproj_kernel,
        out_shape=jax.ShapeDtypeStruct((batch, N_COL_BLOCKS, SEQ, HEAD_DIM), BF16),
        grid_spec=pltpu.PrefetchScalarGridSpec(
            num_scalar_prefetch=1,
            grid=grid,
            in_specs=[
                pl.BlockSpec((IN_TM, D_MODEL), lambda i, j, k: (i, 0), pipeline_mode=pl.Buffered(1)),
                pl.BlockSpec((1, D_MODEL), lambda i, j, k: (0, 0)),
                pl.BlockSpec((D_MODEL, IN_TN), lambda i, j, k: (0, j)),
                pl.BlockSpec((IN_TM, HEAD_DIM), lambda i, j, k: (i % s_tiles, 0)),
                pl.BlockSpec((IN_TM, HEAD_DIM), lambda i, j, k: (i % s_tiles, 0)),
            ],
            out_specs=pl.BlockSpec((None, blk_per_tile, IN_TM, HEAD_DIM),
                                   lambda i, j, k: (i // s_tiles, j, i % s_tiles, 0)),
            scratch_shapes=[pltpu.VMEM((IN_TM, D_MODEL), BF16)],
        ),
        compiler_params=pltpu.CompilerParams(
            dimension_semantics=("parallel", "arbitrary"),
            vmem_limit_bytes=VMEM_LIMIT),
        name="norm_inproj",
    )(kinds, x2d, g, w, cos, sin_signed)


TQ = 256
N_QT = SEQ // TQ


def _softmax_pv(s, v):
    m = jnp.max(s, axis=-1, keepdims=True)
    p = jnp.exp2(s - m)
    l = jnp.sum(p, axis=-1, keepdims=True)
    o = jnp.dot(p.astype(BF16), v, preferred_element_type=F32)
    return o, 1.0 / l


def _pipelined(units, scores, consume):
    s_next = scores(units[0])
    for idx, u in enumerate(units):
        s = s_next
        if idx + 1 < len(units):
            s_next = scores(units[idx + 1])
        consume(u, s)


def _causal_mask():
    r = lax.broadcasted_iota(jnp.int32, (TQ, TQ), 0)
    c = lax.broadcasted_iota(jnp.int32, (TQ, TQ), 1)
    return c <= r


def _dilated_bias_table():
    r = np.arange(TQ)[:, None]
    c = np.arange(SEQ)[None, :]
    d = (SEQ - TQ) + r - c
    mult = np.zeros(d.shape, np.float64)
    for window, dil in DIL_CONFIGS:
        mult += (d >= 0) & (d % dil == 0) & (d <= window)
    return jnp.asarray(np.where(mult > 0, np.log2(np.maximum(mult, 1.0)), NEG), F32)


def _attn_a_kernel(q_ref, k_ref, v_ref, bias_ref, o_ref):
    def scores(i):
        return _nt_dot(q_ref[i * TQ:(i + 1) * TQ, :], k_ref[0:(i + 1) * TQ, :])

    def consume(i, s):
        ext = (i + 1) * TQ
        o, inv_l = _softmax_pv(s + bias_ref[:, SEQ - ext:SEQ], v_ref[0:ext, :])
        o_ref[i * TQ:ext, :] = (o * inv_l).astype(BF16)

    _pipelined(list(range(N_QT)), scores, consume)


def _head_spec(off):
    return pl.BlockSpec((None, None, SEQ, HEAD_DIM), lambda b, h: (b, off + h, 0, 0))


def _attn_a(h_ab, bias):
    batch = h_ab.shape[0]
    return pl.pallas_call(
        _attn_a_kernel,
        out_shape=jax.ShapeDtypeStruct((batch, A_HEADS, SEQ, HEAD_DIM), BF16),
        grid=(batch, A_HEADS),
        in_specs=[_head_spec(0), _head_spec(A_HEADS), _head_spec(2 * A_HEADS),
                  pl.BlockSpec((TQ, SEQ), lambda b, h: (0, 0))],
        out_specs=_head_spec(0),
        compiler_params=pltpu.CompilerParams(
            dimension_semantics=("parallel", "parallel"),
            vmem_limit_bytes=VMEM_LIMIT),
        name="dilated_attn",
    )(h_ab, h_ab, h_ab, bias)


B_Q_OFF, B_K_OFF, B_V_OFF = 12, 16, 20


def _attn_b_kernel(lam_ref, g_ref, q_ref, k_ref, v_ref, o_ref, vcat_ref, *, lam_init):
    vcat_ref[:, :HEAD_DIM] = v_ref[0]
    vcat_ref[:, HEAD_DIM:] = v_ref[1]
    lam_v = lam_ref[...]
    lam = (jnp.exp(jnp.sum(lam_v[0:1] * lam_v[1:2], axis=-1, keepdims=True))
           - jnp.exp(jnp.sum(lam_v[2:3] * lam_v[3:4], axis=-1, keepdims=True)) + lam_init)
    causal = _causal_mask()
    first = {}

    def scores(u):
        i, t = u
        return _nt_dot(q_ref[t, i * TQ:(i + 1) * TQ, :], k_ref[t, 0:(i + 1) * TQ, :])

    def consume(u, s):
        i, t = u
        ext = (i + 1) * TQ
        rows = slice(i * TQ, ext)
        own = jnp.where(causal, s[:, i * TQ:], NEG)
        s = own if i == 0 else jnp.concatenate([s[:, :i * TQ], own], axis=1)
        o, inv_l = _softmax_pv(s, vcat_ref[0:ext, :])
        if t == 0:
            first[i] = o * inv_l
            return
        o = first.pop(i) - lam * (o * inv_l)
        ms = jnp.mean(o * o, axis=-1, keepdims=True)
        y = o * lax.rsqrt(ms + NORM_EPS) * g_ref[...] * (1.0 - lam_init)
        o_ref[0, rows, :] = y[:, :HEAD_DIM].astype(BF16)
        o_ref[1, rows, :] = y[:, HEAD_DIM:].astype(BF16)

    _pipelined([(i, t) for i in range(N_QT) for t in range(2)], scores, consume)


def _attn_b(h_ab, lam_vec, subln_g, lam_init):
    batch = h_ab.shape[0]
    pair_spec = lambda off: pl.BlockSpec((None, 2, SEQ, HEAD_DIM), lambda b, h: (b, off + h, 0, 0))
    return pl.pallas_call(
        functools.partial(_attn_b_kernel, lam_init=lam_init),
        out_shape=jax.ShapeDtypeStruct((batch, 2 * B_HEADS, SEQ, HEAD_DIM), BF16),
        grid=(batch, B_HEADS),
        in_specs=[
            pl.BlockSpec((4, HEAD_DIM), lambda b, h: (0, 0)),
            pl.BlockSpec((1, B_VDIM), lambda b, h: (0, 0)),
            pair_spec(B_Q_OFF), pair_spec(B_K_OFF), pair_spec(B_V_OFF),
        ],
        out_specs=pair_spec(0),
        scratch_shapes=[pltpu.VMEM((SEQ, B_VDIM), BF16)],
        compiler_params=pltpu.CompilerParams(
            dimension_semantics=("parallel", "parallel"),
            vmem_limit_bytes=VMEM_LIMIT),
        name="diff_attn",
    )(lam_vec, subln_g, h_ab, h_ab, h_ab)


KM_ROWS = 16


def _attn_c_kernel(q_ref, k_ref, v_ref, o_ref):
    assert TQ == MOBA_BLOCK
    key_blk = lax.broadcasted_iota(jnp.int32, (KM_ROWS, SEQ), 1) // MOBA_BLOCK
    pool = (lax.broadcasted_iota(jnp.int32, (KM_ROWS, SEQ), 0) == key_blk).astype(BF16)
    km = jnp.dot(pool, k_ref[...], preferred_element_type=F32) * (1.0 / MOBA_BLOCK)
    km_hi = km.astype(BF16)
    km_lo = (km - km_hi.astype(F32)).astype(BF16)
    km_hl = jnp.concatenate([km_hi, km_lo], axis=0)
    row = lax.broadcasted_iota(jnp.int32, (KM_ROWS, TQ), 0)
    pad_rows = jnp.zeros((HEAD_DIM - KM_ROWS, TQ), F32)
    causal = _causal_mask()

    def scores(i):
        q = q_ref[i * TQ:(i + 1) * TQ, :]
        s = _nt_dot(q, k_ref[0:(i + 1) * TQ, :])
        g = _nt_dot(km_hl, q) if i > 0 else None
        return s, g

    def consume(i, sg):
        s, g = sg
        ext = (i + 1) * TQ
        parts = []
        if i > 0:
            past = row < i
            gate = jnp.where(past, g[:KM_ROWS] + g[KM_ROWS:], -jnp.inf)
            rank = jnp.zeros(gate.shape, F32)
            for j in range(i):
                gj = gate[j:j + 1, :]
                beats = (gj > gate) | ((gj == gate) & (j < row))
                rank = rank + beats.astype(F32)
            sel_t = (past & (rank < MOBA_TOPK)).astype(F32)
            sel = jnp.concatenate([sel_t, pad_rows], axis=0).T
            for n in range(i):
                parts.append(jnp.where(sel[:, n:n + 1] > 0.5, s[:, n * TQ:(n + 1) * TQ], NEG))
        parts.append(jnp.where(causal, s[:, i * TQ:], NEG))
        s = parts[0] if i == 0 else jnp.concatenate(parts, axis=1)
        o, inv_l = _softmax_pv(s, v_ref[0:ext, :])
        o_ref[i * TQ:ext, :] = (o * inv_l).astype(BF16)

    _pipelined(list(range(N_QT)), scores, consume)


def _attn_c(h_c):
    batch = h_c.shape[0]
    return pl.pallas_call(
        _attn_c_kernel,
        out_shape=jax.ShapeDtypeStruct((batch, C_HEADS, SEQ, HEAD_DIM), BF16),
        grid=(batch, C_HEADS),
        in_specs=[_head_spec(0), _head_spec(C_HEADS), _head_spec(2 * C_HEADS)],
        out_specs=_head_spec(0),
        compiler_params=pltpu.CompilerParams(
            dimension_semantics=("parallel", "parallel"),
            vmem_limit_bytes=VMEM_LIMIT),
        name="moba_attn",
    )(h_c, h_c, h_c)


OUT_TM = 512
OUT_SUB_M, OUT_SUB_N = 256, 256
Z_BLOCK_OFF = 3


def _outproj_kernel(*refs, n_y, final_norm):
    y_refs = refs[:n_y]
    z_ref, x_ref, w_ref, g_ref, o_ref, ysc_ref, wb_ref = refs[n_y:]
    y_blocks = [(y_ref, hh) for y_ref in y_refs for hh in range(y_ref.shape[0])]

    @pl.when(pl.program_id(0) == 0)
    def _():
        wb_ref[...] = w_ref[...].astype(BF16)
    for r in range(OUT_TM // OUT_SUB_M):
        rows = slice(r * OUT_SUB_M, (r + 1) * OUT_SUB_M)
        for c, (y_ref, hh) in enumerate(y_blocks):
            z = z_ref[c, rows, :].astype(F32)
            gate = z / (1.0 + jnp.exp(-z))
            ysc_ref[rows, c * HEAD_DIM:(c + 1) * HEAD_DIM] = (
                y_ref[hh, rows, :].astype(F32) * gate).astype(BF16)
        ssq = jnp.zeros((OUT_SUB_M, 1), F32)
        for n in range(D_MODEL // OUT_SUB_N):
            cols = slice(n * OUT_SUB_N, (n + 1) * OUT_SUB_N)
            h = x_ref[rows, cols] + jnp.dot(ysc_ref[rows, :], wb_ref[:, cols],
                                            preferred_element_type=F32)
            o_ref[rows, cols] = h
            if final_norm:
                ssq = ssq + jnp.sum(h * h, axis=-1, keepdims=True)
        if final_norm:
            inv = lax.rsqrt(ssq * (1.0 / D_MODEL) + NORM_EPS)
            o_ref[rows, :] = o_ref[rows, :] * inv * g_ref[...]


def _outproj(ys, h_in, x2d, w, g_final, batch, final_norm):
    rows = x2d.shape[0]
    s_tiles = SEQ // OUT_TM
    n_z = D_MODEL // HEAD_DIM
    y_specs = [pl.BlockSpec((None, y.shape[1], OUT_TM, HEAD_DIM),
                            lambda i: (i // s_tiles, 0, i % s_tiles, 0)) for y in ys]
    assert sum(y.shape[1] for y in ys) == n_z
    return pl.pallas_call(
        functools.partial(_outproj_kernel, n_y=len(ys), final_norm=final_norm),
        out_shape=jax.ShapeDtypeStruct((rows, D_MODEL), F32),
        grid=(rows // OUT_TM,),
        in_specs=y_specs + [
            pl.BlockSpec((None, n_z, OUT_TM, HEAD_DIM),
                         lambda i: (i // s_tiles, Z_BLOCK_OFF, i % s_tiles, 0)),
            pl.BlockSpec((OUT_TM, D_MODEL), lambda i: (i, 0)),
            pl.BlockSpec((D_MODEL, D_MODEL), lambda i: (0, 0), pipeline_mode=pl.Buffered(1)),
            pl.BlockSpec((1, D_MODEL), lambda i: (0, 0)),
        ],
        out_specs=pl.BlockSpec((OUT_TM, D_MODEL), lambda i: (i, 0)),
        scratch_shapes=[pltpu.VMEM((OUT_TM, D_MODEL), BF16), pltpu.VMEM((D_MODEL, D_MODEL), BF16)],
        compiler_params=pltpu.CompilerParams(
            dimension_semantics=("arbitrary",),
            vmem_limit_bytes=VMEM_LIMIT),
        name="gate_outproj",
    )(*ys, h_in, x2d, w, g_final)


def _rope_tables():
    inv = 1.0 / (ROPE_THETA ** (jnp.arange(0, HEAD_DIM, 2, dtype=F32) / HEAD_DIM))
    ang = jnp.arange(SEQ, dtype=F32)[:, None] * inv[None, :]
    cos = jnp.concatenate([jnp.cos(ang), jnp.cos(ang)], axis=-1)
    sin = jnp.concatenate([-jnp.sin(ang), jnp.sin(ang)], axis=-1)
    return cos, sin


def _tile_kinds(group_kinds, cols_per_group):
    kinds = []
    for kind, cols in zip(group_kinds, cols_per_group):
        assert cols % IN_TN == 0
        kinds += [kind] * (cols // IN_TN)
    assert len(kinds) == IN_WIDTH // IN_TN
    return jnp.asarray(kinds, jnp.int32)


def kernel(x, norm_ab, w_in_ab, w_out_ab, lam_ab, subln_ab, norm_c, w_in_c, w_out_c, final_norm):
    batch, seq, d = x.shape
    assert (seq, d) == (SEQ, D_MODEL) and x.dtype == F32
    assert w_in_ab.shape == (1, D_MODEL, IN_WIDTH) and w_in_c.shape == (1, D_MODEL, IN_WIDTH)
    cos, sin = _rope_tables()
    aw, bw = A_HEADS * HEAD_DIM, B_HEADS * B_VDIM
    kinds_ab = _tile_kinds(
        [KIND_ROPE_SCALED, KIND_ROPE, KIND_PLAIN, KIND_ROPE_SCALED, KIND_ROPE, KIND_PLAIN, KIND_PLAIN],
        [aw, aw, aw, bw, bw, bw, aw + bw])
    cw = C_HEADS * HEAD_DIM
    kinds_c = _tile_kinds([KIND_ROPE_SCALED, KIND_ROPE, KIND_PLAIN, KIND_PLAIN], [cw, cw, cw, cw])

    x2d = x.reshape(batch * seq, d)

    g_final = final_norm[None, :]
    h_ab = _inproj(x2d, norm_ab[0][None, :], w_in_ab[0], cos, sin, kinds_ab, batch)
    ya = _attn_a(h_ab, _dilated_bias_table())
    lam_init = 0.8 - 0.6 * math.exp(-0.3 * 0)
    yb = _attn_b(h_ab, lam_ab[0], subln_ab[0][None, :], lam_init)
    x1 = _outproj([ya, yb], h_ab, x2d, w_out_ab[0], g_final, batch, final_norm=False)

    h_c = _inproj(x1, norm_c[0][None, :], w_in_c[0], cos, sin, kinds_c, batch)
    yc = _attn_c(h_c)
    out = _outproj([yc], h_c, x1, w_out_c[0], g_final, batch, final_norm=True)
    return out.reshape(batch, seq, d)
```

```python
import functools
import math

import numpy as np
import jax
import jax.numpy as jnp
from jax import lax
from jax.experimental import pallas as pl
from jax.experimental.pallas import tpu as pltpu

D_MODEL = 2048
SEQ = 2048
HEAD_DIM = 128
ROPE_THETA = 10000.0
NORM_EPS = 1e-6
A_HEADS = 8
DIL_CONFIGS = ((128, 1), (512, 4), (2048, 16))
B_HEADS = 4
B_VDIM = 2 * HEAD_DIM
C_HEADS = 16
MOBA_BLOCK = 256
MOBA_TOPK = 3
N_MOBA_BLOCKS = SEQ // MOBA_BLOCK
IN_WIDTH = 8192
N_COL_BLOCKS = IN_WIDTH // HEAD_DIM
SCALE = HEAD_DIM ** -0.5
Q_SCALE = SCALE * math.log2(math.e)
NEG = -1e30

BF16 = jnp.bfloat16
F32 = jnp.float32

KIND_PLAIN, KIND_ROPE, KIND_ROPE_SCALED = 0, 1, 2

VMEM_LIMIT = 56 * 1024 * 1024


def _nt_dot(a, b):
    return lax.dot_general(a, b, (((1,), (1,)), ((), ())), preferred_element_type=F32)


IN_TM, IN_TN = 2048, 1024
IN_SUB_M, IN_SUB_N = 256, 256


def _inproj_kernel(kinds_ref, x_ref, g_ref, w_ref, cos_ref, sin_ref, o_ref, xn_ref):
    j = pl.program_id(1)

    @pl.when(j == 0)
    def _():
        x = x_ref[...]
        ms = jnp.mean(x * x, axis=-1, keepdims=True)
        xn_ref[...] = (x * lax.rsqrt(ms + NORM_EPS) * g_ref[...]).astype(BF16)

    kind = kinds_ref[j]
    scale = jnp.where(kind == KIND_ROPE_SCALED, Q_SCALE, 1.0).astype(F32)
    for nh in range(IN_TN // IN_SUB_N):
        w = w_ref[:, nh * IN_SUB_N:(nh + 1) * IN_SUB_N].astype(BF16)
        for mh in range(IN_TM // IN_SUB_M):
            rows = slice(mh * IN_SUB_M, (mh + 1) * IN_SUB_M)
            acc = jnp.dot(xn_ref[rows, :], w, preferred_element_type=F32)
            cos = jnp.where(kind == KIND_PLAIN, 1.0, cos_ref[rows, :] * scale)
            sin = jnp.where(kind == KIND_PLAIN, 0.0, sin_ref[rows, :] * scale)
            for c in range(IN_SUB_N // HEAD_DIM):
                t = acc[:, c * HEAD_DIM:(c + 1) * HEAD_DIM]
                o_ref[nh * (IN_SUB_N // HEAD_DIM) + c, rows, :] = (
                    t * cos + pltpu.roll(t, HEAD_DIM // 2, 1) * sin).astype(BF16)


def _inproj(x2d, g, w, cos, sin_signed, kinds, batch):
    rows = x2d.shape[0]
    s_tiles = SEQ // IN_TM
    grid = (rows // IN_TM, IN_WIDTH // IN_TN)
    blk_per_tile = IN_TN // HEAD_DIM
    return pl.pallas_call(
        _inproj_kernel,
        out_shape=jax.ShapeDtypeStruct((batch, N_COL_BLOCKS, SEQ, HEAD_DIM), BF16),
        grid_spec=pltpu.PrefetchScalarGridSpec(
            num_scalar_prefetch=1,
            grid=grid,
            in_specs=[
                pl.BlockSpec((IN_TM, D_MODEL), lambda i, j, k: (i, 0), pipeline_mode=pl.Buffered(1)),
                pl.BlockSpec((1, D_MODEL), lambda i, j, k: (0, 0)),
                pl.BlockSpec((D_MODEL, IN_TN), lambda i, j, k: (0, j)),
                pl.BlockSpec((IN_TM, HEAD_DIM), lambda i, j, k: (i % s_tiles, 0)),
                pl.BlockSpec((IN_TM, HEAD_DIM), lambda i, j, k: (i % s_tiles, 0)),
            ],
            out_specs=pl.BlockSpec((None, blk_per_tile, IN_TM, HEAD_DIM),
                                   lambda i, j, k: (i // s_tiles, j, i % s_tiles, 0)),
            scratch_shapes=[pltpu.VMEM((IN_TM, D_MODEL), BF16)],
        ),
        compiler_params=pltpu.CompilerParams(
            dimension_semantics=("parallel", "arbitrary"),
            vmem_limit_bytes=VMEM_LIMIT),
        name="norm_inproj",
    )(kinds, x2d, g, w, cos, sin_signed)


TQ = 256
N_QT = SEQ // TQ


def _softmax_pv(s, v):
    m = jnp.max(s, axis=-1, keepdims=True)
    p = jnp.exp2(s - m)
    l = jnp.sum(p, axis=-1, keepdims=True)
    o = jnp.dot(p.astype(BF16), v, preferred_element_type=F32)
    return o, 1.0 / l


def _pipelined(units, scores, consume):
    s_next = scores(units[0])
    for idx, u in enumerate(units):
        s = s_next
        if idx + 1 < len(units):
            s_next = scores(units[idx + 1])
        consume(u, s)


def _causal_mask():
    r = lax.broadcasted_iota(jnp.int32, (TQ, TQ), 0)
    c = lax.broadcasted_iota(jnp.int32, (TQ, TQ), 1)
    return c <= r


def _dilated_bias_table():
    r = np.arange(TQ)[:, None]
    c = np.arange(SEQ)[None, :]
    d = (SEQ - TQ) + r - c
    mult = np.zeros(d.shape, np.float64)
    for window, dil in DIL_CONFIGS:
        mult += (d >= 0) & (d % dil == 0) & (d <= window)
    return jnp.asarray(np.where(mult > 0, np.log2(np.maximum(mult, 1.0)), NEG), F32)


def _per_head(head_kernel, n_shared, n_scratch=0, slabs_per_head=1):
    def group_kernel(*refs):
        shared = refs[:n_shared]
        grouped = refs[n_shared:len(refs) - n_scratch]
        scratch = refs[len(refs) - n_scratch:]

        def view(r, g):
            return r.at[g] if slabs_per_head == 1 else r.at[pl.ds(g * slabs_per_head, slabs_per_head)]

        def body(g, carry):
            head_kernel(*shared, *[view(r, g) for r in grouped], *scratch)
            return carry

        lax.fori_loop(0, grouped[0].shape[0] // slabs_per_head, body, 0)
    return group_kernel


def _heads_spec(off, n):
    assert off % n == 0
    return pl.BlockSpec((None, n, SEQ, HEAD_DIM), lambda b, h: (b, off // n + h, 0, 0))


HEADS_PER_STEP = 4


def _attn_a_kernel(bias_ref, q_ref, k_ref, v_ref, o_ref):
    def scores(i):
        return _nt_dot(q_ref[i * TQ:(i + 1) * TQ, :], k_ref[0:(i + 1) * TQ, :])

    def consume(i, s):
        ext = (i + 1) * TQ
        o, inv_l = _softmax_pv(s + bias_ref[:, SEQ - ext:SEQ], v_ref[0:ext, :])
        o_ref[i * TQ:ext, :] = (o * inv_l).astype(BF16)

    _pipelined(list(range(N_QT)), scores, consume)


def _attn_a(h_ab, bias):
    batch, n = h_ab.shape[0], HEADS_PER_STEP
    return pl.pallas_call(
        _per_head(_attn_a_kernel, 1),
        out_shape=jax.ShapeDtypeStruct((batch, A_HEADS, SEQ, HEAD_DIM), BF16),
        grid=(batch, A_HEADS // n),
        in_specs=[pl.BlockSpec((TQ, SEQ), lambda b, h: (0, 0)),
                  _heads_spec(0, n), _heads_spec(A_HEADS, n), _heads_spec(2 * A_HEADS, n)],
        out_specs=_heads_spec(0, n),
        compiler_params=pltpu.CompilerParams(
            dimension_semantics=("parallel", "parallel"),
            vmem_limit_bytes=VMEM_LIMIT),
        name="dilated_attn",
    )(bias, h_ab, h_ab, h_ab)


B_Q_OFF, B_K_OFF, B_V_OFF = 24, 32, 40


def _attn_b_kernel(lam_ref, g_ref, q_ref, k_ref, v_ref, o_ref, vcat_ref, *, lam_init):
    vcat_ref[:, :HEAD_DIM] = v_ref[0]
    vcat_ref[:, HEAD_DIM:] = v_ref[1]
    lam_v = lam_ref[...]
    lam = (jnp.exp(jnp.sum(lam_v[0:1] * lam_v[1:2], axis=-1, keepdims=True))
           - jnp.exp(jnp.sum(lam_v[2:3] * lam_v[3:4], axis=-1, keepdims=True)) + lam_init)
    causal = _causal_mask()
    first = {}

    def scores(u):
        i, t = u
        return _nt_dot(q_ref[t, i * TQ:(i + 1) * TQ, :], k_ref[t, 0:(i + 1) * TQ, :])

    def consume(u, s):
        i, t = u
        ext = (i + 1) * TQ
        rows = slice(i * TQ, ext)
        own = jnp.where(causal, s[:, i * TQ:], NEG)
        s = own if i == 0 else jnp.concatenate([s[:, :i * TQ], own], axis=1)
        o, inv_l = _softmax_pv(s, vcat_ref[0:ext, :])
        if t == 0:
            first[i] = o * inv_l
            return
        o = first.pop(i) - lam * (o * inv_l)
        ms = jnp.mean(o * o, axis=-1, keepdims=True)
        y = o * lax.rsqrt(ms + NORM_EPS) * g_ref[...] * (1.0 - lam_init)
        o_ref[0, rows, :] = y[:, :HEAD_DIM].astype(BF16)
        o_ref[1, rows, :] = y[:, HEAD_DIM:].astype(BF16)

    _pipelined([(i, t) for i in range(N_QT) for t in range(2)], scores, consume)


def _attn_b(h_ab, lam_vec, subln_g, lam_init):
    batch = h_ab.shape[0]
    heads = HEADS_PER_STEP // 2
    spec = lambda off: _heads_spec(off, 2 * heads)
    return pl.pallas_call(
        _per_head(functools.partial(_attn_b_kernel, lam_init=lam_init), 2, n_scratch=1,
                  slabs_per_head=2),
        out_shape=jax.ShapeDtypeStruct((batch, 2 * B_HEADS, SEQ, HEAD_DIM), BF16),
        grid=(batch, B_HEADS // heads),
        in_specs=[
            pl.BlockSpec((4, HEAD_DIM), lambda b, h: (0, 0)),
            pl.BlockSpec((1, B_VDIM), lambda b, h: (0, 0)),
            spec(B_Q_OFF), spec(B_K_OFF), spec(B_V_OFF),
        ],
        out_specs=spec(0),
        scratch_shapes=[pltpu.VMEM((SEQ, B_VDIM), BF16)],
        compiler_params=pltpu.CompilerParams(
            dimension_semantics=("parallel", "parallel"),
            vmem_limit_bytes=VMEM_LIMIT),
        name="diff_attn",
    )(lam_vec, subln_g, h_ab, h_ab, h_ab)


KM_ROWS = 16


def _attn_c_kernel(q_ref, k_ref, v_ref, o_ref):
    assert TQ == MOBA_BLOCK
    key_blk = lax.broadcasted_iota(jnp.int32, (KM_ROWS, SEQ), 1) // MOBA_BLOCK
    pool = (lax.broadcasted_iota(jnp.int32, (KM_ROWS, SEQ), 0) == key_blk).astype(BF16)
    km = jnp.dot(pool, k_ref[...], preferred_element_type=F32) * (1.0 / MOBA_BLOCK)
    km_hi = km.astype(BF16)
    km_lo = (km - km_hi.astype(F32)).astype(BF16)
    km_hl = jnp.concatenate([km_hi, km_lo], axis=0)
    row = lax.broadcasted_iota(jnp.int32, (KM_ROWS, TQ), 0)
    pad_rows = jnp.zeros((HEAD_DIM - KM_ROWS, TQ), F32)
    causal = _causal_mask()

    def scores(i):
        q = q_ref[i * TQ:(i + 1) * TQ, :]
        s = _nt_dot(q, k_ref[0:(i + 1) * TQ, :])
        g = _nt_dot(km_hl, q) if i > 0 else None
        return s, g

    def consume(i, sg):
        s, g = sg
        ext = (i + 1) * TQ
        parts = []
        if i > 0:
            past = row < i
            gate = jnp.where(past, g[:KM_ROWS] + g[KM_ROWS:], -jnp.inf)
            rank = jnp.zeros(gate.shape, F32)
            for j in range(i):
                gj = gate[j:j + 1, :]
                beats = (gj > gate) | ((gj == gate) & (j < row))
                rank = rank + beats.astype(F32)
            sel_t = (past & (rank < MOBA_TOPK)).astype(F32)
            sel = jnp.concatenate([sel_t, pad_rows], axis=0).T
            for n in range(i):
                parts.append(jnp.where(sel[:, n:n + 1] > 0.5, s[:, n * TQ:(n + 1) * TQ], NEG))
        parts.append(jnp.where(causal, s[:, i * TQ:], NEG))
        s = parts[0] if i == 0 else jnp.concatenate(parts, axis=1)
        o, inv_l = _softmax_pv(s, v_ref[0:ext, :])
        o_ref[i * TQ:ext, :] = (o * inv_l).astype(BF16)

    _pipelined(list(range(N_QT)), scores, consume)


def _attn_c(h_c):
    batch, n = h_c.shape[0], HEADS_PER_STEP
    return pl.pallas_call(
        _per_head(_attn_c_kernel, 0),
        out_shape=jax.ShapeDtypeStruct((batch, C_HEADS, SEQ, HEAD_DIM), BF16),
        grid=(batch, C_HEADS // n),
        in_specs=[_heads_spec(0, n), _heads_spec(C_HEADS, n), _heads_spec(2 * C_HEADS, n)],
        out_specs=_heads_spec(0, n),
        compiler_params=pltpu.CompilerParams(
            dimension_semantics=("parallel", "parallel"),
            vmem_limit_bytes=VMEM_LIMIT),
        name="moba_attn",
    )(h_c, h_c, h_c)


OUT_TM = 512
OUT_SUB_M, OUT_SUB_N = 256, 256
Z_BLOCK_OFF = 3


def _outproj_kernel(*refs, n_y, final_norm):
    y_refs = refs[:n_y]
    z_ref, x_ref, w_ref, g_ref, o_ref, ysc_ref, wb_ref = refs[n_y:]
    y_blocks = [(y_ref, hh) for y_ref in y_refs for hh in range(y_ref.shape[0])]

    @pl.when(pl.program_id(0) == 0)
    def _():
        wb_ref[...] = w_ref[...].astype(BF16)
    for r in range(OUT_TM // OUT_SUB_M):
        rows = slice(r * OUT_SUB_M, (r + 1) * OUT_SUB_M)
        for c, (y_ref, hh) in enumerate(y_blocks):
            z = z_ref[c, rows, :].astype(F32)
            gate = z / (1.0 + jnp.exp(-z))
            ysc_ref[rows, c * HEAD_DIM:(c + 1) * HEAD_DIM] = (
                y_ref[hh, rows, :].astype(F32) * gate).astype(BF16)
        ssq = jnp.zeros((OUT_SUB_M, 1), F32)
        for n in range(D_MODEL // OUT_SUB_N):
            cols = slice(n * OUT_SUB_N, (n + 1) * OUT_SUB_N)
            h = x_ref[rows, cols] + jnp.dot(ysc_ref[rows, :], wb_ref[:, cols],
                                            preferred_element_type=F32)
            o_ref[rows, cols] = h
            if final_norm:
                ssq = ssq + jnp.sum(h * h, axis=-1, keepdims=True)
        if final_norm:
            inv = lax.rsqrt(ssq * (1.0 / D_MODEL) + NORM_EPS)
            o_ref[rows, :] = o_ref[rows, :] * inv * g_ref[...]


def _outproj(ys, h_in, x2d, w, g_final, batch, final_norm):
    rows = x2d.shape[0]
    s_tiles = SEQ // OUT_TM
    n_z = D_MODEL // HEAD_DIM
    y_specs = [pl.BlockSpec((None, y.shape[1], OUT_TM, HEAD_DIM),
                            lambda i: (i // s_tiles, 0, i % s_tiles, 0)) for y in ys]
    assert sum(y.shape[1] for y in ys) == n_z
    return pl.pallas_call(
        functools.partial(_outproj_kernel, n_y=len(ys), final_norm=final_norm),
        out_shape=jax.ShapeDtypeStruct((rows, D_MODEL), F32),
        grid=(rows // OUT_TM,),
        in_specs=y_specs + [
            pl.BlockSpec((None, n_z, OUT_TM, HEAD_DIM),
                         lambda i: (i // s_tiles, Z_BLOCK_OFF, i % s_tiles, 0)),
            pl.BlockSpec((OUT_TM, D_MODEL), lambda i: (i, 0)),
            pl.BlockSpec((D_MODEL, D_MODEL), lambda i: (0, 0), pipeline_mode=pl.Buffered(1)),
            pl.BlockSpec((1, D_MODEL), lambda i: (0, 0)),
        ],
        out_specs=pl.BlockSpec((OUT_TM, D_MODEL), lambda i: (i, 0)),
        scratch_shapes=[pltpu.VMEM((OUT_TM, D_MODEL), BF16), pltpu.VMEM((D_MODEL, D_MODEL), BF16)],
        compiler_params=pltpu.CompilerParams(
            dimension_semantics=("arbitrary",),
            vmem_limit_bytes=VMEM_LIMIT),
        name="gate_outproj",
    )(*ys, h_in, x2d, w, g_final)


def _rope_tables():
    inv = 1.0 / (ROPE_THETA ** (jnp.arange(0, HEAD_DIM, 2, dtype=F32) / HEAD_DIM))
    ang = jnp.arange(SEQ, dtype=F32)[:, None] * inv[None, :]
    cos = jnp.concatenate([jnp.cos(ang), jnp.cos(ang)], axis=-1)
    sin = jnp.concatenate([-jnp.sin(ang), jnp.sin(ang)], axis=-1)
    return cos, sin


def _tile_kinds(group_kinds, cols_per_group):
    kinds = []
    for kind, cols in zip(group_kinds, cols_per_group):
        assert cols % IN_TN == 0
        kinds += [kind] * (cols // IN_TN)
    assert len(kinds) == IN_WIDTH // IN_TN
    return jnp.asarray(kinds, jnp.int32)


def kernel(x, norm_ab, w_in_ab, w_out_ab, lam_ab, subln_ab, norm_c, w_in_c, w_out_c, final_norm):
    batch, seq, d = x.shape
    assert (seq, d) == (SEQ, D_MODEL) and x.dtype == F32
    assert w_in_ab.shape == (1, D_MODEL, IN_WIDTH) and w_in_c.shape == (1, D_MODEL, IN_WIDTH)
    cos, sin = _rope_tables()
    aw, bw = A_HEADS * HEAD_DIM, B_HEADS * B_VDIM
    kinds_ab = _tile_kinds(
        [KIND_ROPE_SCALED, KIND_ROPE, KIND_PLAIN, KIND_ROPE_SCALED, KIND_ROPE, KIND_PLAIN, KIND_PLAIN],
        [aw, aw, aw, bw, bw, bw, aw + bw])
    cw = C_HEADS * HEAD_DIM
    kinds_c = _tile_kinds([KIND_ROPE_SCALED, KIND_ROPE, KIND_PLAIN, KIND_PLAIN], [cw, cw, cw, cw])

    x2d = x.reshape(batch * seq, d)

    g_final = final_norm[None, :]
    h_ab = _inproj(x2d, norm_ab[0][None, :], w_in_ab[0], cos, sin, kinds_ab, batch)
    ya = _attn_a(h_ab, _dilated_bias_table())
    lam_init = 0.8 - 0.6 * math.exp(-0.3 * 0)
    yb = _attn_b(h_ab, lam_ab[0], subln_ab[0][None, :], lam_init)
    x1 = _outproj([ya, yb], h_ab, x2d, w_out_ab[0], g_final, batch, final_norm=False)

    h_c = _inproj(x1, norm_c[0][None, :], w_in_c[0], cos, sin, kinds_c, batch)
    yc = _attn_c(h_c)
    out = _outproj([yc], h_c, x1, w_out_c[0], g_final, batch, final_norm=True)
    return out.reshape(batch, seq, d)
```

```python
import functools
import math

import numpy as np
import jax
import jax.numpy as jnp
from jax import lax
from jax.experimental import pallas as pl
from jax.experimental.pallas import tpu as pltpu

D_MODEL = 2048
SEQ = 2048
HEAD_DIM = 128
ROPE_THETA = 10000.0
NORM_EPS = 1e-6
A_HEADS = 8
DIL_CONFIGS = ((128, 1), (512, 4), (2048, 16))
B_HEADS = 4
B_VDIM = 2 * HEAD_DIM
C_HEADS = 16
MOBA_BLOCK = 256
MOBA_TOPK = 3
N_MOBA_BLOCKS = SEQ // MOBA_BLOCK
IN_WIDTH = 8192
N_COL_BLOCKS = IN_WIDTH // HEAD_DIM
SCALE = HEAD_DIM ** -0.5
Q_SCALE = SCALE * math.log2(math.e)
NEG = -1e30

BF16 = jnp.bfloat16
F32 = jnp.float32

KIND_PLAIN, KIND_ROPE, KIND_ROPE_SCALED = 0, 1, 2

VMEM_LIMIT = 56 * 1024 * 1024


def _nt_dot(a, b):
    return lax.dot_general(a, b, (((1,), (1,)), ((), ())), preferred_element_type=F32)


IN_TM, IN_TN = 2048, 1024
IN_SUB_M, IN_SUB_N = 256, 256


def _inproj_kernel(kinds_ref, x_ref, g_ref, w_ref, cos_ref, sin_ref, o_ref, xn_ref):
    j = pl.program_id(1)

    @pl.when(j == 0)
    def _():
        x = x_ref[...]
        ms = jnp.mean(x * x, axis=-1, keepdims=True)
        xn_ref[...] = (x * lax.rsqrt(ms + NORM_EPS) * g_ref[...]).astype(BF16)

    kind = kinds_ref[j]
    scale = jnp.where(kind == KIND_ROPE_SCALED, Q_SCALE, 1.0).astype(F32)
    for nh in range(IN_TN // IN_SUB_N):
        w = w_ref[:, nh * IN_SUB_N:(nh + 1) * IN_SUB_N].astype(BF16)
        for mh in range(IN_TM // IN_SUB_M):
            rows = slice(mh * IN_SUB_M, (mh + 1) * IN_SUB_M)
            acc = jnp.dot(xn_ref[rows, :], w, preferred_element_type=F32)
            cos = jnp.where(kind == KIND_PLAIN, 1.0, cos_ref[rows, :] * scale)
            sin = jnp.where(kind == KIND_PLAIN, 0.0, sin_ref[rows, :] * scale)
            for c in range(IN_SUB_N // HEAD_DIM):
                t = acc[:, c * HEAD_DIM:(c + 1) * HEAD_DIM]
                o_ref[nh * (IN_SUB_N // HEAD_DIM) + c, rows, :] = (
                    t * cos + pltpu.roll(t, HEAD_DIM // 2, 1) * sin).astype(BF16)


def _inproj(x2d, g, w, cos, sin_signed, kinds, batch):
    rows = x2d.shape[0]
    s_tiles = SEQ // IN_TM
    grid = (rows // IN_TM, IN_WIDTH // IN_TN)
    blk_per_tile = IN_TN // HEAD_DIM
    return pl.pallas_call(
        _inproj_kernel,
        out_shape=jax.ShapeDtypeStruct((batch, N_COL_BLOCKS, SEQ, HEAD_DIM), BF16),
        grid_spec=pltpu.PrefetchScalarGridSpec(
            num_scalar_prefetch=1,
            grid=grid,
            in_specs=[
                pl.BlockSpec((IN_TM, D_MODEL), lambda i, j, k: (i, 0), pipeline_mode=pl.Buffered(1)),
                pl.BlockSpec((1, D_MODEL), lambda i, j, k: (0, 0)),
                pl.BlockSpec((D_MODEL, IN_TN), lambda i, j, k: (0, j)),
                pl.BlockSpec((IN_TM, HEAD_DIM), lambda i, j, k: (i % s_tiles, 0)),
                pl.BlockSpec((IN_TM, HEAD_DIM), lambda i, j, k: (i % s_tiles, 0)),
            ],
            out_specs=pl.BlockSpec((None, blk_per_tile, IN_TM, HEAD_DIM),
                                   lambda i, j, k: (i // s_tiles, j, i % s_tiles, 0)),
            scratch_shapes=[pltpu.VMEM((IN_TM, D_MODEL), BF16)],
        ),
        compiler_params=pltpu.CompilerParams(
            dimension_semantics=("parallel", "arbitrary"),
            vmem_limit_bytes=VMEM_LIMIT),
        name="norm_inproj",
    )(kinds, x2d, g, w, cos, sin_signed)


TQ = 256
N_QT = SEQ // TQ
PART = 256


def _softmax_pv_keymajor(parts, vt):
    m = functools.reduce(jnp.maximum, [jnp.max(s, axis=0, keepdims=True) for s in parts])
    ps = [jnp.exp2(s - m) for s in parts]
    inv_l = 1.0 / sum(jnp.sum(p, axis=0, keepdims=True) for p in ps)
    p_t = jnp.concatenate([p.astype(BF16) for p in ps], axis=0)
    o_t = jnp.dot(vt, p_t, preferred_element_type=F32)
    return (o_t * inv_l).T


def _softmax_pv_querymajor(s, v):
    m = jnp.max(s, axis=-1, keepdims=True)
    p = jnp.exp2(s - m)
    inv_l = 1.0 / jnp.sum(p, axis=-1, keepdims=True)
    return jnp.dot(p.astype(BF16), v, preferred_element_type=F32) * inv_l


def _pipelined(units, scores, consume):
    s_next = scores(units[0])
    for idx, u in enumerate(units):
        s = s_next
        if idx + 1 < len(units):
            s_next = scores(units[idx + 1])
        consume(u, s)


def _key_le_query_mask(key_axis=0):
    key = lax.broadcasted_iota(jnp.int32, (TQ, TQ), key_axis)
    query = lax.broadcasted_iota(jnp.int32, (TQ, TQ), 1 - key_axis)
    return key <= query


def _dilated_bias_table():
    key = np.arange(SEQ)[:, None]
    query = (SEQ - TQ) + np.arange(TQ)[None, :]
    d = query - key
    mult = np.zeros(d.shape, np.float64)
    for window, dil in DIL_CONFIGS:
        mult += (d >= 0) & (d % dil == 0) & (d <= window)
    return jnp.asarray(np.where(mult > 0, np.log2(np.maximum(mult, 1.0)), NEG), F32)


def _per_head(head_kernel, n_shared, n_scratch=0, slabs_per_head=1):
    def group_kernel(*refs):
        shared = refs[:n_shared]
        grouped = refs[n_shared:len(refs) - n_scratch]
        scratch = refs[len(refs) - n_scratch:]

        def view(r, g):
            return r.at[g] if slabs_per_head == 1 else r.at[pl.ds(g * slabs_per_head, slabs_per_head)]

        def body(g, carry):
            head_kernel(*shared, *[view(r, g) for r in grouped], *scratch)
            return carry

        lax.fori_loop(0, grouped[0].shape[0] // slabs_per_head, body, 0)
    return group_kernel


def _heads_spec(off, n):
    assert off % n == 0
    return pl.BlockSpec((None, n, SEQ, HEAD_DIM), lambda b, h: (b, off // n + h, 0, 0))


HEADS_PER_STEP = 4


def _attn_a_kernel(bias_ref, q_ref, k_ref, v_ref, o_ref, vt_ref):
    vt_ref[...] = v_ref[...].astype(F32).T.astype(BF16)

    def scores(i):
        return _nt_dot(k_ref[0:(i + 1) * TQ, :], q_ref[i * TQ:(i + 1) * TQ, :])

    def consume(i, s_t):
        ext = (i + 1) * TQ
        off = SEQ - ext
        parts = [s_t[r:r + PART, :] + bias_ref[off + r:off + r + PART, :] for r in range(0, ext, PART)]
        o = _softmax_pv_keymajor(parts, vt_ref[:, 0:ext])
        o_ref[i * TQ:ext, :] = o.astype(BF16)

    _pipelined(list(range(N_QT)), scores, consume)


def _attn_a(h_ab, bias):
    batch, n = h_ab.shape[0], HEADS_PER_STEP
    return pl.pallas_call(
        _per_head(_attn_a_kernel, 1, n_scratch=1),
        out_shape=jax.ShapeDtypeStruct((batch, A_HEADS, SEQ, HEAD_DIM), BF16),
        grid=(batch, A_HEADS // n),
        in_specs=[pl.BlockSpec((SEQ, TQ), lambda b, h: (0, 0)),
                  _heads_spec(0, n), _heads_spec(A_HEADS, n), _heads_spec(2 * A_HEADS, n)],
        out_specs=_heads_spec(0, n),
        scratch_shapes=[pltpu.VMEM((HEAD_DIM, SEQ), BF16)],
        compiler_params=pltpu.CompilerParams(
            dimension_semantics=("parallel", "parallel"),
            vmem_limit_bytes=VMEM_LIMIT),
        name="dilated_attn",
    )(bias, h_ab, h_ab, h_ab)


B_Q_OFF, B_K_OFF, B_V_OFF = 24, 32, 40


def _attn_b_kernel(lam_ref, g_ref, q_ref, k_ref, v_ref, o_ref, vcat_ref, *, lam_init):
    vcat_ref[:, :HEAD_DIM] = v_ref[0]
    vcat_ref[:, HEAD_DIM:] = v_ref[1]
    lam_v = lam_ref[...]
    lam = (jnp.exp(jnp.sum(lam_v[0:1] * lam_v[1:2], axis=-1, keepdims=True))
           - jnp.exp(jnp.sum(lam_v[2:3] * lam_v[3:4], axis=-1, keepdims=True)) + lam_init)
    query_ge_key = _key_le_query_mask(key_axis=1)
    first = {}

    def scores(u):
        i, t = u
        return _nt_dot(q_ref[t, i * TQ:(i + 1) * TQ, :], k_ref[t, 0:(i + 1) * TQ, :])

    def consume(u, s):
        i, t = u
        ext = (i + 1) * TQ
        rows = slice(i * TQ, ext)
        own = jnp.where(query_ge_key, s[:, i * TQ:], NEG)
        s = own if i == 0 else jnp.concatenate([s[:, :i * TQ], own], axis=1)
        o = _softmax_pv_querymajor(s, vcat_ref[0:ext, :])
        if t == 0:
            first[i] = o
            return
        o = first.pop(i) - lam * o
        ms = jnp.mean(o * o, axis=-1, keepdims=True)
        y = o * lax.rsqrt(ms + NORM_EPS) * g_ref[...] * (1.0 - lam_init)
        o_ref[0, rows, :] = y[:, :HEAD_DIM].astype(BF16)
        o_ref[1, rows, :] = y[:, HEAD_DIM:].astype(BF16)

    _pipelined([(i, t) for i in range(N_QT) for t in range(2)], scores, consume)


def _attn_b(h_ab, lam_vec, subln_g, lam_init):
    batch = h_ab.shape[0]
    heads = HEADS_PER_STEP // 2
    spec = lambda off: _heads_spec(off, 2 * heads)
    return pl.pallas_call(
        _per_head(functools.partial(_attn_b_kernel, lam_init=lam_init), 2, n_scratch=1,
                  slabs_per_head=2),
        out_shape=jax.ShapeDtypeStruct((batch, 2 * B_HEADS, SEQ, HEAD_DIM), BF16),
        grid=(batch, B_HEADS // heads),
        in_specs=[
            pl.BlockSpec((4, HEAD_DIM), lambda b, h: (0, 0)),
            pl.BlockSpec((1, B_VDIM), lambda b, h: (0, 0)),
            spec(B_Q_OFF), spec(B_K_OFF), spec(B_V_OFF),
        ],
        out_specs=spec(0),
        scratch_shapes=[pltpu.VMEM((SEQ, B_VDIM), BF16)],
        compiler_params=pltpu.CompilerParams(
            dimension_semantics=("parallel", "parallel"),
            vmem_limit_bytes=VMEM_LIMIT),
        name="diff_attn",
    )(lam_vec, subln_g, h_ab, h_ab, h_ab)


KM_ROWS = 16


def _attn_c_kernel(q_ref, k_ref, v_ref, o_ref, vt_ref):
    assert TQ == MOBA_BLOCK
    key_blk = lax.broadcasted_iota(jnp.int32, (KM_ROWS, SEQ), 1) // MOBA_BLOCK
    pool = (lax.broadcasted_iota(jnp.int32, (KM_ROWS, SEQ), 0) == key_blk).astype(BF16)
    km = jnp.dot(pool, k_ref[...], preferred_element_type=F32) * (1.0 / MOBA_BLOCK)
    km_hi = km.astype(BF16)
    km_lo = (km - km_hi.astype(F32)).astype(BF16)
    km_hl = jnp.concatenate([km_hi, km_lo], axis=0)
    row = lax.broadcasted_iota(jnp.int32, (KM_ROWS, TQ), 0)
    vt_ref[...] = v_ref[...].astype(F32).T.astype(BF16)
    key_le_query = _key_le_query_mask()

    def scores(i):
        q = q_ref[i * TQ:(i + 1) * TQ, :]
        s_t = _nt_dot(k_ref[0:(i + 1) * TQ, :], q)
        g = _nt_dot(km_hl, q) if i > 0 else None
        return s_t, g

    def consume(i, sg):
        s_t, g = sg
        ext = (i + 1) * TQ
        parts = []
        if i > 0:
            past = row < i
            gate = jnp.where(past, g[:KM_ROWS] + g[KM_ROWS:], -jnp.inf)
            rank = jnp.zeros(gate.shape, F32)
            for j in range(i):
                gj = gate[j:j + 1, :]
                beats = (gj > gate) | ((gj == gate) & (j < row))
                rank = rank + beats.astype(F32)
            bias = jnp.where(past & (rank < MOBA_TOPK), 0.0, NEG)
            for n in range(i):
                for r in range(n * TQ, (n + 1) * TQ, PART):
                    parts.append(s_t[r:r + PART, :] + bias[n:n + 1, :])
        for r in range(0, TQ, PART):
            parts.append(jnp.where(key_le_query[r:r + PART, :], s_t[i * TQ + r:i * TQ + r + PART, :], NEG))
        o_ref[i * TQ:ext, :] = _softmax_pv_keymajor(parts, vt_ref[:, 0:ext]).astype(BF16)

    _pipelined(list(range(N_QT)), scores, consume)


def _attn_c(h_c):
    batch, n = h_c.shape[0], HEADS_PER_STEP
    return pl.pallas_call(
        _per_head(_attn_c_kernel, 0, n_scratch=1),
        out_shape=jax.ShapeDtypeStruct((batch, C_HEADS, SEQ, HEAD_DIM), BF16),
        grid=(batch, C_HEADS // n),
        in_specs=[_heads_spec(0, n), _heads_spec(C_HEADS, n), _heads_spec(2 * C_HEADS, n)],
        out_specs=_heads_spec(0, n),
        scratch_shapes=[pltpu.VMEM((HEAD_DIM, SEQ), BF16)],
        compiler_params=pltpu.CompilerParams(
            dimension_semantics=("parallel", "parallel"),
            vmem_limit_bytes=VMEM_LIMIT),
        name="moba_attn",
    )(h_c, h_c, h_c)


OUT_TM = 512
OUT_SUB_M, OUT_SUB_N = 256, 256
Z_BLOCK_OFF = 3


def _outproj_kernel(*refs, n_y, final_norm):
    y_refs = refs[:n_y]
    z_ref, x_ref, w_ref, g_ref, o_ref, ysc_ref, wb_ref = refs[n_y:]
    y_blocks = [(y_ref, hh) for y_ref in y_refs for hh in range(y_ref.shape[0])]

    @pl.when(pl.program_id(0) == 0)
    def _():
        wb_ref[...] = w_ref[...].astype(BF16)
    for r in range(OUT_TM // OUT_SUB_M):
        rows = slice(r * OUT_SUB_M, (r + 1) * OUT_SUB_M)
        for c, (y_ref, hh) in enumerate(y_blocks):
            z = z_ref[c, rows, :].astype(F32)
            gate = z / (1.0 + jnp.exp(-z))
            ysc_ref[rows, c * HEAD_DIM:(c + 1) * HEAD_DIM] = (
                y_ref[hh, rows, :].astype(F32) * gate).astype(BF16)
        ssq = jnp.zeros((OUT_SUB_M, 1), F32)
        for n in range(D_MODEL // OUT_SUB_N):
            cols = slice(n * OUT_SUB_N, (n + 1) * OUT_SUB_N)
            h = x_ref[rows, cols] + jnp.dot(ysc_ref[rows, :], wb_ref[:, cols],
                                            preferred_element_type=F32)
            o_ref[rows, cols] = h
            if final_norm:
                ssq = ssq + jnp.sum(h * h, axis=-1, keepdims=True)
        if final_norm:
            inv = lax.rsqrt(ssq * (1.0 / D_MODEL) + NORM_EPS)
            o_ref[rows, :] = o_ref[rows, :] * inv * g_ref[...]


def _outproj(ys, h_in, x2d, w, g_final, batch, final_norm):
    rows = x2d.shape[0]
    s_tiles = SEQ // OUT_TM
    n_z = D_MODEL // HEAD_DIM
    y_specs = [pl.BlockSpec((None, y.shape[1], OUT_TM, HEAD_DIM),
                            lambda i: (i // s_tiles, 0, i % s_tiles, 0)) for y in ys]
    assert sum(y.shape[1] for y in ys) == n_z
    return pl.pallas_call(
        functools.partial(_outproj_kernel, n_y=len(ys), final_norm=final_norm),
        out_shape=jax.ShapeDtypeStruct((rows, D_MODEL), F32),
        grid=(rows // OUT_TM,),
        in_specs=y_specs + [
            pl.BlockSpec((None, n_z, OUT_TM, HEAD_DIM),
                         lambda i: (i // s_tiles, Z_BLOCK_OFF, i % s_tiles, 0)),
            pl.BlockSpec((OUT_TM, D_MODEL), lambda i: (i, 0)),
            pl.BlockSpec((D_MODEL, D_MODEL), lambda i: (0, 0), pipeline_mode=pl.Buffered(1)),
            pl.BlockSpec((1, D_MODEL), lambda i: (0, 0)),
        ],
        out_specs=pl.BlockSpec((OUT_TM, D_MODEL), lambda i: (i, 0)),
        scratch_shapes=[pltpu.VMEM((OUT_TM, D_MODEL), BF16), pltpu.VMEM((D_MODEL, D_MODEL), BF16)],
        compiler_params=pltpu.CompilerParams(
            dimension_semantics=("arbitrary",),
            vmem_limit_bytes=VMEM_LIMIT),
        name="gate_outproj",
    )(*ys, h_in, x2d, w, g_final)


def _rope_tables():
    inv = 1.0 / (ROPE_THETA ** (jnp.arange(0, HEAD_DIM, 2, dtype=F32) / HEAD_DIM))
    ang = jnp.arange(SEQ, dtype=F32)[:, None] * inv[None, :]
    cos = jnp.concatenate([jnp.cos(ang), jnp.cos(ang)], axis=-1)
    sin = jnp.concatenate([-jnp.sin(ang), jnp.sin(ang)], axis=-1)
    return cos, sin


def _tile_kinds(group_kinds, cols_per_group):
    kinds = []
    for kind, cols in zip(group_kinds, cols_per_group):
        assert cols % IN_TN == 0
        kinds += [kind] * (cols // IN_TN)
    assert len(kinds) == IN_WIDTH // IN_TN
    return jnp.asarray(kinds, jnp.int32)


def kernel(x, norm_ab, w_in_ab, w_out_ab, lam_ab, subln_ab, norm_c, w_in_c, w_out_c, final_norm):
    batch, seq, d = x.shape
    assert (seq, d) == (SEQ, D_MODEL) and x.dtype == F32
    assert w_in_ab.shape == (1, D_MODEL, IN_WIDTH) and w_in_c.shape == (1, D_MODEL, IN_WIDTH)
    cos, sin = _rope_tables()
    aw, bw = A_HEADS * HEAD_DIM, B_HEADS * B_VDIM
    kinds_ab = _tile_kinds(
        [KIND_ROPE_SCALED, KIND_ROPE, KIND_PLAIN, KIND_ROPE_SCALED, KIND_ROPE, KIND_PLAIN, KIND_PLAIN],
        [aw, aw, aw, bw, bw, bw, aw + bw])
    cw = C_HEADS * HEAD_DIM
    kinds_c = _tile_kinds([KIND_ROPE_SCALED, KIND_ROPE, KIND_PLAIN, KIND_PLAIN], [cw, cw, cw, cw])

    x2d = x.reshape(batch * seq, d)

    g_final = final_norm[None, :]
    h_ab = _inproj(x2d, norm_ab[0][None, :], w_in_ab[0], cos, sin, kinds_ab, batch)
    ya = _attn_a(h_ab, _dilated_bias_table())
    lam_init = 0.8 - 0.6 * math.exp(-0.3 * 0)
    yb = _attn_b(h_ab, lam_ab[0], subln_ab[0][None, :], lam_init)
    x1 = _outproj([ya, yb], h_ab, x2d, w_out_ab[0], g_final, batch, final_norm=False)

    h_c = _inproj(x1, norm_c[0][None, :], w_in_c[0], cos, sin, kinds_c, batch)
    yc = _attn_c(h_c)
    out = _outproj([yc], h_c, x1, w_out_c[0], g_final, batch, final_norm=True)
    return out.reshape(batch, seq, d)
```

```python
import functools
import math

import numpy as np
import jax
import jax.numpy as jnp
from jax import lax
from jax.experimental import pallas as pl
from jax.experimental.pallas import tpu as pltpu

D_MODEL = 2048
SEQ = 2048
HEAD_DIM = 128
ROPE_THETA = 10000.0
NORM_EPS = 1e-6
A_HEADS = 8
DIL_CONFIGS = ((128, 1), (512, 4), (2048, 16))
B_HEADS = 4
B_VDIM = 2 * HEAD_DIM
C_HEADS = 16
MOBA_BLOCK = 256
MOBA_TOPK = 3
N_MOBA_BLOCKS = SEQ // MOBA_BLOCK
IN_WIDTH = 8192
N_COL_BLOCKS = IN_WIDTH // HEAD_DIM
SCALE = HEAD_DIM ** -0.5
Q_SCALE = SCALE * math.log2(math.e)
NEG = -1e30

BF16 = jnp.bfloat16
F32 = jnp.float32

KIND_PLAIN, KIND_ROPE, KIND_ROPE_SCALED = 0, 1, 2

VMEM_LIMIT = 56 * 1024 * 1024


def _nt_dot(a, b):
    return lax.dot_general(a, b, (((1,), (1,)), ((), ())), preferred_element_type=F32)


IN_TM, IN_TN = 2048, 1024
IN_SUB_M, IN_SUB_N = 256, 256


def _inproj_kernel(kinds_ref, x_ref, g_ref, w_ref, cos_ref, sin_ref, o_ref, xn_ref):
    j = pl.program_id(1)

    @pl.when(j == 0)
    def _():
        x = x_ref[...]
        ms = jnp.mean(x * x, axis=-1, keepdims=True)
        xn_ref[...] = (x * lax.rsqrt(ms + NORM_EPS) * g_ref[...]).astype(BF16)

    kind = kinds_ref[j]
    scale = jnp.where(kind == KIND_ROPE_SCALED, Q_SCALE, 1.0).astype(F32)
    for nh in range(IN_TN // IN_SUB_N):
        w = w_ref[:, nh * IN_SUB_N:(nh + 1) * IN_SUB_N].astype(BF16)
        for mh in range(IN_TM // IN_SUB_M):
            rows = slice(mh * IN_SUB_M, (mh + 1) * IN_SUB_M)
            acc = jnp.dot(xn_ref[rows, :], w, preferred_element_type=F32)
            cos = jnp.where(kind == KIND_PLAIN, 1.0, cos_ref[rows, :] * scale)
            sin = jnp.where(kind == KIND_PLAIN, 0.0, sin_ref[rows, :] * scale)
            for c in range(IN_SUB_N // HEAD_DIM):
                t = acc[:, c * HEAD_DIM:(c + 1) * HEAD_DIM]
                o_ref[nh * (IN_SUB_N // HEAD_DIM) + c, rows, :] = (
                    t * cos + pltpu.roll(t, HEAD_DIM // 2, 1) * sin).astype(BF16)


def _inproj(x2d, g, w, cos, sin_signed, kinds, batch):
    rows = x2d.shape[0]
    s_tiles = SEQ // IN_TM
    grid = (rows // IN_TM, IN_WIDTH // IN_TN)
    blk_per_tile = IN_TN // HEAD_DIM
    return pl.pallas_call(
        _inproj_kernel,
        out_shape=jax.ShapeDtypeStruct((batch, N_COL_BLOCKS, SEQ, HEAD_DIM), BF16),
        grid_spec=pltpu.PrefetchScalarGridSpec(
            num_scalar_prefetch=1,
            grid=grid,
            in_specs=[
                pl.BlockSpec((IN_TM, D_MODEL), lambda i, j, k: (i, 0), pipeline_mode=pl.Buffered(1)),
                pl.BlockSpec((1, D_MODEL), lambda i, j, k: (0, 0)),
                pl.BlockSpec((D_MODEL, IN_TN), lambda i, j, k: (0, j)),
                pl.BlockSpec((IN_TM, HEAD_DIM), lambda i, j, k: (i % s_tiles, 0)),
                pl.BlockSpec((IN_TM, HEAD_DIM), lambda i, j, k: (i % s_tiles, 0)),
            ],
            out_specs=pl.BlockSpec((None, blk_per_tile, IN_TM, HEAD_DIM),
                                   lambda i, j, k: (i // s_tiles, j, i % s_tiles, 0)),
            scratch_shapes=[pltpu.VMEM((IN_TM, D_MODEL), BF16)],
        ),
        compiler_params=pltpu.CompilerParams(
            dimension_semantics=("parallel", "arbitrary"),
            vmem_limit_bytes=VMEM_LIMIT),
        name="norm_inproj",
    )(kinds, x2d, g, w, cos, sin_signed)


TQ = 256
N_QT = SEQ // TQ
PART = 256
A_PIPE_DEPTH, B_PIPE_DEPTH, C_PIPE_DEPTH = 5, 1, 3


def _softmax_pv_keymajor(parts, vt):
    m = functools.reduce(jnp.maximum, [jnp.max(s, axis=0, keepdims=True) for s in parts])
    ps = [jnp.exp2(s - m) for s in parts]
    inv_l = 1.0 / sum(jnp.sum(p, axis=0, keepdims=True) for p in ps)
    p_t = jnp.concatenate([p.astype(BF16) for p in ps], axis=0)
    o_t = jnp.dot(vt, p_t, preferred_element_type=F32)
    return (o_t * inv_l).T


def _softmax_pv_querymajor(s, v):
    m = jnp.max(s, axis=-1, keepdims=True)
    p = jnp.exp2(s - m)
    inv_l = 1.0 / jnp.sum(p, axis=-1, keepdims=True)
    return jnp.dot(p.astype(BF16), v, preferred_element_type=F32) * inv_l


def _pipelined(units, scores, consume, depth):
    ready = [scores(u) for u in units[:depth]]
    for idx, u in enumerate(units):
        if idx + depth < len(units):
            ready.append(scores(units[idx + depth]))
        consume(u, ready.pop(0))


def _key_le_query_mask(key_axis=0):
    key = lax.broadcasted_iota(jnp.int32, (TQ, TQ), key_axis)
    query = lax.broadcasted_iota(jnp.int32, (TQ, TQ), 1 - key_axis)
    return key <= query


def _dilated_bias_table():
    key = np.arange(SEQ)[:, None]
    query = (SEQ - TQ) + np.arange(TQ)[None, :]
    d = query - key
    mult = np.zeros(d.shape, np.float64)
    for window, dil in DIL_CONFIGS:
        mult += (d >= 0) & (d % dil == 0) & (d <= window)
    return jnp.asarray(np.where(mult > 0, np.log2(np.maximum(mult, 1.0)), NEG), F32)


def _per_head(head_kernel, n_shared, n_scratch=0, slabs_per_head=1):
    def group_kernel(*refs):
        shared = refs[:n_shared]
        grouped = refs[n_shared:len(refs) - n_scratch]
        scratch = refs[len(refs) - n_scratch:]

        def view(r, g):
            return r.at[g] if slabs_per_head == 1 else r.at[pl.ds(g * slabs_per_head, slabs_per_head)]

        def body(g, carry):
            head_kernel(*shared, *[view(r, g) for r in grouped], *scratch)
            return carry

        lax.fori_loop(0, grouped[0].shape[0] // slabs_per_head, body, 0)
    return group_kernel


def _heads_spec(off, n):
    assert off % n == 0
    return pl.BlockSpec((None, n, SEQ, HEAD_DIM), lambda b, h: (b, off // n + h, 0, 0))


HEADS_PER_STEP = 4


def _attn_a_kernel(bias_ref, q_ref, k_ref, v_ref, o_ref, vt_ref):
    vt_ref[...] = v_ref[...].astype(F32).T.astype(BF16)

    def scores(i):
        return _nt_dot(k_ref[0:(i + 1) * TQ, :], q_ref[i * TQ:(i + 1) * TQ, :])

    def consume(i, s_t):
        ext = (i + 1) * TQ
        off = SEQ - ext
        parts = [s_t[r:r + PART, :] + bias_ref[off + r:off + r + PART, :] for r in range(0, ext, PART)]
        o = _softmax_pv_keymajor(parts, vt_ref[:, 0:ext])
        o_ref[i * TQ:ext, :] = o.astype(BF16)

    _pipelined(list(range(N_QT)), scores, consume, depth=A_PIPE_DEPTH)


def _attn_a(h_ab, bias):
    batch, n = h_ab.shape[0], HEADS_PER_STEP
    return pl.pallas_call(
        _per_head(_attn_a_kernel, 1, n_scratch=1),
        out_shape=jax.ShapeDtypeStruct((batch, A_HEADS, SEQ, HEAD_DIM), BF16),
        grid=(batch, A_HEADS // n),
        in_specs=[pl.BlockSpec((SEQ, TQ), lambda b, h: (0, 0)),
                  _heads_spec(0, n), _heads_spec(A_HEADS, n), _heads_spec(2 * A_HEADS, n)],
        out_specs=_heads_spec(0, n),
        scratch_shapes=[pltpu.VMEM((HEAD_DIM, SEQ), BF16)],
        compiler_params=pltpu.CompilerParams(
            dimension_semantics=("parallel", "parallel"),
            vmem_limit_bytes=VMEM_LIMIT),
        name="dilated_attn",
    )(bias, h_ab, h_ab, h_ab)


B_Q_OFF, B_K_OFF, B_V_OFF = 24, 32, 40


def _attn_b_kernel(lam_ref, g_ref, q_ref, k_ref, v_ref, o_ref, vcat_ref, *, lam_init):
    vcat_ref[:, :HEAD_DIM] = v_ref[0]
    vcat_ref[:, HEAD_DIM:] = v_ref[1]
    lam_v = lam_ref[...]
    lam = (jnp.exp(jnp.sum(lam_v[0:1] * lam_v[1:2], axis=-1, keepdims=True))
           - jnp.exp(jnp.sum(lam_v[2:3] * lam_v[3:4], axis=-1, keepdims=True)) + lam_init)
    query_ge_key = _key_le_query_mask(key_axis=1)
    first = {}

    def scores(u):
        i, t = u
        return _nt_dot(q_ref[t, i * TQ:(i + 1) * TQ, :], k_ref[t, 0:(i + 1) * TQ, :])

    def consume(u, s):
        i, t = u
        ext = (i + 1) * TQ
        rows = slice(i * TQ, ext)
        own = jnp.where(query_ge_key, s[:, i * TQ:], NEG)
        s = own if i == 0 else jnp.concatenate([s[:, :i * TQ], own], axis=1)
        o = _softmax_pv_querymajor(s, vcat_ref[0:ext, :])
        if t == 0:
            first[i] = o
            return
        o = first.pop(i) - lam * o
        ms = jnp.mean(o * o, axis=-1, keepdims=True)
        y = o * lax.rsqrt(ms + NORM_EPS) * g_ref[...] * (1.0 - lam_init)
        o_ref[0, rows, :] = y[:, :HEAD_DIM].astype(BF16)
        o_ref[1, rows, :] = y[:, HEAD_DIM:].astype(BF16)

    _pipelined([(i, t) for i in range(N_QT) for t in range(2)], scores, consume, depth=B_PIPE_DEPTH)


def _attn_b(h_ab, lam_vec, subln_g, lam_init):
    batch = h_ab.shape[0]
    heads = HEADS_PER_STEP // 2
    spec = lambda off: _heads_spec(off, 2 * heads)
    return pl.pallas_call(
        _per_head(functools.partial(_attn_b_kernel, lam_init=lam_init), 2, n_scratch=1,
                  slabs_per_head=2),
        out_shape=jax.ShapeDtypeStruct((batch, 2 * B_HEADS, SEQ, HEAD_DIM), BF16),
        grid=(batch, B_HEADS // heads),
        in_specs=[
            pl.BlockSpec((4, HEAD_DIM), lambda b, h: (0, 0)),
            pl.BlockSpec((1, B_VDIM), lambda b, h: (0, 0)),
            spec(B_Q_OFF), spec(B_K_OFF), spec(B_V_OFF),
        ],
        out_specs=spec(0),
        scratch_shapes=[pltpu.VMEM((SEQ, B_VDIM), BF16)],
        compiler_params=pltpu.CompilerParams(
            dimension_semantics=("parallel", "parallel"),
            vmem_limit_bytes=VMEM_LIMIT),
        name="diff_attn",
    )(lam_vec, subln_g, h_ab, h_ab, h_ab)


KM_ROWS = 16


def _attn_c_kernel(q_ref, k_ref, v_ref, o_ref, vt_ref):
    assert TQ == MOBA_BLOCK
    key_blk = lax.broadcasted_iota(jnp.int32, (KM_ROWS, SEQ), 1) // MOBA_BLOCK
    pool = (lax.broadcasted_iota(jnp.int32, (KM_ROWS, SEQ), 0) == key_blk).astype(BF16)
    km = jnp.dot(pool, k_ref[...], preferred_element_type=F32) * (1.0 / MOBA_BLOCK)
    km_hi = km.astype(BF16)
    km_lo = (km - km_hi.astype(F32)).astype(BF16)
    km_hl = jnp.concatenate([km_hi, km_lo], axis=0)
    row = lax.broadcasted_iota(jnp.int32, (KM_ROWS, TQ), 0)
    vt_ref[...] = v_ref[...].astype(F32).T.astype(BF16)
    key_le_query = _key_le_query_mask()

    def scores(i):
        q = q_ref[i * TQ:(i + 1) * TQ, :]
        s_t = _nt_dot(k_ref[0:(i + 1) * TQ, :], q)
        g = _nt_dot(km_hl, q) if i > 0 else None
        return s_t, g

    def consume(i, sg):
        s_t, g = sg
        ext = (i + 1) * TQ
        parts = []
        if i > 0:
            past = row < i
            gate = jnp.where(past, g[:KM_ROWS] + g[KM_ROWS:], -jnp.inf)
            rank = jnp.zeros(gate.shape, F32)
            for j in range(i):
                gj = gate[j:j + 1, :]
                beats = (gj > gate) | ((gj == gate) & (j < row))
                rank = rank + beats.astype(F32)
            bias = jnp.where(past & (rank < MOBA_TOPK), 0.0, NEG)
            for n in range(i):
                for r in range(n * TQ, (n + 1) * TQ, PART):
                    parts.append(s_t[r:r + PART, :] + bias[n:n + 1, :])
        for r in range(0, TQ, PART):
            parts.append(jnp.where(key_le_query[r:r + PART, :], s_t[i * TQ + r:i * TQ + r + PART, :], NEG))
        o_ref[i * TQ:ext, :] = _softmax_pv_keymajor(parts, vt_ref[:, 0:ext]).astype(BF16)

    _pipelined(list(range(N_QT)), scores, consume, depth=C_PIPE_DEPTH)


def _attn_c(h_c):
    batch, n = h_c.shape[0], HEADS_PER_STEP
    return pl.pallas_call(
        _per_head(_attn_c_kernel, 0, n_scratch=1),
        out_shape=jax.ShapeDtypeStruct((batch, C_HEADS, SEQ, HEAD_DIM), BF16),
        grid=(batch, C_HEADS // n),
        in_specs=[_heads_spec(0, n), _heads_spec(C_HEADS, n), _heads_spec(2 * C_HEADS, n)],
        out_specs=_heads_spec(0, n),
        scratch_shapes=[pltpu.VMEM((HEAD_DIM, SEQ), BF16)],
        compiler_params=pltpu.CompilerParams(
            dimension_semantics=("parallel", "parallel"),
            vmem_limit_bytes=VMEM_LIMIT),
        name="moba_attn",
    )(h_c, h_c, h_c)


OUT_TM = 512
OUT_SUB_M, OUT_SUB_N = 256, 256
Z_BLOCK_OFF = 3


def _outproj_kernel(*refs, n_y, final_norm):
    y_refs = refs[:n_y]
    z_ref, x_ref, w_ref, g_ref, o_ref, ysc_ref, wb_ref = refs[n_y:]
    y_blocks = [(y_ref, hh) for y_ref in y_refs for hh in range(y_ref.shape[0])]

    @pl.when(pl.program_id(0) == 0)
    def _():
        wb_ref[...] = w_ref[...].astype(BF16)
    for r in range(OUT_TM // OUT_SUB_M):
        rows = slice(r * OUT_SUB_M, (r + 1) * OUT_SUB_M)
        for c, (y_ref, hh) in enumerate(y_blocks):
            z = z_ref[c, rows, :].astype(F32)
            gate = z / (1.0 + jnp.exp(-z))
            ysc_ref[rows, c * HEAD_DIM:(c + 1) * HEAD_DIM] = (
                y_ref[hh, rows, :].astype(F32) * gate).astype(BF16)
        ssq = jnp.zeros((OUT_SUB_M, 1), F32)
        for n in range(D_MODEL // OUT_SUB_N):
            cols = slice(n * OUT_SUB_N, (n + 1) * OUT_SUB_N)
            h = x_ref[rows, cols] + jnp.dot(ysc_ref[rows, :], wb_ref[:, cols],
                                            preferred_element_type=F32)
            o_ref[rows, cols] = h
            if final_norm:
                ssq = ssq + jnp.sum(h * h, axis=-1, keepdims=True)
        if final_norm:
            inv = lax.rsqrt(ssq * (1.0 / D_MODEL) + NORM_EPS)
            o_ref[rows, :] = o_ref[rows, :] * inv * g_ref[...]


def _outproj(ys, h_in, x2d, w, g_final, batch, final_norm):
    rows = x2d.shape[0]
    s_tiles = SEQ // OUT_TM
    n_z = D_MODEL // HEAD_DIM
    y_specs = [pl.BlockSpec((None, y.shape[1], OUT_TM, HEAD_DIM),
                            lambda i: (i // s_tiles, 0, i % s_tiles, 0)) for y in ys]
    assert sum(y.shape[1] for y in ys) == n_z
    return pl.pallas_call(
        functools.partial(_outproj_kernel, n_y=len(ys), final_norm=final_norm),
        out_shape=jax.ShapeDtypeStruct((rows, D_MODEL), F32),
        grid=(rows // OUT_TM,),
        in_specs=y_specs + [
            pl.BlockSpec((None, n_z, OUT_TM, HEAD_DIM),
                         lambda i: (i // s_tiles, Z_BLOCK_OFF, i % s_tiles, 0)),
            pl.BlockSpec((OUT_TM, D_MODEL), lambda i: (i, 0)),
            pl.BlockSpec((D_MODEL, D_MODEL), lambda i: (0, 0), pipeline_mode=pl.Buffered(1)),
            pl.BlockSpec((1, D_MODEL), lambda i: (0, 0)),
        ],
        out_specs=pl.BlockSpec((OUT_TM, D_MODEL), lambda i: (i, 0)),
        scratch_shapes=[pltpu.VMEM((OUT_TM, D_MODEL), BF16), pltpu.VMEM((D_MODEL, D_MODEL), BF16)],
        compiler_params=pltpu.CompilerParams(
            dimension_semantics=("arbitrary",),
            vmem_limit_bytes=VMEM_LIMIT),
        name="gate_outproj",
    )(*ys, h_in, x2d, w, g_final)


def _rope_tables():
    inv = 1.0 / (ROPE_THETA ** (jnp.arange(0, HEAD_DIM, 2, dtype=F32) / HEAD_DIM))
    ang = jnp.arange(SEQ, dtype=F32)[:, None] * inv[None, :]
    cos = jnp.concatenate([jnp.cos(ang), jnp.cos(ang)], axis=-1)
    sin = jnp.concatenate([-jnp.sin(ang), jnp.sin(ang)], axis=-1)
    return cos, sin


def _tile_kinds(group_kinds, cols_per_group):
    kinds = []
    for kind, cols in zip(group_kinds, cols_per_group):
        assert cols % IN_TN == 0
        kinds += [kind] * (cols // IN_TN)
    assert len(kinds) == IN_WIDTH // IN_TN
    return jnp.asarray(kinds, jnp.int32)


def kernel(x, norm_ab, w_in_ab, w_out_ab, lam_ab, subln_ab, norm_c, w_in_c, w_out_c, final_norm):
    batch, seq, d = x.shape
    assert (seq, d) == (SEQ, D_MODEL) and x.dtype == F32
    assert w_in_ab.shape == (1, D_MODEL, IN_WIDTH) and w_in_c.shape == (1, D_MODEL, IN_WIDTH)
    cos, sin = _rope_tables()
    aw, bw = A_HEADS * HEAD_DIM, B_HEADS * B_VDIM
    kinds_ab = _tile_kinds(
        [KIND_ROPE_SCALED, KIND_ROPE, KIND_PLAIN, KIND_ROPE_SCALED, KIND_ROPE, KIND_PLAIN, KIND_PLAIN],
        [aw, aw, aw, bw, bw, bw, aw + bw])
    cw = C_HEADS * HEAD_DIM
    kinds_c = _tile_kinds([KIND_ROPE_SCALED, KIND_ROPE, KIND_PLAIN, KIND_PLAIN], [cw, cw, cw, cw])

    x2d = x.reshape(batch * seq, d)

    g_final = final_norm[None, :]
    h_ab = _inproj(x2d, norm_ab[0][None, :], w_in_ab[0], cos, sin, kinds_ab, batch)
    ya = _attn_a(h_ab, _dilated_bias_table())
    lam_init = 0.8 - 0.6 * math.exp(-0.3 * 0)
    yb = _attn_b(h_ab, lam_ab[0], subln_ab[0][None, :], lam_init)
    x1 = _outproj([ya, yb], h_ab, x2d, w_out_ab[0], g_final, batch, final_norm=False)

    h_c = _inproj(x1, norm_c[0][None, :], w_in_c[0], cos, sin, kinds_c, batch)
    yc = _attn_c(h_c)
    out = _outproj([yc], h_c, x1, w_out_c[0], g_final, batch, final_norm=True)
    return out.reshape(batch, seq, d)
```

```python
import functools
import math

import numpy as np
import jax
import jax.numpy as jnp
from jax import lax
from jax.experimental import pallas as pl
from jax.experimental.pallas import tpu as pltpu

D_MODEL = 2048
SEQ = 2048
HEAD_DIM = 128
ROPE_THETA = 10000.0
NORM_EPS = 1e-6
A_HEADS = 8
DIL_CONFIGS = ((128, 1), (512, 4), (2048, 16))
B_HEADS = 4
B_VDIM = 2 * HEAD_DIM
C_HEADS = 16
MOBA_BLOCK = 256
MOBA_TOPK = 3
N_MOBA_BLOCKS = SEQ // MOBA_BLOCK
IN_WIDTH = 8192
N_COL_BLOCKS = IN_WIDTH // HEAD_DIM
SCALE = HEAD_DIM ** -0.5
Q_SCALE = SCALE * math.log2(math.e)
NEG = -1e30

BF16 = jnp.bfloat16
F32 = jnp.float32

KIND_PLAIN, KIND_ROPE, KIND_ROPE_SCALED = 0, 1, 2

VMEM_LIMIT = 56 * 1024 * 1024


def _nt_dot(a, b):
    return lax.dot_general(a, b, (((1,), (1,)), ((), ())), preferred_element_type=F32)


IN_TM, IN_TN = 2048, 1024
IN_SUB_M, IN_SUB_N = 256, 256


def _inproj_kernel(kinds_ref, x_ref, g_ref, w_ref, cos_ref, sin_ref, o_ref, xn_ref):
    j = pl.program_id(1)

    @pl.when(j == 0)
    def _():
        x = x_ref[...]
        ms = jnp.mean(x * x, axis=-1, keepdims=True)
        xn_ref[...] = (x * lax.rsqrt(ms + NORM_EPS) * g_ref[...]).astype(BF16)

    kind = kinds_ref[j]
    scale = jnp.where(kind == KIND_ROPE_SCALED, Q_SCALE, 1.0).astype(F32)
    for nh in range(IN_TN // IN_SUB_N):
        w = w_ref[:, nh * IN_SUB_N:(nh + 1) * IN_SUB_N].astype(BF16)
        for mh in range(IN_TM // IN_SUB_M):
            rows = slice(mh * IN_SUB_M, (mh + 1) * IN_SUB_M)
            acc = jnp.dot(xn_ref[rows, :], w, preferred_element_type=F32)
            cos = jnp.where(kind == KIND_PLAIN, 1.0, cos_ref[rows, :] * scale)
            sin = jnp.where(kind == KIND_PLAIN, 0.0, sin_ref[rows, :] * scale)
            for c in range(IN_SUB_N // HEAD_DIM):
                t = acc[:, c * HEAD_DIM:(c + 1) * HEAD_DIM]
                o_ref[nh * (IN_SUB_N // HEAD_DIM) + c, rows, :] = (
                    t * cos + pltpu.roll(t, HEAD_DIM // 2, 1) * sin).astype(BF16)


def _inproj(x2d, g, w, cos, sin_signed, kinds, batch):
    rows = x2d.shape[0]
    s_tiles = SEQ // IN_TM
    grid = (rows // IN_TM, IN_WIDTH // IN_TN)
    blk_per_tile = IN_TN // HEAD_DIM
    return pl.pallas_call(
        _inproj_kernel,
        out_shape=jax.ShapeDtypeStruct((batch, N_COL_BLOCKS, SEQ, HEAD_DIM), BF16),
        grid_spec=pltpu.PrefetchScalarGridSpec(
            num_scalar_prefetch=1,
            grid=grid,
            in_specs=[
                pl.BlockSpec((IN_TM, D_MODEL), lambda i, j, k: (i, 0), pipeline_mode=pl.Buffered(1)),
                pl.BlockSpec((1, D_MODEL), lambda i, j, k: (0, 0)),
                pl.BlockSpec((D_MODEL, IN_TN), lambda i, j, k: (0, j)),
                pl.BlockSpec((IN_TM, HEAD_DIM), lambda i, j, k: (i % s_tiles, 0)),
                pl.BlockSpec((IN_TM, HEAD_DIM), lambda i, j, k: (i % s_tiles, 0)),
            ],
            out_specs=pl.BlockSpec((None, blk_per_tile, IN_TM, HEAD_DIM),
                                   lambda i, j, k: (i // s_tiles, j, i % s_tiles, 0)),
            scratch_shapes=[pltpu.VMEM((IN_TM, D_MODEL), BF16)],
        ),
        compiler_params=pltpu.CompilerParams(
            dimension_semantics=("parallel", "arbitrary"),
            vmem_limit_bytes=VMEM_LIMIT),
        name="norm_inproj",
    )(kinds, x2d, g, w, cos, sin_signed)


TQ = 256
N_QT = SEQ // TQ
PART = 256
A_PIPE_DEPTH, B_PIPE_DEPTH, C_PIPE_DEPTH = 5, 1, 3


ONES_ROWS = 16


def _store_vt(vt_ref, v_ref):
    dv = v_ref.shape[1]
    vt_ref[0:dv, :] = v_ref[...].astype(F32).T.astype(BF16)
    vt_ref[dv:, :] = jnp.ones((ONES_ROWS, v_ref.shape[0]), BF16)


def _softmax_pv_keymajor(parts, vt):
    m = functools.reduce(jnp.maximum, [jnp.max(s, axis=0, keepdims=True) for s in parts])
    p_t = jnp.concatenate([jnp.exp2(s - m).astype(BF16) for s in parts], axis=0)
    o_t = jnp.dot(vt, p_t, preferred_element_type=F32)
    dv = vt.shape[0] - ONES_ROWS
    return (o_t[:dv] * (1.0 / o_t[dv:dv + 1])).T


def _softmax_pv_querymajor(s, v):
    m = jnp.max(s, axis=-1, keepdims=True)
    p = jnp.exp2(s - m)
    inv_l = 1.0 / jnp.sum(p, axis=-1, keepdims=True)
    return jnp.dot(p.astype(BF16), v, preferred_element_type=F32) * inv_l


def _pipelined(units, scores, consume, depth):
    ready = [scores(u) for u in units[:depth]]
    for idx, u in enumerate(units):
        if idx + depth < len(units):
            ready.append(scores(units[idx + depth]))
        consume(u, ready.pop(0))


def _key_le_query_mask(key_axis=0):
    key = lax.broadcasted_iota(jnp.int32, (TQ, TQ), key_axis)
    query = lax.broadcasted_iota(jnp.int32, (TQ, TQ), 1 - key_axis)
    return key <= query


def _dilated_bias_table():
    key = np.arange(SEQ)[:, None]
    query = (SEQ - TQ) + np.arange(TQ)[None, :]
    d = query - key
    mult = np.zeros(d.shape, np.float64)
    for window, dil in DIL_CONFIGS:
        mult += (d >= 0) & (d % dil == 0) & (d <= window)
    return jnp.asarray(np.where(mult > 0, np.log2(np.maximum(mult, 1.0)), NEG), F32)


def _per_head(head_kernel, n_shared, n_scratch=0, slabs_per_head=1):
    def group_kernel(*refs):
        shared = refs[:n_shared]
        grouped = refs[n_shared:len(refs) - n_scratch]
        scratch = refs[len(refs) - n_scratch:]

        def view(r, g):
            return r.at[g] if slabs_per_head == 1 else r.at[pl.ds(g * slabs_per_head, slabs_per_head)]

        def body(g, carry):
            head_kernel(*shared, *[view(r, g) for r in grouped], *scratch)
            return carry

        lax.fori_loop(0, grouped[0].shape[0] // slabs_per_head, body, 0)
    return group_kernel


def _heads_spec(off, n):
    assert off % n == 0
    return pl.BlockSpec((None, n, SEQ, HEAD_DIM), lambda b, h: (b, off // n + h, 0, 0))


HEADS_PER_STEP = 4


def _attn_a_kernel(bias_ref, q_ref, k_ref, v_ref, o_ref, vt_ref):
    _store_vt(vt_ref, v_ref)

    def scores(i):
        return _nt_dot(k_ref[0:(i + 1) * TQ, :], q_ref[i * TQ:(i + 1) * TQ, :])

    def consume(i, s_t):
        ext = (i + 1) * TQ
        off = SEQ - ext
        parts = [s_t[r:r + PART, :] + bias_ref[off + r:off + r + PART, :] for r in range(0, ext, PART)]
        o = _softmax_pv_keymajor(parts, vt_ref[:, 0:ext])
        o_ref[i * TQ:ext, :] = o.astype(BF16)

    _pipelined(list(range(N_QT)), scores, consume, depth=A_PIPE_DEPTH)


def _attn_a(h_ab, bias):
    batch, n = h_ab.shape[0], HEADS_PER_STEP
    return pl.pallas_call(
        _per_head(_attn_a_kernel, 1, n_scratch=1),
        out_shape=jax.ShapeDtypeStruct((batch, A_HEADS, SEQ, HEAD_DIM), BF16),
        grid=(batch, A_HEADS // n),
        in_specs=[pl.BlockSpec((SEQ, TQ), lambda b, h: (0, 0)),
                  _heads_spec(0, n), _heads_spec(A_HEADS, n), _heads_spec(2 * A_HEADS, n)],
        out_specs=_heads_spec(0, n),
        scratch_shapes=[pltpu.VMEM((HEAD_DIM + ONES_ROWS, SEQ), BF16)],
        compiler_params=pltpu.CompilerParams(
            dimension_semantics=("parallel", "parallel"),
            vmem_limit_bytes=VMEM_LIMIT),
        name="dilated_attn",
    )(bias, h_ab, h_ab, h_ab)


B_Q_OFF, B_K_OFF, B_V_OFF = 24, 32, 40


def _attn_b_kernel(lam_ref, g_ref, q_ref, k_ref, v_ref, o_ref, vcat_ref, *, lam_init):
    vcat_ref[:, :HEAD_DIM] = v_ref[0]
    vcat_ref[:, HEAD_DIM:] = v_ref[1]
    lam_v = lam_ref[...]
    lam = (jnp.exp(jnp.sum(lam_v[0:1] * lam_v[1:2], axis=-1, keepdims=True))
           - jnp.exp(jnp.sum(lam_v[2:3] * lam_v[3:4], axis=-1, keepdims=True)) + lam_init)
    query_ge_key = _key_le_query_mask(key_axis=1)
    first = {}

    def scores(u):
        i, t = u
        return _nt_dot(q_ref[t, i * TQ:(i + 1) * TQ, :], k_ref[t, 0:(i + 1) * TQ, :])

    def consume(u, s):
        i, t = u
        ext = (i + 1) * TQ
        rows = slice(i * TQ, ext)
        own = jnp.where(query_ge_key, s[:, i * TQ:], NEG)
        s = own if i == 0 else jnp.concatenate([s[:, :i * TQ], own], axis=1)
        o = _softmax_pv_querymajor(s, vcat_ref[0:ext, :])
        if t == 0:
            first[i] = o
            return
        o = first.pop(i) - lam * o
        ms = jnp.mean(o * o, axis=-1, keepdims=True)
        y = o * lax.rsqrt(ms + NORM_EPS) * g_ref[...] * (1.0 - lam_init)
        o_ref[0, rows, :] = y[:, :HEAD_DIM].astype(BF16)
        o_ref[1, rows, :] = y[:, HEAD_DIM:].astype(BF16)

    _pipelined([(i, t) for i in range(N_QT) for t in range(2)], scores, consume, depth=B_PIPE_DEPTH)


def _attn_b(h_ab, lam_vec, subln_g, lam_init):
    batch = h_ab.shape[0]
    heads = HEADS_PER_STEP // 2
    spec = lambda off: _heads_spec(off, 2 * heads)
    return pl.pallas_call(
        _per_head(functools.partial(_attn_b_kernel, lam_init=lam_init), 2, n_scratch=1,
                  slabs_per_head=2),
        out_shape=jax.ShapeDtypeStruct((batch, 2 * B_HEADS, SEQ, HEAD_DIM), BF16),
        grid=(batch, B_HEADS // heads),
        in_specs=[
            pl.BlockSpec((4, HEAD_DIM), lambda b, h: (0, 0)),
            pl.BlockSpec((1, B_VDIM), lambda b, h: (0, 0)),
            spec(B_Q_OFF), spec(B_K_OFF), spec(B_V_OFF),
        ],
        out_specs=spec(0),
        scratch_shapes=[pltpu.VMEM((SEQ, B_VDIM), BF16)],
        compiler_params=pltpu.CompilerParams(
            dimension_semantics=("parallel", "parallel"),
            vmem_limit_bytes=VMEM_LIMIT),
        name="diff_attn",
    )(lam_vec, subln_g, h_ab, h_ab, h_ab)


KM_ROWS = 16


def _attn_c_kernel(q_ref, k_ref, v_ref, o_ref, vt_ref):
    assert TQ == MOBA_BLOCK
    key_blk = lax.broadcasted_iota(jnp.int32, (KM_ROWS, SEQ), 1) // MOBA_BLOCK
    pool = (lax.broadcasted_iota(jnp.int32, (KM_ROWS, SEQ), 0) == key_blk).astype(BF16)
    km = jnp.dot(pool, k_ref[...], preferred_element_type=F32) * (1.0 / MOBA_BLOCK)
    km_hi = km.astype(BF16)
    km_lo = (km - km_hi.astype(F32)).astype(BF16)
    km_hl = jnp.concatenate([km_hi, km_lo], axis=0)
    row = lax.broadcasted_iota(jnp.int32, (KM_ROWS, TQ), 0)
    _store_vt(vt_ref, v_ref)
    key_le_query = _key_le_query_mask()

    def scores(i):
        q = q_ref[i * TQ:(i + 1) * TQ, :]
        s_t = _nt_dot(k_ref[0:(i + 1) * TQ, :], q)
        g = _nt_dot(km_hl, q) if i > 0 else None
        return s_t, g

    def consume(i, sg):
        s_t, g = sg
        ext = (i + 1) * TQ
        parts = []
        if i > 0:
            past = row < i
            gate = jnp.where(past, g[:KM_ROWS] + g[KM_ROWS:], -jnp.inf)
            rank = jnp.zeros(gate.shape, F32)
            for j in range(i):
                gj = gate[j:j + 1, :]
                beats = (gj > gate) | ((gj == gate) & (j < row))
                rank = rank + beats.astype(F32)
            bias = jnp.where(past & (rank < MOBA_TOPK), 0.0, NEG)
            for n in range(i):
                for r in range(n * TQ, (n + 1) * TQ, PART):
                    parts.append(s_t[r:r + PART, :] + bias[n:n + 1, :])
        for r in range(0, TQ, PART):
            parts.append(jnp.where(key_le_query[r:r + PART, :], s_t[i * TQ + r:i * TQ + r + PART, :], NEG))
        o_ref[i * TQ:ext, :] = _softmax_pv_keymajor(parts, vt_ref[:, 0:ext]).astype(BF16)

    _pipelined(list(range(N_QT)), scores, consume, depth=C_PIPE_DEPTH)


def _attn_c(h_c):
    batch, n = h_c.shape[0], HEADS_PER_STEP
    return pl.pallas_call(
        _per_head(_attn_c_kernel, 0, n_scratch=1),
        out_shape=jax.ShapeDtypeStruct((batch, C_HEADS, SEQ, HEAD_DIM), BF16),
        grid=(batch, C_HEADS // n),
        in_specs=[_heads_spec(0, n), _heads_spec(C_HEADS, n), _heads_spec(2 * C_HEADS, n)],
        out_specs=_heads_spec(0, n),
        scratch_shapes=[pltpu.VMEM((HEAD_DIM + ONES_ROWS, SEQ), BF16)],
        compiler_params=pltpu.CompilerParams(
            dimension_semantics=("parallel", "parallel"),
            vmem_limit_bytes=VMEM_LIMIT),
        name="moba_attn",
    )(h_c, h_c, h_c)


OUT_TM = 512
OUT_SUB_M, OUT_SUB_N = 256, 256
Z_BLOCK_OFF = 3


def _outproj_kernel(*refs, n_y, final_norm):
    y_refs = refs[:n_y]
    z_ref, x_ref, w_ref, g_ref, o_ref, ysc_ref, wb_ref = refs[n_y:]
    y_blocks = [(y_ref, hh) for y_ref in y_refs for hh in range(y_ref.shape[0])]

    @pl.when(pl.program_id(0) == 0)
    def _():
        wb_ref[...] = w_ref[...].astype(BF16)
    for r in range(OUT_TM // OUT_SUB_M):
        rows = slice(r * OUT_SUB_M, (r + 1) * OUT_SUB_M)
        for c, (y_ref, hh) in enumerate(y_blocks):
            z = z_ref[c, rows, :].astype(F32)
            gate = z / (1.0 + jnp.exp(-z))
            ysc_ref[rows, c * HEAD_DIM:(c + 1) * HEAD_DIM] = (
                y_ref[hh, rows, :].astype(F32) * gate).astype(BF16)
        ssq = jnp.zeros((OUT_SUB_M, 1), F32)
        for n in range(D_MODEL // OUT_SUB_N):
            cols = slice(n * OUT_SUB_N, (n + 1) * OUT_SUB_N)
            h = x_ref[rows, cols] + jnp.dot(ysc_ref[rows, :], wb_ref[:, cols],
                                            preferred_element_type=F32)
            o_ref[rows, cols] = h
            if final_norm:
                ssq = ssq + jnp.sum(h * h, axis=-1, keepdims=True)
        if final_norm:
            inv = lax.rsqrt(ssq * (1.0 / D_MODEL) + NORM_EPS)
            o_ref[rows, :] = o_ref[rows, :] * inv * g_ref[...]


def _outproj(ys, h_in, x2d, w, g_final, batch, final_norm):
    rows = x2d.shape[0]
    s_tiles = SEQ // OUT_TM
    n_z = D_MODEL // HEAD_DIM
    y_specs = [pl.BlockSpec((None, y.shape[1], OUT_TM, HEAD_DIM),
                            lambda i: (i // s_tiles, 0, i % s_tiles, 0)) for y in ys]
    assert sum(y.shape[1] for y in ys) == n_z
    return pl.pallas_call(
        functools.partial(_outproj_kernel, n_y=len(ys), final_norm=final_norm),
        out_shape=jax.ShapeDtypeStruct((rows, D_MODEL), F32),
        grid=(rows // OUT_TM,),
        in_specs=y_specs + [
            pl.BlockSpec((None, n_z, OUT_TM, HEAD_DIM),
                         lambda i: (i // s_tiles, Z_BLOCK_OFF, i % s_tiles, 0)),
            pl.BlockSpec((OUT_TM, D_MODEL), lambda i: (i, 0)),
            pl.BlockSpec((D_MODEL, D_MODEL), lambda i: (0, 0), pipeline_mode=pl.Buffered(1)),
            pl.BlockSpec((1, D_MODEL), lambda i: (0, 0)),
        ],
        out_specs=pl.BlockSpec((OUT_TM, D_MODEL), lambda i: (i, 0)),
        scratch_shapes=[pltpu.VMEM((OUT_TM, D_MODEL), BF16), pltpu.VMEM((D_MODEL, D_MODEL), BF16)],
        compiler_params=pltpu.CompilerParams(
            dimension_semantics=("arbitrary",),
            vmem_limit_bytes=VMEM_LIMIT),
        name="gate_outproj",
    )(*ys, h_in, x2d, w, g_final)


def _rope_tables():
    inv = 1.0 / (ROPE_THETA ** (np.arange(0, HEAD_DIM, 2, dtype=np.float64) / HEAD_DIM))
    ang = np.arange(SEQ, dtype=np.float64)[:, None] * inv[None, :]
    cos = np.concatenate([np.cos(ang), np.cos(ang)], axis=-1)
    sin = np.concatenate([-np.sin(ang), np.sin(ang)], axis=-1)
    return jnp.asarray(cos, F32), jnp.asarray(sin, F32)


def _tile_kinds(group_kinds, cols_per_group):
    kinds = []
    for kind, cols in zip(group_kinds, cols_per_group):
        assert cols % IN_TN == 0
        kinds += [kind] * (cols // IN_TN)
    assert len(kinds) == IN_WIDTH // IN_TN
    return jnp.asarray(kinds, jnp.int32)


def kernel(x, norm_ab, w_in_ab, w_out_ab, lam_ab, subln_ab, norm_c, w_in_c, w_out_c, final_norm):
    batch, seq, d = x.shape
    assert (seq, d) == (SEQ, D_MODEL) and x.dtype == F32
    assert w_in_ab.shape == (1, D_MODEL, IN_WIDTH) and w_in_c.shape == (1, D_MODEL, IN_WIDTH)
    cos, sin = _rope_tables()
    aw, bw = A_HEADS * HEAD_DIM, B_HEADS * B_VDIM
    kinds_ab = _tile_kinds(
        [KIND_ROPE_SCALED, KIND_ROPE, KIND_PLAIN, KIND_ROPE_SCALED, KIND_ROPE, KIND_PLAIN, KIND_PLAIN],
        [aw, aw, aw, bw, bw, bw, aw + bw])
    cw = C_HEADS * HEAD_DIM
    kinds_c = _tile_kinds([KIND_ROPE_SCALED, KIND_ROPE, KIND_PLAIN, KIND_PLAIN], [cw, cw, cw, cw])

    x2d = x.reshape(batch * seq, d)

    g_final = final_norm[None, :]
    h_ab = _inproj(x2d, norm_ab[0][None, :], w_in_ab[0], cos, sin, kinds_ab, batch)
    ya = _attn_a(h_ab, _dilated_bias_table())
    lam_init = 0.8 - 0.6 * math.exp(-0.3 * 0)
    yb = _attn_b(h_ab, lam_ab[0], subln_ab[0][None, :], lam_init)
    x1 = _outproj([ya, yb], h_ab, x2d, w_out_ab[0], g_final, batch, final_norm=False)

    h_c = _inproj(x1, norm_c[0][None, :], w_in_c[0], cos, sin, kinds_c, batch)
    yc = _attn_c(h_c)
    out = _outproj([yc], h_c, x1, w_out_c[0], g_final, batch, final_norm=True)
    return out.reshape(batch, seq, d)
```

```python
import functools
import math

import numpy as np
import jax
import jax.numpy as jnp
from jax import lax
from jax.experimental import pallas as pl
from jax.experimental.pallas import tpu as pltpu

D_MODEL = 2048
SEQ = 2048
HEAD_DIM = 128
ROPE_THETA = 10000.0
NORM_EPS = 1e-6
A_HEADS = 8
DIL_CONFIGS = ((128, 1), (512, 4), (2048, 16))
B_HEADS = 4
B_VDIM = 2 * HEAD_DIM
C_HEADS = 16
MOBA_BLOCK = 256
MOBA_TOPK = 3
N_MOBA_BLOCKS = SEQ // MOBA_BLOCK
IN_WIDTH = 8192
N_COL_BLOCKS = IN_WIDTH // HEAD_DIM
SCALE = HEAD_DIM ** -0.5
Q_SCALE = SCALE * math.log2(math.e)
NEG = -1e30

BF16 = jnp.bfloat16
F32 = jnp.float32

KIND_PLAIN, KIND_ROPE, KIND_ROPE_SCALED = 0, 1, 2

VMEM_LIMIT = 56 * 1024 * 1024


def _nt_dot(a, b):
    return lax.dot_general(a, b, (((1,), (1,)), ((), ())), preferred_element_type=F32)


IN_TM, IN_TN = 2048, 1024
IN_SUB_M, IN_SUB_N = 256, 256


def _inproj_kernel(kinds_ref, x_ref, g_ref, w_ref, cos_ref, sin_ref, o_ref, xn_ref):
    j = pl.program_id(1)

    @pl.when(j == 0)
    def _():
        x = x_ref[...]
        ms = jnp.mean(x * x, axis=-1, keepdims=True)
        xn_ref[...] = (x * lax.rsqrt(ms + NORM_EPS) * g_ref[...]).astype(BF16)

    kind = kinds_ref[j]
    scale = jnp.where(kind == KIND_ROPE_SCALED, Q_SCALE, 1.0).astype(F32)
    for nh in range(IN_TN // IN_SUB_N):
        w = w_ref[:, nh * IN_SUB_N:(nh + 1) * IN_SUB_N].astype(BF16)
        for mh in range(IN_TM // IN_SUB_M):
            rows = slice(mh * IN_SUB_M, (mh + 1) * IN_SUB_M)
            acc = jnp.dot(xn_ref[rows, :], w, preferred_element_type=F32)
            cos = jnp.where(kind == KIND_PLAIN, 1.0, cos_ref[rows, :] * scale)
            sin = jnp.where(kind == KIND_PLAIN, 0.0, sin_ref[rows, :] * scale)
            for c in range(IN_SUB_N // HEAD_DIM):
                t = acc[:, c * HEAD_DIM:(c + 1) * HEAD_DIM]
                o_ref[nh * (IN_SUB_N // HEAD_DIM) + c, rows, :] = (
                    t * cos + pltpu.roll(t, HEAD_DIM // 2, 1) * sin).astype(BF16)


def _inproj(x2d, g, w, cos, sin_signed, kinds, batch):
    rows = x2d.shape[0]
    s_tiles = SEQ // IN_TM
    grid = (rows // IN_TM, IN_WIDTH // IN_TN)
    blk_per_tile = IN_TN // HEAD_DIM
    return pl.pallas_call(
        _inproj_kernel,
        out_shape=jax.ShapeDtypeStruct((batch, N_COL_BLOCKS, SEQ, HEAD_DIM), BF16),
        grid_spec=pltpu.PrefetchScalarGridSpec(
            num_scalar_prefetch=1,
            grid=grid,
            in_specs=[
                pl.BlockSpec((IN_TM, D_MODEL), lambda i, j, k: (i, 0), pipeline_mode=pl.Buffered(1)),
                pl.BlockSpec((1, D_MODEL), lambda i, j, k: (0, 0)),
                pl.BlockSpec((D_MODEL, IN_TN), lambda i, j, k: (0, j)),
                pl.BlockSpec((IN_TM, HEAD_DIM), lambda i, j, k: (i % s_tiles, 0)),
                pl.BlockSpec((IN_TM, HEAD_DIM), lambda i, j, k: (i % s_tiles, 0)),
            ],
            out_specs=pl.BlockSpec((None, blk_per_tile, IN_TM, HEAD_DIM),
                                   lambda i, j, k: (i // s_tiles, j, i % s_tiles, 0)),
            scratch_shapes=[pltpu.VMEM((IN_TM, D_MODEL), BF16)],
        ),
        compiler_params=pltpu.CompilerParams(
            dimension_semantics=("parallel", "arbitrary"),
            vmem_limit_bytes=VMEM_LIMIT),
        name="norm_inproj",
    )(kinds, x2d, g, w, cos, sin_signed)


TQ = 256
N_QT = SEQ // TQ
PART = 256
A_PIPE_DEPTH, B_PIPE_DEPTH, C_PIPE_DEPTH = 3, 1, 2
A_INTERLEAVE, B_INTERLEAVE, C_INTERLEAVE = 2, 1, 2


ONES_ROWS = 16


def _store_vt(vt_ref, v_ref):
    dv = v_ref.shape[1]
    vt_ref[0:dv, :] = v_ref[...].astype(F32).T.astype(BF16)
    vt_ref[dv:, :] = jnp.ones((ONES_ROWS, v_ref.shape[0]), BF16)


def _softmax_pv_keymajor(parts, vt):
    m = functools.reduce(jnp.maximum, [jnp.max(s, axis=0, keepdims=True) for s in parts])
    p_t = jnp.concatenate([jnp.exp2(s - m).astype(BF16) for s in parts], axis=0)
    o_t = jnp.dot(vt, p_t, preferred_element_type=F32)
    dv = vt.shape[0] - ONES_ROWS
    return (o_t[:dv] * (1.0 / o_t[dv:dv + 1])).T


def _softmax_pv_querymajor(s, v):
    m = jnp.max(s, axis=-1, keepdims=True)
    p = jnp.exp2(s - m)
    inv_l = 1.0 / jnp.sum(p, axis=-1, keepdims=True)
    return jnp.dot(p.astype(BF16), v, preferred_element_type=F32) * inv_l


def _pipelined(units, scores, consume, depth):
    ready = [scores(u) for u in units[:depth]]
    for idx, u in enumerate(units):
        if idx + depth < len(units):
            ready.append(scores(units[idx + depth]))
        consume(u, ready.pop(0))


def _key_le_query_mask(key_axis=0):
    key = lax.broadcasted_iota(jnp.int32, (TQ, TQ), key_axis)
    query = lax.broadcasted_iota(jnp.int32, (TQ, TQ), 1 - key_axis)
    return key <= query


def _dilated_bias_table():
    key = np.arange(SEQ)[:, None]
    query = (SEQ - TQ) + np.arange(TQ)[None, :]
    d = query - key
    mult = np.zeros(d.shape, np.float64)
    for window, dil in DIL_CONFIGS:
        mult += (d >= 0) & (d % dil == 0) & (d <= window)
    return jnp.asarray(np.where(mult > 0, np.log2(np.maximum(mult, 1.0)), NEG), F32)


def _per_head(head_fn, n_shared, n_scratch, depth, interleave=1, slabs_per_head=1):
    def group_kernel(*refs):
        shared = refs[:n_shared]
        grouped = refs[n_shared:len(refs) - n_scratch]
        scratch = refs[len(refs) - n_scratch:]

        def view(r, g):
            return r.at[g] if slabs_per_head == 1 else r.at[pl.ds(g * slabs_per_head, slabs_per_head)]

        def body(g, carry):
            heads = [head_fn(*shared, *[view(r, g * interleave + h) for r in grouped],
                             *[s.at[h] for s in scratch]) for h in range(interleave)]
            n_units = len(heads[0][0])
            units = [(h, heads[h][0][idx]) for idx in range(n_units) for h in range(interleave)]
            _pipelined(units, lambda hu: heads[hu[0]][1](hu[1]),
                       lambda hu, s: heads[hu[0]][2](hu[1], s), depth * interleave)
            return carry

        lax.fori_loop(0, grouped[0].shape[0] // (slabs_per_head * interleave), body, 0)
    return group_kernel


def _heads_spec(off, n):
    assert off % n == 0
    return pl.BlockSpec((None, n, SEQ, HEAD_DIM), lambda b, h: (b, off // n + h, 0, 0))


HEADS_PER_STEP = 4


def _attn_a_kernel(bias_ref, q_ref, k_ref, v_ref, o_ref, vt_ref):
    _store_vt(vt_ref, v_ref)

    def scores(i):
        return _nt_dot(k_ref[0:(i + 1) * TQ, :], q_ref[i * TQ:(i + 1) * TQ, :])

    def consume(i, s_t):
        ext = (i + 1) * TQ
        off = SEQ - ext
        parts = [s_t[r:r + PART, :] + bias_ref[off + r:off + r + PART, :] for r in range(0, ext, PART)]
        o = _softmax_pv_keymajor(parts, vt_ref[:, 0:ext])
        o_ref[i * TQ:ext, :] = o.astype(BF16)

    return list(range(N_QT)), scores, consume


def _attn_a(h_ab, bias):
    batch, n = h_ab.shape[0], HEADS_PER_STEP
    return pl.pallas_call(
        _per_head(_attn_a_kernel, 1, 1, A_PIPE_DEPTH, interleave=A_INTERLEAVE),
        out_shape=jax.ShapeDtypeStruct((batch, A_HEADS, SEQ, HEAD_DIM), BF16),
        grid=(batch, A_HEADS // n),
        in_specs=[pl.BlockSpec((SEQ, TQ), lambda b, h: (0, 0)),
                  _heads_spec(0, n), _heads_spec(A_HEADS, n), _heads_spec(2 * A_HEADS, n)],
        out_specs=_heads_spec(0, n),
        scratch_shapes=[pltpu.VMEM((A_INTERLEAVE, HEAD_DIM + ONES_ROWS, SEQ), BF16)],
        compiler_params=pltpu.CompilerParams(
            dimension_semantics=("parallel", "parallel"),
            vmem_limit_bytes=VMEM_LIMIT),
        name="dilated_attn",
    )(bias, h_ab, h_ab, h_ab)


B_Q_OFF, B_K_OFF, B_V_OFF = 24, 32, 40


def _attn_b_kernel(lam_ref, g_ref, q_ref, k_ref, v_ref, o_ref, vcat_ref, *, lam_init):
    vcat_ref[:, :HEAD_DIM] = v_ref[0]
    vcat_ref[:, HEAD_DIM:] = v_ref[1]
    lam_v = lam_ref[...]
    lam = (jnp.exp(jnp.sum(lam_v[0:1] * lam_v[1:2], axis=-1, keepdims=True))
           - jnp.exp(jnp.sum(lam_v[2:3] * lam_v[3:4], axis=-1, keepdims=True)) + lam_init)
    query_ge_key = _key_le_query_mask(key_axis=1)
    first = {}

    def scores(u):
        i, t = u
        return _nt_dot(q_ref[t, i * TQ:(i + 1) * TQ, :], k_ref[t, 0:(i + 1) * TQ, :])

    def consume(u, s):
        i, t = u
        ext = (i + 1) * TQ
        rows = slice(i * TQ, ext)
        own = jnp.where(query_ge_key, s[:, i * TQ:], NEG)
        s = own if i == 0 else jnp.concatenate([s[:, :i * TQ], own], axis=1)
        o = _softmax_pv_querymajor(s, vcat_ref[0:ext, :])
        if t == 0:
            first[i] = o
            return
        o = first.pop(i) - lam * o
        ms = jnp.mean(o * o, axis=-1, keepdims=True)
        y = o * lax.rsqrt(ms + NORM_EPS) * g_ref[...] * (1.0 - lam_init)
        o_ref[0, rows, :] = y[:, :HEAD_DIM].astype(BF16)
        o_ref[1, rows, :] = y[:, HEAD_DIM:].astype(BF16)

    return [(i, t) for i in range(N_QT) for t in range(2)], scores, consume


def _attn_b(h_ab, lam_vec, subln_g, lam_init):
    batch = h_ab.shape[0]
    heads = HEADS_PER_STEP // 2
    spec = lambda off: _heads_spec(off, 2 * heads)
    return pl.pallas_call(
        _per_head(functools.partial(_attn_b_kernel, lam_init=lam_init), 2, 1, B_PIPE_DEPTH,
                  interleave=B_INTERLEAVE, slabs_per_head=2),
        out_shape=jax.ShapeDtypeStruct((batch, 2 * B_HEADS, SEQ, HEAD_DIM), BF16),
        grid=(batch, B_HEADS // heads),
        in_specs=[
            pl.BlockSpec((4, HEAD_DIM), lambda b, h: (0, 0)),
            pl.BlockSpec((1, B_VDIM), lambda b, h: (0, 0)),
            spec(B_Q_OFF), spec(B_K_OFF), spec(B_V_OFF),
        ],
        out_specs=spec(0),
        scratch_shapes=[pltpu.VMEM((B_INTERLEAVE, SEQ, B_VDIM), BF16)],
        compiler_params=pltpu.CompilerParams(
            dimension_semantics=("parallel", "parallel"),
            vmem_limit_bytes=VMEM_LIMIT),
        name="diff_attn",
    )(lam_vec, subln_g, h_ab, h_ab, h_ab)


KM_ROWS = 16


def _attn_c_kernel(q_ref, k_ref, v_ref, o_ref, vt_ref):
    assert TQ == MOBA_BLOCK
    key_blk = lax.broadcasted_iota(jnp.int32, (KM_ROWS, SEQ), 1) // MOBA_BLOCK
    pool = (lax.broadcasted_iota(jnp.int32, (KM_ROWS, SEQ), 0) == key_blk).astype(BF16)
    km = jnp.dot(pool, k_ref[...], preferred_element_type=F32) * (1.0 / MOBA_BLOCK)
    km_hi = km.astype(BF16)
    km_lo = (km - km_hi.astype(F32)).astype(BF16)
    km_hl = jnp.concatenate([km_hi, km_lo], axis=0)
    row = lax.broadcasted_iota(jnp.int32, (KM_ROWS, TQ), 0)
    _store_vt(vt_ref, v_ref)
    key_le_query = _key_le_query_mask()

    def scores(i):
        q = q_ref[i * TQ:(i + 1) * TQ, :]
        s_t = _nt_dot(k_ref[0:(i + 1) * TQ, :], q)
        g = _nt_dot(km_hl, q) if i > 0 else None
        return s_t, g

    def consume(i, sg):
        s_t, g = sg
        ext = (i + 1) * TQ
        parts = []
        if i > 0:
            past = row < i
            gate = jnp.where(past, g[:KM_ROWS] + g[KM_ROWS:], -jnp.inf)
            rank = jnp.zeros(gate.shape, F32)
            for j in range(i):
                gj = gate[j:j + 1, :]
                beats = (gj > gate) | ((gj == gate) & (j < row))
                rank = rank + beats.astype(F32)
            bias = jnp.where(past & (rank < MOBA_TOPK), 0.0, NEG)
            for n in range(i):
                for r in range(n * TQ, (n + 1) * TQ, PART):
                    parts.append(s_t[r:r + PART, :] + bias[n:n + 1, :])
        for r in range(0, TQ, PART):
            parts.append(jnp.where(key_le_query[r:r + PART, :], s_t[i * TQ + r:i * TQ + r + PART, :], NEG))
        o_ref[i * TQ:ext, :] = _softmax_pv_keymajor(parts, vt_ref[:, 0:ext]).astype(BF16)

    return list(range(N_QT)), scores, consume


def _attn_c(h_c):
    batch, n = h_c.shape[0], HEADS_PER_STEP
    return pl.pallas_call(
        _per_head(_attn_c_kernel, 0, 1, C_PIPE_DEPTH, interleave=C_INTERLEAVE),
        out_shape=jax.ShapeDtypeStruct((batch, C_HEADS, SEQ, HEAD_DIM), BF16),
        grid=(batch, C_HEADS // n),
        in_specs=[_heads_spec(0, n), _heads_spec(C_HEADS, n), _heads_spec(2 * C_HEADS, n)],
        out_specs=_heads_spec(0, n),
        scratch_shapes=[pltpu.VMEM((C_INTERLEAVE, HEAD_DIM + ONES_ROWS, SEQ), BF16)],
        compiler_params=pltpu.CompilerParams(
            dimension_semantics=("parallel", "parallel"),
            vmem_limit_bytes=VMEM_LIMIT),
        name="moba_attn",
    )(h_c, h_c, h_c)


OUT_TM = 512
OUT_SUB_M, OUT_SUB_N = 256, 256
Z_BLOCK_OFF = 3


def _outproj_kernel(*refs, n_y, final_norm):
    y_refs = refs[:n_y]
    z_ref, x_ref, w_ref, g_ref, o_ref, ysc_ref, wb_ref = refs[n_y:]
    y_blocks = [(y_ref, hh) for y_ref in y_refs for hh in range(y_ref.shape[0])]

    @pl.when(pl.program_id(0) == 0)
    def _():
        wb_ref[...] = w_ref[...].astype(BF16)
    for r in range(OUT_TM // OUT_SUB_M):
        rows = slice(r * OUT_SUB_M, (r + 1) * OUT_SUB_M)
        for c, (y_ref, hh) in enumerate(y_blocks):
            z = z_ref[c, rows, :].astype(F32)
            gate = z / (1.0 + jnp.exp(-z))
            ysc_ref[rows, c * HEAD_DIM:(c + 1) * HEAD_DIM] = (
                y_ref[hh, rows, :].astype(F32) * gate).astype(BF16)
        ssq = jnp.zeros((OUT_SUB_M, 1), F32)
        for n in range(D_MODEL // OUT_SUB_N):
            cols = slice(n * OUT_SUB_N, (n + 1) * OUT_SUB_N)
            h = x_ref[rows, cols] + jnp.dot(ysc_ref[rows, :], wb_ref[:, cols],
                                            preferred_element_type=F32)
            o_ref[rows, cols] = h
            if final_norm:
                ssq = ssq + jnp.sum(h * h, axis=-1, keepdims=True)
        if final_norm:
            inv = lax.rsqrt(ssq * (1.0 / D_MODEL) + NORM_EPS)
            o_ref[rows, :] = o_ref[rows, :] * inv * g_ref[...]


def _outproj(ys, h_in, x2d, w, g_final, batch, final_norm):
    rows = x2d.shape[0]
    s_tiles = SEQ // OUT_TM
    n_z = D_MODEL // HEAD_DIM
    y_specs = [pl.BlockSpec((None, y.shape[1], OUT_TM, HEAD_DIM),
                            lambda i: (i // s_tiles, 0, i % s_tiles, 0)) for y in ys]
    assert sum(y.shape[1] for y in ys) == n_z
    return pl.pallas_call(
        functools.partial(_outproj_kernel, n_y=len(ys), final_norm=final_norm),
        out_shape=jax.ShapeDtypeStruct((rows, D_MODEL), F32),
        grid=(rows // OUT_TM,),
        in_specs=y_specs + [
            pl.BlockSpec((None, n_z, OUT_TM, HEAD_DIM),
                         lambda i: (i // s_tiles, Z_BLOCK_OFF, i % s_tiles, 0)),
            pl.BlockSpec((OUT_TM, D_MODEL), lambda i: (i, 0)),
            pl.BlockSpec((D_MODEL, D_MODEL), lambda i: (0, 0), pipeline_mode=pl.Buffered(1)),
            pl.BlockSpec((1, D_MODEL), lambda i: (0, 0)),
        ],
        out_specs=pl.BlockSpec((OUT_TM, D_MODEL), lambda i: (i, 0)),
        scratch_shapes=[pltpu.VMEM((OUT_TM, D_MODEL), BF16), pltpu.VMEM((D_MODEL, D_MODEL), BF16)],
        compiler_params=pltpu.CompilerParams(
            dimension_semantics=("arbitrary",),
            vmem_limit_bytes=VMEM_LIMIT),
        name="gate_outproj",
    )(*ys, h_in, x2d, w, g_final)


def _rope_tables():
    inv = 1.0 / (ROPE_THETA ** (np.arange(0, HEAD_DIM, 2, dtype=np.float64) / HEAD_DIM))
    ang = np.arange(SEQ, dtype=np.float64)[:, None] * inv[None, :]
    cos = np.concatenate([np.cos(ang), np.cos(ang)], axis=-1)
    sin = np.concatenate([-np.sin(ang), np.sin(ang)], axis=-1)
    return jnp.asarray(cos, F32), jnp.asarray(sin, F32)


def _tile_kinds(group_kinds, cols_per_group):
    kinds = []
    for kind, cols in zip(group_kinds, cols_per_group):
        assert cols % IN_TN == 0
        kinds += [kind] * (cols // IN_TN)
    assert len(kinds) == IN_WIDTH // IN_TN
    return jnp.asarray(kinds, jnp.int32)


def kernel(x, norm_ab, w_in_ab, w_out_ab, lam_ab, subln_ab, norm_c, w_in_c, w_out_c, final_norm):
    batch, seq, d = x.shape
    assert (seq, d) == (SEQ, D_MODEL) and x.dtype == F32
    assert w_in_ab.shape == (1, D_MODEL, IN_WIDTH) and w_in_c.shape == (1, D_MODEL, IN_WIDTH)
    cos, sin = _rope_tables()
    aw, bw = A_HEADS * HEAD_DIM, B_HEADS * B_VDIM
    kinds_ab = _tile_kinds(
        [KIND_ROPE_SCALED, KIND_ROPE, KIND_PLAIN, KIND_ROPE_SCALED, KIND_ROPE, KIND_PLAIN, KIND_PLAIN],
        [aw, aw, aw, bw, bw, bw, aw + bw])
    cw = C_HEADS * HEAD_DIM
    kinds_c = _tile_kinds([KIND_ROPE_SCALED, KIND_ROPE, KIND_PLAIN, KIND_PLAIN], [cw, cw, cw, cw])

    x2d = x.reshape(batch * seq, d)

    g_final = final_norm[None, :]
    h_ab = _inproj(x2d, norm_ab[0][None, :], w_in_ab[0], cos, sin, kinds_ab, batch)
    ya = _attn_a(h_ab, _dilated_bias_table())
    lam_init = 0.8 - 0.6 * math.exp(-0.3 * 0)
    yb = _attn_b(h_ab, lam_ab[0], subln_ab[0][None, :], lam_init)
    x1 = _outproj([ya, yb], h_ab, x2d, w_out_ab[0], g_final, batch, final_norm=False)

    h_c = _inproj(x1, norm_c[0][None, :], w_in_c[0], cos, sin, kinds_c, batch)
    yc = _attn_c(h_c)
    out = _outproj([yc], h_c, x1, w_out_c[0], g_final, batch, final_norm=True)
    return out.reshape(batch, seq, d)
```

```python
import functools
import math

import numpy as np
import jax
import jax.numpy as jnp
from jax import lax
from jax.experimental import pallas as pl
from jax.experimental.pallas import tpu as pltpu

D_MODEL = 2048
SEQ = 2048
HEAD_DIM = 128
ROPE_THETA = 10000.0
NORM_EPS = 1e-6
A_HEADS = 8
DIL_CONFIGS = ((128, 1), (512, 4), (2048, 16))
B_HEADS = 4
B_VDIM = 2 * HEAD_DIM
C_HEADS = 16
MOBA_BLOCK = 256
MOBA_TOPK = 3
N_MOBA_BLOCKS = SEQ // MOBA_BLOCK
IN_WIDTH = 8192
N_COL_BLOCKS = IN_WIDTH // HEAD_DIM
SCALE = HEAD_DIM ** -0.5
Q_SCALE = SCALE * math.log2(math.e)
NEG = -1e30

BF16 = jnp.bfloat16
F32 = jnp.float32

KIND_PLAIN, KIND_ROPE, KIND_ROPE_SCALED = 0, 1, 2

VMEM_LIMIT = 56 * 1024 * 1024


def _nt_dot(a, b):
    return lax.dot_general(a, b, (((1,), (1,)), ((), ())), preferred_element_type=F32)


IN_TM, IN_TN = 2048, 1024
IN_SUB_M, IN_SUB_N = 256, 256


def _inproj_kernel(kinds_ref, x_hbm, g_ref, w_ref, cos_ref, sin_ref, o_ref, xn_ref, x_ref, x_sem):
    i, j = pl.program_id(0), pl.program_id(1)

    def x_copy(row_tile):
        return pltpu.make_async_copy(x_hbm.at[pl.ds(row_tile * IN_TM, IN_TM), :], x_ref, x_sem)

    @pl.when((i == 0) & (j == 0))
    def _():
        x_copy(0).start()

    @pl.when((j == 1) & (i + 1 < pl.num_programs(0)))
    def _():
        x_copy(i + 1).start()

    kind = kinds_ref[j]
    scale = jnp.where(kind == KIND_ROPE_SCALED, Q_SCALE, 1.0).astype(F32)

    def tile(normalise):
        for nh in range(IN_TN // IN_SUB_N):
            w = w_ref[:, nh * IN_SUB_N:(nh + 1) * IN_SUB_N].astype(BF16)
            for mh in range(IN_TM // IN_SUB_M):
                rows = slice(mh * IN_SUB_M, (mh + 1) * IN_SUB_M)
                if normalise and nh == 0:
                    x = x_ref[rows, :]
                    ms = jnp.mean(x * x, axis=-1, keepdims=True)
                    xn_ref[rows, :] = (x * lax.rsqrt(ms + NORM_EPS) * g_ref[...]).astype(BF16)
                acc = jnp.dot(xn_ref[rows, :], w, preferred_element_type=F32)
                cos = jnp.where(kind == KIND_PLAIN, 1.0, cos_ref[rows, :] * scale)
                sin = jnp.where(kind == KIND_PLAIN, 0.0, sin_ref[rows, :] * scale)
                for c in range(IN_SUB_N // HEAD_DIM):
                    t = acc[:, c * HEAD_DIM:(c + 1) * HEAD_DIM]
                    o_ref[nh * (IN_SUB_N // HEAD_DIM) + c, rows, :] = (
                        t * cos + pltpu.roll(t, HEAD_DIM // 2, 1) * sin).astype(BF16)

    @pl.when(j == 0)
    def _():
        x_copy(i).wait()
        tile(True)

    pl.when(j != 0)(lambda: tile(False))


def _inproj(x2d, g, w, cos, sin_signed, kinds, batch):
    rows = x2d.shape[0]
    s_tiles = SEQ // IN_TM
    grid = (rows // IN_TM, IN_WIDTH // IN_TN)
    blk_per_tile = IN_TN // HEAD_DIM
    return pl.pallas_call(
        _inproj_kernel,
        out_shape=jax.ShapeDtypeStruct((batch, N_COL_BLOCKS, SEQ, HEAD_DIM), BF16),
        grid_spec=pltpu.PrefetchScalarGridSpec(
            num_scalar_prefetch=1,
            grid=grid,
            in_specs=[
                pl.BlockSpec(memory_space=pl.ANY),
                pl.BlockSpec((1, D_MODEL), lambda i, j, k: (0, 0)),
                pl.BlockSpec((D_MODEL, IN_TN), lambda i, j, k: (0, j)),
                pl.BlockSpec((IN_TM, HEAD_DIM), lambda i, j, k: (i % s_tiles, 0)),
                pl.BlockSpec((IN_TM, HEAD_DIM), lambda i, j, k: (i % s_tiles, 0)),
            ],
            out_specs=pl.BlockSpec((None, blk_per_tile, IN_TM, HEAD_DIM),
                                   lambda i, j, k: (i // s_tiles, j, i % s_tiles, 0)),
            scratch_shapes=[pltpu.VMEM((IN_TM, D_MODEL), BF16), pltpu.VMEM((IN_TM, D_MODEL), F32),
                            pltpu.SemaphoreType.DMA],
        ),
        compiler_params=pltpu.CompilerParams(
            dimension_semantics=("arbitrary", "arbitrary"),
            vmem_limit_bytes=VMEM_LIMIT),
        name="norm_inproj",
    )(kinds, x2d, g, w, cos, sin_signed)


TQ = 256
N_QT = SEQ // TQ
PART = 256
A_PIPE_DEPTH, B_PIPE_DEPTH, C_PIPE_DEPTH = 5, 1, 2
A_INTERLEAVE, B_INTERLEAVE, C_INTERLEAVE = 1, 1, 2


ONES_ROWS = 16


def _store_vt(vt_ref, v_ref):
    dv = v_ref.shape[1]
    vt_ref[0:dv, :] = v_ref[...].astype(F32).T.astype(BF16)
    vt_ref[dv:, :] = jnp.ones((ONES_ROWS, v_ref.shape[0]), BF16)


def _softmax_pv_keymajor(parts, vt):
    m = functools.reduce(jnp.maximum, [jnp.max(s, axis=0, keepdims=True) for s in parts])
    p_t = jnp.concatenate([jnp.exp2(s - m).astype(BF16) for s in parts], axis=0)
    o_t = jnp.dot(vt, p_t, preferred_element_type=F32)
    dv = vt.shape[0] - ONES_ROWS
    return (o_t[:dv] * (1.0 / o_t[dv:dv + 1])).T


def _softmax_pv_querymajor(s, v):
    m = jnp.max(s, axis=-1, keepdims=True)
    p = jnp.exp2(s - m)
    inv_l = 1.0 / jnp.sum(p, axis=-1, keepdims=True)
    return jnp.dot(p.astype(BF16), v, preferred_element_type=F32) * inv_l


def _pipelined(units, scores, consume, depth):
    ready = [scores(u) for u in units[:depth]]
    for idx, u in enumerate(units):
        if idx + depth < len(units):
            ready.append(scores(units[idx + depth]))
        consume(u, ready.pop(0))


def _key_le_query_mask(key_axis=0):
    key = lax.broadcasted_iota(jnp.int32, (TQ, TQ), key_axis)
    query = lax.broadcasted_iota(jnp.int32, (TQ, TQ), 1 - key_axis)
    return key <= query


def _dilated_bias_table():
    key = np.arange(SEQ)[:, None]
    query = (SEQ - TQ) + np.arange(TQ)[None, :]
    d = query - key
    mult = np.zeros(d.shape, np.float64)
    for window, dil in DIL_CONFIGS:
        mult += (d >= 0) & (d % dil == 0) & (d <= window)
    return jnp.asarray(np.where(mult > 0, np.log2(np.maximum(mult, 1.0)), NEG), F32)


def _per_head(head_fn, n_shared, n_scratch, depth, interleave=1, slabs_per_head=1):
    def group_kernel(*refs):
        shared = refs[:n_shared]
        grouped = refs[n_shared:len(refs) - n_scratch]
        scratch = refs[len(refs) - n_scratch:]

        def view(r, g):
            return r.at[g] if slabs_per_head == 1 else r.at[pl.ds(g * slabs_per_head, slabs_per_head)]

        def body(g, carry):
            heads = [head_fn(*shared, *[view(r, g * interleave + h) for r in grouped],
                             *[s.at[h] for s in scratch]) for h in range(interleave)]
            n_units = len(heads[0][0])
            units = [(h, heads[h][0][idx]) for idx in range(n_units) for h in range(interleave)]
            _pipelined(units, lambda hu: heads[hu[0]][1](hu[1]),
                       lambda hu, s: heads[hu[0]][2](hu[1], s), depth * interleave)
            return carry

        lax.fori_loop(0, grouped[0].shape[0] // (slabs_per_head * interleave), body, 0)
    return group_kernel


def _heads_spec(off, n):
    assert off % n == 0
    return pl.BlockSpec((None, n, SEQ, HEAD_DIM), lambda b, h: (b, off // n + h, 0, 0))


HEADS_PER_STEP = 4


def _attn_a_kernel(bias_ref, q_ref, k_ref, v_ref, o_ref, vt_ref):
    _store_vt(vt_ref, v_ref)

    def scores(i):
        return _nt_dot(k_ref[0:(i + 1) * TQ, :], q_ref[i * TQ:(i + 1) * TQ, :])

    def consume(i, s_t):
        ext = (i + 1) * TQ
        off = SEQ - ext
        parts = [s_t[r:r + PART, :] + bias_ref[off + r:off + r + PART, :] for r in range(0, ext, PART)]
        o = _softmax_pv_keymajor(parts, vt_ref[:, 0:ext])
        o_ref[i * TQ:ext, :] = o.astype(BF16)

    return list(range(N_QT)), scores, consume


def _attn_a(h_ab, bias):
    batch, n = h_ab.shape[0], HEADS_PER_STEP
    return pl.pallas_call(
        _per_head(_attn_a_kernel, 1, 1, A_PIPE_DEPTH, interleave=A_INTERLEAVE),
        out_shape=jax.ShapeDtypeStruct((batch, A_HEADS, SEQ, HEAD_DIM), BF16),
        grid=(batch, A_HEADS // n),
        in_specs=[pl.BlockSpec((SEQ, TQ), lambda b, h: (0, 0)),
                  _heads_spec(0, n), _heads_spec(A_HEADS, n), _heads_spec(2 * A_HEADS, n)],
        out_specs=_heads_spec(0, n),
        scratch_shapes=[pltpu.VMEM((A_INTERLEAVE, HEAD_DIM + ONES_ROWS, SEQ), BF16)],
        compiler_params=pltpu.CompilerParams(
            dimension_semantics=("parallel", "parallel"),
            vmem_limit_bytes=VMEM_LIMIT),
        name="dilated_attn",
    )(bias, h_ab, h_ab, h_ab)


B_Q_OFF, B_K_OFF, B_V_OFF = 24, 32, 40


def _attn_b_kernel(lam_ref, g_ref, q_ref, k_ref, v_ref, o_ref, vcat_ref, *, lam_init):
    vcat_ref[:, :HEAD_DIM] = v_ref[0]
    vcat_ref[:, HEAD_DIM:] = v_ref[1]
    lam_v = lam_ref[...]
    lam = (jnp.exp(jnp.sum(lam_v[0:1] * lam_v[1:2], axis=-1, keepdims=True))
           - jnp.exp(jnp.sum(lam_v[2:3] * lam_v[3:4], axis=-1, keepdims=True)) + lam_init)
    query_ge_key = _key_le_query_mask(key_axis=1)
    first = {}

    def scores(u):
        i, t = u
        return _nt_dot(q_ref[t, i * TQ:(i + 1) * TQ, :], k_ref[t, 0:(i + 1) * TQ, :])

    def consume(u, s):
        i, t = u
        ext = (i + 1) * TQ
        rows = slice(i * TQ, ext)
        own = jnp.where(query_ge_key, s[:, i * TQ:], NEG)
        s = own if i == 0 else jnp.concatenate([s[:, :i * TQ], own], axis=1)
        o = _softmax_pv_querymajor(s, vcat_ref[0:ext, :])
        if t == 0:
            first[i] = o
            return
        o = first.pop(i) - lam * o
        ms = jnp.mean(o * o, axis=-1, keepdims=True)
        y = o * lax.rsqrt(ms + NORM_EPS) * g_ref[...] * (1.0 - lam_init)
        o_ref[0, rows, :] = y[:, :HEAD_DIM].astype(BF16)
        o_ref[1, rows, :] = y[:, HEAD_DIM:].astype(BF16)

    return [(i, t) for i in range(N_QT) for t in range(2)], scores, consume


def _attn_b(h_ab, lam_vec, subln_g, lam_init):
    batch = h_ab.shape[0]
    heads = HEADS_PER_STEP // 2
    spec = lambda off: _heads_spec(off, 2 * heads)
    return pl.pallas_call(
        _per_head(functools.partial(_attn_b_kernel, lam_init=lam_init), 2, 1, B_PIPE_DEPTH,
                  interleave=B_INTERLEAVE, slabs_per_head=2),
        out_shape=jax.ShapeDtypeStruct((batch, 2 * B_HEADS, SEQ, HEAD_DIM), BF16),
        grid=(batch, B_HEADS // heads),
        in_specs=[
            pl.BlockSpec((4, HEAD_DIM), lambda b, h: (0, 0)),
            pl.BlockSpec((1, B_VDIM), lambda b, h: (0, 0)),
            spec(B_Q_OFF), spec(B_K_OFF), spec(B_V_OFF),
        ],
        out_specs=spec(0),
        scratch_shapes=[pltpu.VMEM((B_INTERLEAVE, SEQ, B_VDIM), BF16)],
        compiler_params=pltpu.CompilerParams(
            dimension_semantics=("parallel", "parallel"),
            vmem_limit_bytes=VMEM_LIMIT),
        name="diff_attn",
    )(lam_vec, subln_g, h_ab, h_ab, h_ab)


KM_ROWS = 16


def _attn_c_kernel(q_ref, k_ref, v_ref, o_ref, vt_ref):
    assert TQ == MOBA_BLOCK
    key_blk = lax.broadcasted_iota(jnp.int32, (KM_ROWS, SEQ), 1) // MOBA_BLOCK
    pool = (lax.broadcasted_iota(jnp.int32, (KM_ROWS, SEQ), 0) == key_blk).astype(BF16)
    km = jnp.dot(pool, k_ref[...], preferred_element_type=F32) * (1.0 / MOBA_BLOCK)
    km_hi = km.astype(BF16)
    km_lo = (km - km_hi.astype(F32)).astype(BF16)
    km_hl = jnp.concatenate([km_hi, km_lo], axis=0)
    row = lax.broadcasted_iota(jnp.int32, (KM_ROWS, TQ), 0)
    _store_vt(vt_ref, v_ref)
    key_le_query = _key_le_query_mask()

    def scores(i):
        q = q_ref[i * TQ:(i + 1) * TQ, :]
        s_t = _nt_dot(k_ref[0:(i + 1) * TQ, :], q)
        g = _nt_dot(km_hl, q) if i > 0 else None
        return s_t, g

    def consume(i, sg):
        s_t, g = sg
        ext = (i + 1) * TQ
        parts = []
        if i > 0:
            past = row < i
            gate = jnp.where(past, g[:KM_ROWS] + g[KM_ROWS:], -jnp.inf)
            rank = jnp.zeros(gate.shape, F32)
            for j in range(i):
                gj = gate[j:j + 1, :]
                beats = (gj > gate) | ((gj == gate) & (j < row))
                rank = rank + beats.astype(F32)
            bias = jnp.where(past & (rank < MOBA_TOPK), 0.0, NEG)
            for n in range(i):
                for r in range(n * TQ, (n + 1) * TQ, PART):
                    parts.append(s_t[r:r + PART, :] + bias[n:n + 1, :])
        for r in range(0, TQ, PART):
            parts.append(jnp.where(key_le_query[r:r + PART, :], s_t[i * TQ + r:i * TQ + r + PART, :], NEG))
        o_ref[i * TQ:ext, :] = _softmax_pv_keymajor(parts, vt_ref[:, 0:ext]).astype(BF16)

    return list(range(N_QT)), scores, consume


def _attn_c(h_c):
    batch, n = h_c.shape[0], HEADS_PER_STEP
    return pl.pallas_call(
        _per_head(_attn_c_kernel, 0, 1, C_PIPE_DEPTH, interleave=C_INTERLEAVE),
        out_shape=jax.ShapeDtypeStruct((batch, C_HEADS, SEQ, HEAD_DIM), BF16),
        grid=(batch, C_HEADS // n),
        in_specs=[_heads_spec(0, n), _heads_spec(C_HEADS, n), _heads_spec(2 * C_HEADS, n)],
        out_specs=_heads_spec(0, n),
        scratch_shapes=[pltpu.VMEM((C_INTERLEAVE, HEAD_DIM + ONES_ROWS, SEQ), BF16)],
        compiler_params=pltpu.CompilerParams(
            dimension_semantics=("parallel", "parallel"),
            vmem_limit_bytes=VMEM_LIMIT),
        name="moba_attn",
    )(h_c, h_c, h_c)


OUT_TM = 512
OUT_SUB_M, OUT_SUB_N = 256, 256
Z_BLOCK_OFF = 3


def _outproj_kernel(*refs, n_y, final_norm):
    y_refs = refs[:n_y]
    z_ref, x_ref, w_ref, g_ref, o_ref, ysc_ref, wb_ref = refs[n_y:]
    y_blocks = [(y_ref, hh) for y_ref in y_refs for hh in range(y_ref.shape[0])]

    @pl.when(pl.program_id(0) == 0)
    def _():
        wb_ref[...] = w_ref[...].astype(BF16)
    for r in range(OUT_TM // OUT_SUB_M):
        rows = slice(r * OUT_SUB_M, (r + 1) * OUT_SUB_M)
        for c, (y_ref, hh) in enumerate(y_blocks):
            z = z_ref[c, rows, :].astype(F32)
            gate = z / (1.0 + jnp.exp(-z))
            ysc_ref[rows, c * HEAD_DIM:(c + 1) * HEAD_DIM] = (
                y_ref[hh, rows, :].astype(F32) * gate).astype(BF16)
        ssq = jnp.zeros((OUT_SUB_M, 1), F32)
        for n in range(D_MODEL // OUT_SUB_N):
            cols = slice(n * OUT_SUB_N, (n + 1) * OUT_SUB_N)
            h = x_ref[rows, cols] + jnp.dot(ysc_ref[rows, :], wb_ref[:, cols],
                                            preferred_element_type=F32)
            o_ref[rows, cols] = h
            if final_norm:
                ssq = ssq + jnp.sum(h * h, axis=-1, keepdims=True)
        if final_norm:
            inv = lax.rsqrt(ssq * (1.0 / D_MODEL) + NORM_EPS)
            o_ref[rows, :] = o_ref[rows, :] * inv * g_ref[...]


def _outproj(ys, h_in, x2d, w, g_final, batch, final_norm):
    rows = x2d.shape[0]
    s_tiles = SEQ // OUT_TM
    n_z = D_MODEL // HEAD_DIM
    y_specs = [pl.BlockSpec((None, y.shape[1], OUT_TM, HEAD_DIM),
                            lambda i: (i // s_tiles, 0, i % s_tiles, 0)) for y in ys]
    assert sum(y.shape[1] for y in ys) == n_z
    return pl.pallas_call(
        functools.partial(_outproj_kernel, n_y=len(ys), final_norm=final_norm),
        out_shape=jax.ShapeDtypeStruct((rows, D_MODEL), F32),
        grid=(rows // OUT_TM,),
        in_specs=y_specs + [
            pl.BlockSpec((None, n_z, OUT_TM, HEAD_DIM),
                         lambda i: (i // s_tiles, Z_BLOCK_OFF, i % s_tiles, 0)),
            pl.BlockSpec((OUT_TM, D_MODEL), lambda i: (i, 0)),
            pl.BlockSpec((D_MODEL, D_MODEL), lambda i: (0, 0), pipeline_mode=pl.Buffered(1)),
            pl.BlockSpec((1, D_MODEL), lambda i: (0, 0)),
        ],
        out_specs=pl.BlockSpec((OUT_TM, D_MODEL), lambda i: (i, 0)),
        scratch_shapes=[pltpu.VMEM((OUT_TM, D_MODEL), BF16), pltpu.VMEM((D_MODEL, D_MODEL), BF16)],
        compiler_params=pltpu.CompilerParams(
            dimension_semantics=("arbitrary",),
            vmem_limit_bytes=VMEM_LIMIT),
        name="gate_outproj",
    )(*ys, h_in, x2d, w, g_final)


def _rope_tables():
    inv = 1.0 / (ROPE_THETA ** (np.arange(0, HEAD_DIM, 2, dtype=np.float64) / HEAD_DIM))
    ang = np.arange(SEQ, dtype=np.float64)[:, None] * inv[None, :]
    cos = np.concatenate([np.cos(ang), np.cos(ang)], axis=-1)
    sin = np.concatenate([-np.sin(ang), np.sin(ang)], axis=-1)
    return jnp.asarray(cos, F32), jnp.asarray(sin, F32)


def _tile_kinds(group_kinds, cols_per_group):
    kinds = []
    for kind, cols in zip(group_kinds, cols_per_group):
        assert cols % IN_TN == 0
        kinds += [kind] * (cols // IN_TN)
    assert len(kinds) == IN_WIDTH // IN_TN
    return jnp.asarray(kinds, jnp.int32)


def kernel(x, norm_ab, w_in_ab, w_out_ab, lam_ab, subln_ab, norm_c, w_in_c, w_out_c, final_norm):
    batch, seq, d = x.shape
    assert (seq, d) == (SEQ, D_MODEL) and x.dtype == F32
    assert w_in_ab.shape == (1, D_MODEL, IN_WIDTH) and w_in_c.shape == (1, D_MODEL, IN_WIDTH)
    cos, sin = _rope_tables()
    aw, bw = A_HEADS * HEAD_DIM, B_HEADS * B_VDIM
    kinds_ab = _tile_kinds(
        [KIND_ROPE_SCALED, KIND_ROPE, KIND_PLAIN, KIND_ROPE_SCALED, KIND_ROPE, KIND_PLAIN, KIND_PLAIN],
        [aw, aw, aw, bw, bw, bw, aw + bw])
    cw = C_HEADS * HEAD_DIM
    kinds_c = _tile_kinds([KIND_ROPE_SCALED, KIND_ROPE, KIND_PLAIN, KIND_PLAIN], [cw, cw, cw, cw])

    x2d = x.reshape(batch * seq, d)

    g_final = final_norm[None, :]
    h_ab = _inproj(x2d, norm_ab[0][None, :], w_in_ab[0], cos, sin, kinds_ab, batch)
    ya = _attn_a(h_ab, _dilated_bias_table())
    lam_init = 0.8 - 0.6 * math.exp(-0.3 * 0)
    yb = _attn_b(h_ab, lam_ab[0], subln_ab[0][None, :], lam_init)
    x1 = _outproj([ya, yb], h_ab, x2d, w_out_ab[0], g_final, batch, final_norm=False)

    h_c = _inproj(x1, norm_c[0][None, :], w_in_c[0], cos, sin, kinds_c, batch)
    yc = _attn_c(h_c)
    out = _outproj([yc], h_c, x1, w_out_c[0], g_final, batch, final_norm=True)
    return out.reshape(batch, seq, d)
```

```python
import functools
import math

import numpy as np
import jax
import jax.numpy as jnp
from jax import lax
from jax.experimental import pallas as pl
from jax.experimental.pallas import tpu as pltpu

D_MODEL = 2048
SEQ = 2048
HEAD_DIM = 128
ROPE_THETA = 10000.0
NORM_EPS = 1e-6
A_HEADS = 8
DIL_CONFIGS = ((128, 1), (512, 4), (2048, 16))
B_HEADS = 4
B_VDIM = 2 * HEAD_DIM
C_HEADS = 16
MOBA_BLOCK = 256
MOBA_TOPK = 3
N_MOBA_BLOCKS = SEQ // MOBA_BLOCK
IN_WIDTH = 8192
N_COL_BLOCKS = IN_WIDTH // HEAD_DIM
SCALE = HEAD_DIM ** -0.5
Q_SCALE = SCALE * math.log2(math.e)
NEG = -1e30

BF16 = jnp.bfloat16
F32 = jnp.float32

KIND_PLAIN, KIND_ROPE, KIND_ROPE_SCALED = 0, 1, 2

VMEM_LIMIT = 56 * 1024 * 1024


def _nt_dot(a, b):
    return lax.dot_general(a, b, (((1,), (1,)), ((), ())), preferred_element_type=F32)


IN_TM, IN_TN = 2048, 1024
IN_SUB_M, IN_SUB_N = 256, 256


def _inproj_kernel(kinds_ref, x_hbm, g_ref, w_ref, cos_ref, sin_ref, o_ref, xn_ref, x_ref, x_sem):
    i, j = pl.program_id(0), pl.program_id(1)

    n_chunks = IN_TM // IN_SUB_M

    def x_copy(row_tile, c):
        src = x_hbm.at[pl.ds(row_tile * IN_TM + c * IN_SUB_M, IN_SUB_M), :]
        return pltpu.make_async_copy(src, x_ref.at[pl.ds(c * IN_SUB_M, IN_SUB_M), :], x_sem.at[c])

    @pl.when((i == 0) & (j == 0))
    def _():
        for c in range(n_chunks):
            x_copy(0, c).start()

    @pl.when((j == 1) & (i + 1 < pl.num_programs(0)))
    def _():
        for c in range(n_chunks):
            x_copy(i + 1, c).start()

    kind = kinds_ref[j]
    scale = jnp.where(kind == KIND_ROPE_SCALED, Q_SCALE, 1.0).astype(F32)

    def tile(normalise):
        for nh in range(IN_TN // IN_SUB_N):
            w = w_ref[:, nh * IN_SUB_N:(nh + 1) * IN_SUB_N].astype(BF16)
            for mh in range(IN_TM // IN_SUB_M):
                rows = slice(mh * IN_SUB_M, (mh + 1) * IN_SUB_M)
                if normalise and nh == 0:
                    x_copy(i, mh).wait()
                    x = x_ref[rows, :]
                    ms = jnp.mean(x * x, axis=-1, keepdims=True)
                    xn_ref[rows, :] = (x * lax.rsqrt(ms + NORM_EPS) * g_ref[...]).astype(BF16)
                acc = jnp.dot(xn_ref[rows, :], w, preferred_element_type=F32)
                cos = jnp.where(kind == KIND_PLAIN, 1.0, cos_ref[rows, :] * scale)
                sin = jnp.where(kind == KIND_PLAIN, 0.0, sin_ref[rows, :] * scale)
                for c in range(IN_SUB_N // HEAD_DIM):
                    t = acc[:, c * HEAD_DIM:(c + 1) * HEAD_DIM]
                    o_ref[nh * (IN_SUB_N // HEAD_DIM) + c, rows, :] = (
                        t * cos + pltpu.roll(t, HEAD_DIM // 2, 1) * sin).astype(BF16)

    pl.when(j == 0)(lambda: tile(True))
    pl.when(j != 0)(lambda: tile(False))


def _inproj(x2d, g, w, cos, sin_signed, kinds, batch):
    rows = x2d.shape[0]
    s_tiles = SEQ // IN_TM
    grid = (rows // IN_TM, IN_WIDTH // IN_TN)
    blk_per_tile = IN_TN // HEAD_DIM
    return pl.pallas_call(
        _inproj_kernel,
        out_shape=jax.ShapeDtypeStruct((batch, N_COL_BLOCKS, SEQ, HEAD_DIM), BF16),
        grid_spec=pltpu.PrefetchScalarGridSpec(
            num_scalar_prefetch=1,
            grid=grid,
            in_specs=[
                pl.BlockSpec(memory_space=pl.ANY),
                pl.BlockSpec((1, D_MODEL), lambda i, j, k: (0, 0)),
                pl.BlockSpec((D_MODEL, IN_TN), lambda i, j, k: (0, j)),
                pl.BlockSpec((IN_TM, HEAD_DIM), lambda i, j, k: (i % s_tiles, 0)),
                pl.BlockSpec((IN_TM, HEAD_DIM), lambda i, j, k: (i % s_tiles, 0)),
            ],
            out_specs=pl.BlockSpec((None, blk_per_tile, IN_TM, HEAD_DIM),
                                   lambda i, j, k: (i // s_tiles, j, i % s_tiles, 0)),
            scratch_shapes=[pltpu.VMEM((IN_TM, D_MODEL), BF16), pltpu.VMEM((IN_TM, D_MODEL), F32),
                            pltpu.SemaphoreType.DMA((IN_TM // IN_SUB_M,))],
        ),
        compiler_params=pltpu.CompilerParams(
            dimension_semantics=("arbitrary", "arbitrary"),
            vmem_limit_bytes=VMEM_LIMIT),
        name="norm_inproj",
    )(kinds, x2d, g, w, cos, sin_signed)


TQ = 256
N_QT = SEQ // TQ
PART = 256
A_PIPE_DEPTH, B_PIPE_DEPTH, C_PIPE_DEPTH = 5, 1, 2
A_INTERLEAVE, B_INTERLEAVE, C_INTERLEAVE = 1, 1, 2


ONES_ROWS = 16


def _store_vt(vt_ref, v_ref):
    dv = v_ref.shape[1]
    vt_ref[0:dv, :] = v_ref[...].astype(F32).T.astype(BF16)
    vt_ref[dv:, :] = jnp.ones((ONES_ROWS, v_ref.shape[0]), BF16)


def _softmax_pv_keymajor(parts, vt):
    m = functools.reduce(jnp.maximum, [jnp.max(s, axis=0, keepdims=True) for s in parts])
    p_t = jnp.concatenate([jnp.exp2(s - m).astype(BF16) for s in parts], axis=0)
    o_t = jnp.dot(vt, p_t, preferred_element_type=F32)
    dv = vt.shape[0] - ONES_ROWS
    return (o_t[:dv] * (1.0 / o_t[dv:dv + 1])).T


def _softmax_pv_querymajor(s, v):
    m = jnp.max(s, axis=-1, keepdims=True)
    p = jnp.exp2(s - m)
    inv_l = 1.0 / jnp.sum(p, axis=-1, keepdims=True)
    return jnp.dot(p.astype(BF16), v, preferred_element_type=F32) * inv_l


def _pipelined(units, scores, consume, depth):
    ready = [scores(u) for u in units[:depth]]
    for idx, u in enumerate(units):
        if idx + depth < len(units):
            ready.append(scores(units[idx + depth]))
        consume(u, ready.pop(0))


def _key_le_query_mask(key_axis=0):
    key = lax.broadcasted_iota(jnp.int32, (TQ, TQ), key_axis)
    query = lax.broadcasted_iota(jnp.int32, (TQ, TQ), 1 - key_axis)
    return key <= query


def _dilated_bias_table():
    key = np.arange(SEQ)[:, None]
    query = (SEQ - TQ) + np.arange(TQ)[None, :]
    d = query - key
    mult = np.zeros(d.shape, np.float64)
    for window, dil in DIL_CONFIGS:
        mult += (d >= 0) & (d % dil == 0) & (d <= window)
    return jnp.asarray(np.where(mult > 0, np.log2(np.maximum(mult, 1.0)), NEG), F32)


def _per_head(head_fn, n_shared, n_scratch, depth, interleave=1, slabs_per_head=1):
    def group_kernel(*refs):
        shared = refs[:n_shared]
        grouped = refs[n_shared:len(refs) - n_scratch]
        scratch = refs[len(refs) - n_scratch:]

        def view(r, g):
            return r.at[g] if slabs_per_head == 1 else r.at[pl.ds(g * slabs_per_head, slabs_per_head)]

        def body(g, carry):
            heads = [head_fn(*shared, *[view(r, g * interleave + h) for r in grouped],
                             *[s.at[h] for s in scratch]) for h in range(interleave)]
            n_units = len(heads[0][0])
            units = [(h, heads[h][0][idx]) for idx in range(n_units) for h in range(interleave)]
            _pipelined(units, lambda hu: heads[hu[0]][1](hu[1]),
                       lambda hu, s: heads[hu[0]][2](hu[1], s), depth * interleave)
            return carry

        lax.fori_loop(0, grouped[0].shape[0] // (slabs_per_head * interleave), body, 0)
    return group_kernel


def _heads_spec(off, n):
    assert off % n == 0
    return pl.BlockSpec((None, n, SEQ, HEAD_DIM), lambda b, h: (b, off // n + h, 0, 0))


HEADS_PER_STEP = 4


def _attn_a_kernel(bias_ref, q_ref, k_ref, v_ref, o_ref, vt_ref):
    _store_vt(vt_ref, v_ref)

    def scores(i):
        return _nt_dot(k_ref[0:(i + 1) * TQ, :], q_ref[i * TQ:(i + 1) * TQ, :])

    def consume(i, s_t):
        ext = (i + 1) * TQ
        off = SEQ - ext
        parts = [s_t[r:r + PART, :] + bias_ref[off + r:off + r + PART, :] for r in range(0, ext, PART)]
        o = _softmax_pv_keymajor(parts, vt_ref[:, 0:ext])
        o_ref[i * TQ:ext, :] = o.astype(BF16)

    return list(range(N_QT)), scores, consume


def _attn_a(h_ab, bias):
    batch, n = h_ab.shape[0], HEADS_PER_STEP
    return pl.pallas_call(
        _per_head(_attn_a_kernel, 1, 1, A_PIPE_DEPTH, interleave=A_INTERLEAVE),
        out_shape=jax.ShapeDtypeStruct((batch, A_HEADS, SEQ, HEAD_DIM), BF16),
        grid=(batch, A_HEADS // n),
        in_specs=[pl.BlockSpec((SEQ, TQ), lambda b, h: (0, 0)),
                  _heads_spec(0, n), _heads_spec(A_HEADS, n), _heads_spec(2 * A_HEADS, n)],
        out_specs=_heads_spec(0, n),
        scratch_shapes=[pltpu.VMEM((A_INTERLEAVE, HEAD_DIM + ONES_ROWS, SEQ), BF16)],
        compiler_params=pltpu.CompilerParams(
            dimension_semantics=("parallel", "parallel"),
            vmem_limit_bytes=VMEM_LIMIT),
        name="dilated_attn",
    )(bias, h_ab, h_ab, h_ab)


B_Q_OFF, B_K_OFF, B_V_OFF = 24, 32, 40


def _attn_b_kernel(lam_ref, g_ref, q_ref, k_ref, v_ref, o_ref, vcat_ref, *, lam_init):
    vcat_ref[:, :HEAD_DIM] = v_ref[0]
    vcat_ref[:, HEAD_DIM:] = v_ref[1]
    lam_v = lam_ref[...]
    lam = (jnp.exp(jnp.sum(lam_v[0:1] * lam_v[1:2], axis=-1, keepdims=True))
           - jnp.exp(jnp.sum(lam_v[2:3] * lam_v[3:4], axis=-1, keepdims=True)) + lam_init)
    query_ge_key = _key_le_query_mask(key_axis=1)
    first = {}

    def scores(u):
        i, t = u
        return _nt_dot(q_ref[t, i * TQ:(i + 1) * TQ, :], k_ref[t, 0:(i + 1) * TQ, :])

    def consume(u, s):
        i, t = u
        ext = (i + 1) * TQ
        rows = slice(i * TQ, ext)
        own = jnp.where(query_ge_key, s[:, i * TQ:], NEG)
        s = own if i == 0 else jnp.concatenate([s[:, :i * TQ], own], axis=1)
        o = _softmax_pv_querymajor(s, vcat_ref[0:ext, :])
        if t == 0:
            first[i] = o
            return
        o = first.pop(i) - lam * o
        ms = jnp.mean(o * o, axis=-1, keepdims=True)
        y = o * lax.rsqrt(ms + NORM_EPS) * g_ref[...] * (1.0 - lam_init)
        o_ref[0, rows, :] = y[:, :HEAD_DIM].astype(BF16)
        o_ref[1, rows, :] = y[:, HEAD_DIM:].astype(BF16)

    return [(i, t) for i in range(N_QT) for t in range(2)], scores, consume


def _attn_b(h_ab, lam_vec, subln_g, lam_init):
    batch = h_ab.shape[0]
    heads = HEADS_PER_STEP // 2
    spec = lambda off: _heads_spec(off, 2 * heads)
    return pl.pallas_call(
        _per_head(functools.partial(_attn_b_kernel, lam_init=lam_init), 2, 1, B_PIPE_DEPTH,
                  interleave=B_INTERLEAVE, slabs_per_head=2),
        out_shape=jax.ShapeDtypeStruct((batch, 2 * B_HEADS, SEQ, HEAD_DIM), BF16),
        grid=(batch, B_HEADS // heads),
        in_specs=[
            pl.BlockSpec((4, HEAD_DIM), lambda b, h: (0, 0)),
            pl.BlockSpec((1, B_VDIM), lambda b, h: (0, 0)),
            spec(B_Q_OFF), spec(B_K_OFF), spec(B_V_OFF),
        ],
        out_specs=spec(0),
        scratch_shapes=[pltpu.VMEM((B_INTERLEAVE, SEQ, B_VDIM), BF16)],
        compiler_params=pltpu.CompilerParams(
            dimension_semantics=("parallel", "parallel"),
            vmem_limit_bytes=VMEM_LIMIT),
        name="diff_attn",
    )(lam_vec, subln_g, h_ab, h_ab, h_ab)


KM_ROWS = 16


def _attn_c_kernel(q_ref, k_ref, v_ref, o_ref, vt_ref):
    assert TQ == MOBA_BLOCK
    key_blk = lax.broadcasted_iota(jnp.int32, (KM_ROWS, SEQ), 1) // MOBA_BLOCK
    pool = (lax.broadcasted_iota(jnp.int32, (KM_ROWS, SEQ), 0) == key_blk).astype(BF16)
    km = jnp.dot(pool, k_ref[...], preferred_element_type=F32) * (1.0 / MOBA_BLOCK)
    km_hi = km.astype(BF16)
    km_lo = (km - km_hi.astype(F32)).astype(BF16)
    km_hl = jnp.concatenate([km_hi, km_lo], axis=0)
    row = lax.broadcasted_iota(jnp.int32, (KM_ROWS, TQ), 0)
    _store_vt(vt_ref, v_ref)
    key_le_query = _key_le_query_mask()

    def scores(i):
        q = q_ref[i * TQ:(i + 1) * TQ, :]
        s_t = _nt_dot(k_ref[0:(i + 1) * TQ, :], q)
        g = _nt_dot(km_hl, q) if i > 0 else None
        return s_t, g

    def consume(i, sg):
        s_t, g = sg
        ext = (i + 1) * TQ
        parts = []
        if i > 0:
            past = row < i
            gate = jnp.where(past, g[:KM_ROWS] + g[KM_ROWS:], -jnp.inf)
            rank = jnp.zeros(gate.shape, F32)
            for j in range(i):
                gj = gate[j:j + 1, :]
                beats = (gj > gate) | ((gj == gate) & (j < row))
                rank = rank + beats.astype(F32)
            bias = jnp.where(past & (rank < MOBA_TOPK), 0.0, NEG)
            for n in range(i):
                for r in range(n * TQ, (n + 1) * TQ, PART):
                    parts.append(s_t[r:r + PART, :] + bias[n:n + 1, :])
        for r in range(0, TQ, PART):
            parts.append(jnp.where(key_le_query[r:r + PART, :], s_t[i * TQ + r:i * TQ + r + PART, :], NEG))
        o_ref[i * TQ:ext, :] = _softmax_pv_keymajor(parts, vt_ref[:, 0:ext]).astype(BF16)

    return list(range(N_QT)), scores, consume


def _attn_c(h_c):
    batch, n = h_c.shape[0], HEADS_PER_STEP
    return pl.pallas_call(
        _per_head(_attn_c_kernel, 0, 1, C_PIPE_DEPTH, interleave=C_INTERLEAVE),
        out_shape=jax.ShapeDtypeStruct((batch, C_HEADS, SEQ, HEAD_DIM), BF16),
        grid=(batch, C_HEADS // n),
        in_specs=[_heads_spec(0, n), _heads_spec(C_HEADS, n), _heads_spec(2 * C_HEADS, n)],
        out_specs=_heads_spec(0, n),
        scratch_shapes=[pltpu.VMEM((C_INTERLEAVE, HEAD_DIM + ONES_ROWS, SEQ), BF16)],
        compiler_params=pltpu.CompilerParams(
            dimension_semantics=("parallel", "parallel"),
            vmem_limit_bytes=VMEM_LIMIT),
        name="moba_attn",
    )(h_c, h_c, h_c)


OUT_TM = 512
OUT_SUB_M, OUT_SUB_N = 256, 256
Z_BLOCK_OFF = 3


def _outproj_kernel(*refs, n_y, final_norm):
    y_refs = refs[:n_y]
    z_ref, x_ref, w_hbm, g_ref, o_ref, ysc_ref, wb_ref, wstage_ref, w_sem = refs[n_y:]
    y_blocks = [(y_ref, hh) for y_ref in y_refs for hh in range(y_ref.shape[0])]
    n_col = D_MODEL // OUT_SUB_N

    def w_copy(n):
        return pltpu.make_async_copy(w_hbm.at[:, pl.ds(n * OUT_SUB_N, OUT_SUB_N)],
                                     wstage_ref.at[n % 2], w_sem.at[n % 2])

    def tile(stream_weights):
        if stream_weights:
            w_copy(0).start()
            w_copy(1).start()
        for r in range(OUT_TM // OUT_SUB_M):
            rows = slice(r * OUT_SUB_M, (r + 1) * OUT_SUB_M)
            for c, (y_ref, hh) in enumerate(y_blocks):
                z = z_ref[c, rows, :].astype(F32)
                gate = z / (1.0 + jnp.exp(-z))
                ysc_ref[rows, c * HEAD_DIM:(c + 1) * HEAD_DIM] = (
                    y_ref[hh, rows, :].astype(F32) * gate).astype(BF16)
            ssq = jnp.zeros((OUT_SUB_M, 1), F32)
            for n in range(n_col):
                cols = slice(n * OUT_SUB_N, (n + 1) * OUT_SUB_N)
                if stream_weights and r == 0:
                    w_copy(n).wait()
                    wb_ref[:, cols] = wstage_ref[n % 2].astype(BF16)
                    if n + 2 < n_col:
                        w_copy(n + 2).start()
                h = x_ref[rows, cols] + jnp.dot(ysc_ref[rows, :], wb_ref[:, cols],
                                                preferred_element_type=F32)
                o_ref[rows, cols] = h
                if final_norm:
                    ssq = ssq + jnp.sum(h * h, axis=-1, keepdims=True)
            if final_norm:
                inv = lax.rsqrt(ssq * (1.0 / D_MODEL) + NORM_EPS)
                o_ref[rows, :] = o_ref[rows, :] * inv * g_ref[...]

    pl.when(pl.program_id(0) == 0)(lambda: tile(True))
    pl.when(pl.program_id(0) != 0)(lambda: tile(False))


def _outproj(ys, h_in, x2d, w, g_final, batch, final_norm):
    rows = x2d.shape[0]
    s_tiles = SEQ // OUT_TM
    n_z = D_MODEL // HEAD_DIM
    y_specs = [pl.BlockSpec((None, y.shape[1], OUT_TM, HEAD_DIM),
                            lambda i: (i // s_tiles, 0, i % s_tiles, 0)) for y in ys]
    assert sum(y.shape[1] for y in ys) == n_z
    return pl.pallas_call(
        functools.partial(_outproj_kernel, n_y=len(ys), final_norm=final_norm),
        out_shape=jax.ShapeDtypeStruct((rows, D_MODEL), F32),
        grid=(rows // OUT_TM,),
        in_specs=y_specs + [
            pl.BlockSpec((None, n_z, OUT_TM, HEAD_DIM),
                         lambda i: (i // s_tiles, Z_BLOCK_OFF, i % s_tiles, 0)),
            pl.BlockSpec((OUT_TM, D_MODEL), lambda i: (i, 0)),
            pl.BlockSpec(memory_space=pl.ANY),
            pl.BlockSpec((1, D_MODEL), lambda i: (0, 0)),
        ],
        out_specs=pl.BlockSpec((OUT_TM, D_MODEL), lambda i: (i, 0)),
        scratch_shapes=[pltpu.VMEM((OUT_TM, D_MODEL), BF16), pltpu.VMEM((D_MODEL, D_MODEL), BF16),
                        pltpu.VMEM((2, D_MODEL, OUT_SUB_N), F32), pltpu.SemaphoreType.DMA((2,))],
        compiler_params=pltpu.CompilerParams(
            dimension_semantics=("arbitrary",),
            vmem_limit_bytes=VMEM_LIMIT),
        name="gate_outproj",
    )(*ys, h_in, x2d, w, g_final)


def _rope_tables():
    inv = 1.0 / (ROPE_THETA ** (np.arange(0, HEAD_DIM, 2, dtype=np.float64) / HEAD_DIM))
    ang = np.arange(SEQ, dtype=np.float64)[:, None] * inv[None, :]
    cos = np.concatenate([np.cos(ang), np.cos(ang)], axis=-1)
    sin = np.concatenate([-np.sin(ang), np.sin(ang)], axis=-1)
    return jnp.asarray(cos, F32), jnp.asarray(sin, F32)


def _tile_kinds(group_kinds, cols_per_group):
    kinds = []
    for kind, cols in zip(group_kinds, cols_per_group):
        assert cols % IN_TN == 0
        kinds += [kind] * (cols // IN_TN)
    assert len(kinds) == IN_WIDTH // IN_TN
    return jnp.asarray(kinds, jnp.int32)


def kernel(x, norm_ab, w_in_ab, w_out_ab, lam_ab, subln_ab, norm_c, w_in_c, w_out_c, final_norm):
    batch, seq, d = x.shape
    assert (seq, d) == (SEQ, D_MODEL) and x.dtype == F32
    assert w_in_ab.shape == (1, D_MODEL, IN_WIDTH) and w_in_c.shape == (1, D_MODEL, IN_WIDTH)
    cos, sin = _rope_tables()
    aw, bw = A_HEADS * HEAD_DIM, B_HEADS * B_VDIM
    kinds_ab = _tile_kinds(
        [KIND_ROPE_SCALED, KIND_ROPE, KIND_PLAIN, KIND_ROPE_SCALED, KIND_ROPE, KIND_PLAIN, KIND_PLAIN],
        [aw, aw, aw, bw, bw, bw, aw + bw])
    cw = C_HEADS * HEAD_DIM
    kinds_c = _tile_kinds([KIND_ROPE_SCALED, KIND_ROPE, KIND_PLAIN, KIND_PLAIN], [cw, cw, cw, cw])

    x2d = x.reshape(batch * seq, d)

    g_final = final_norm[None, :]
    h_ab = _inproj(x2d, norm_ab[0][None, :], w_in_ab[0], cos, sin, kinds_ab, batch)
    ya = _attn_a(h_ab, _dilated_bias_table())
    lam_init = 0.8 - 0.6 * math.exp(-0.3 * 0)
    yb = _attn_b(h_ab, lam_ab[0], subln_ab[0][None, :], lam_init)
    x1 = _outproj([ya, yb], h_ab, x2d, w_out_ab[0], g_final, batch, final_norm=False)

    h_c = _inproj(x1, norm_c[0][None, :], w_in_c[0], cos, sin, kinds_c, batch)
    yc = _attn_c(h_c)
    out = _outproj([yc], h_c, x1, w_out_c[0], g_final, batch, final_norm=True)
    return out.reshape(batch, seq, d)
```

```python
import functools
import math

import numpy as np
import jax
import jax.numpy as jnp
from jax import lax
from jax.experimental import pallas as pl
from jax.experimental.pallas import tpu as pltpu

D_MODEL = 2048
SEQ = 2048
HEAD_DIM = 128
ROPE_THETA = 10000.0
NORM_EPS = 1e-6
A_HEADS = 8
DIL_CONFIGS = ((128, 1), (512, 4), (2048, 16))
B_HEADS = 4
B_VDIM = 2 * HEAD_DIM
C_HEADS = 16
MOBA_BLOCK = 256
MOBA_TOPK = 3
N_MOBA_BLOCKS = SEQ // MOBA_BLOCK
IN_WIDTH = 8192
N_COL_BLOCKS = IN_WIDTH // HEAD_DIM
SCALE = HEAD_DIM ** -0.5
Q_SCALE = SCALE * math.log2(math.e)
NEG = -1e30

BF16 = jnp.bfloat16
F32 = jnp.float32

KIND_PLAIN, KIND_ROPE, KIND_ROPE_SCALED = 0, 1, 2

VMEM_LIMIT = 56 * 1024 * 1024


def _nt_dot(a, b):
    return lax.dot_general(a, b, (((1,), (1,)), ((), ())), preferred_element_type=F32)


IN_TM, IN_TN = 2048, 1024
IN_SUB_M, IN_SUB_N = 256, 256


def _inproj_kernel(kinds_ref, x_hbm, g_ref, w_ref, cos_ref, sin_ref, o_ref, xn_ref, x_ref, x_sem):
    i, j = pl.program_id(0), pl.program_id(1)

    def x_copy(row_tile):
        return pltpu.make_async_copy(x_hbm.at[pl.ds(row_tile * IN_TM, IN_TM), :], x_ref, x_sem)

    @pl.when((i == 0) & (j == 0))
    def _():
        x_copy(0).start()

    @pl.when((j == 1) & (i + 1 < pl.num_programs(0)))
    def _():
        x_copy(i + 1).start()

    kind = kinds_ref[j]
    scale = jnp.where(kind == KIND_ROPE_SCALED, Q_SCALE, 1.0).astype(F32)

    def tile(normalise):
        for nh in range(IN_TN // IN_SUB_N):
            w = w_ref[:, nh * IN_SUB_N:(nh + 1) * IN_SUB_N].astype(BF16)
            for mh in range(IN_TM // IN_SUB_M):
                rows = slice(mh * IN_SUB_M, (mh + 1) * IN_SUB_M)
                if normalise and nh == 0:
                    x = x_ref[rows, :]
                    ms = jnp.mean(x * x, axis=-1, keepdims=True)
                    xn_ref[rows, :] = (x * lax.rsqrt(ms + NORM_EPS) * g_ref[...]).astype(BF16)
                acc = jnp.dot(xn_ref[rows, :], w, preferred_element_type=F32)
                cos = jnp.where(kind == KIND_PLAIN, 1.0, cos_ref[rows, :] * scale)
                sin = jnp.where(kind == KIND_PLAIN, 0.0, sin_ref[rows, :] * scale)
                for c in range(IN_SUB_N // HEAD_DIM):
                    t = acc[:, c * HEAD_DIM:(c + 1) * HEAD_DIM]
                    o_ref[nh * (IN_SUB_N // HEAD_DIM) + c, rows, :] = (
                        t * cos + pltpu.roll(t, HEAD_DIM // 2, 1) * sin).astype(BF16)

    @pl.when(j == 0)
    def _():
        x_copy(i).wait()
        tile(True)

    pl.when(j != 0)(lambda: tile(False))


def _inproj(x2d, g, w, cos, sin_signed, kinds, batch):
    rows = x2d.shape[0]
    s_tiles = SEQ // IN_TM
    grid = (rows // IN_TM, IN_WIDTH // IN_TN)
    blk_per_tile = IN_TN // HEAD_DIM
    return pl.pallas_call(
        _inproj_kernel,
        out_shape=jax.ShapeDtypeStruct((batch, N_COL_BLOCKS, SEQ, HEAD_DIM), BF16),
        grid_spec=pltpu.PrefetchScalarGridSpec(
            num_scalar_prefetch=1,
            grid=grid,
            in_specs=[
                pl.BlockSpec(memory_space=pl.ANY),
                pl.BlockSpec((1, D_MODEL), lambda i, j, k: (0, 0)),
                pl.BlockSpec((D_MODEL, IN_TN), lambda i, j, k: (0, j)),
                pl.BlockSpec((IN_TM, HEAD_DIM), lambda i, j, k: (i % s_tiles, 0)),
                pl.BlockSpec((IN_TM, HEAD_DIM), lambda i, j, k: (i % s_tiles, 0)),
            ],
            out_specs=pl.BlockSpec((None, blk_per_tile, IN_TM, HEAD_DIM),
                                   lambda i, j, k: (i // s_tiles, j, i % s_tiles, 0)),
            scratch_shapes=[pltpu.VMEM((IN_TM, D_MODEL), BF16), pltpu.VMEM((IN_TM, D_MODEL), F32),
                            pltpu.SemaphoreType.DMA],
        ),
        compiler_params=pltpu.CompilerParams(
            dimension_semantics=("arbitrary", "arbitrary"),
            vmem_limit_bytes=VMEM_LIMIT),
        name="norm_inproj",
    )(kinds, x2d, g, w, cos, sin_signed)


TQ = 256
N_QT = SEQ // TQ
PART = 256
A_PIPE_DEPTH, B_PIPE_DEPTH, C_PIPE_DEPTH = 5, 2, 2
A_INTERLEAVE, B_INTERLEAVE, C_INTERLEAVE = 1, 1, 2


ONES_ROWS = 16


def _store_vt(vt_ref, v_ref):
    dv = v_ref.shape[1]
    vt_ref[0:dv, :] = v_ref[...].astype(F32).T.astype(BF16)
    vt_ref[dv:, :] = jnp.ones((ONES_ROWS, v_ref.shape[0]), BF16)


def _with_max(parts):
    return parts, functools.reduce(jnp.maximum, [jnp.max(s, axis=0, keepdims=True) for s in parts])


def _softmax_pv_keymajor(parts_and_max, vt):
    parts, m = parts_and_max
    p_t = jnp.concatenate([jnp.exp2(s - m).astype(BF16) for s in parts], axis=0)
    o_t = jnp.dot(vt, p_t, preferred_element_type=F32)
    dv = vt.shape[0] - ONES_ROWS
    return (o_t[:dv] * (1.0 / o_t[dv:dv + 1])).T


def _pipelined(units, scores, consume, depth):
    ready = [scores(u) for u in units[:depth]]
    for idx, u in enumerate(units):
        if idx + depth < len(units):
            ready.append(scores(units[idx + depth]))
        consume(u, ready.pop(0))


def _key_le_query_mask(key_axis=0):
    key = lax.broadcasted_iota(jnp.int32, (TQ, TQ), key_axis)
    query = lax.broadcasted_iota(jnp.int32, (TQ, TQ), 1 - key_axis)
    return key <= query


def _dilated_bias_table():
    key = np.arange(SEQ)[:, None]
    query = (SEQ - TQ) + np.arange(TQ)[None, :]
    d = query - key
    mult = np.zeros(d.shape, np.float64)
    for window, dil in DIL_CONFIGS:
        mult += (d >= 0) & (d % dil == 0) & (d <= window)
    return jnp.asarray(np.where(mult > 0, np.log2(np.maximum(mult, 1.0)), NEG), F32)


def _per_head(head_fn, n_shared, n_scratch, depth, interleave=1, slabs_per_head=1):
    def group_kernel(*refs):
        shared = refs[:n_shared]
        grouped = refs[n_shared:len(refs) - n_scratch]
        scratch = refs[len(refs) - n_scratch:]

        def view(r, g):
            return r.at[g] if slabs_per_head == 1 else r.at[pl.ds(g * slabs_per_head, slabs_per_head)]

        def body(g, carry):
            heads = [head_fn(*shared, *[view(r, g * interleave + h) for r in grouped],
                             *[s.at[h] for s in scratch]) for h in range(interleave)]
            n_units = len(heads[0][0])
            units = [(h, heads[h][0][idx]) for idx in range(n_units) for h in range(interleave)]
            _pipelined(units, lambda hu: heads[hu[0]][1](hu[1]),
                       lambda hu, s: heads[hu[0]][2](hu[1], s), depth * interleave)
            return carry

        lax.fori_loop(0, grouped[0].shape[0] // (slabs_per_head * interleave), body, 0)
    return group_kernel


def _heads_spec(off, n):
    assert off % n == 0
    return pl.BlockSpec((None, n, SEQ, HEAD_DIM), lambda b, h: (b, off // n + h, 0, 0))


HEADS_PER_STEP = 4


def _attn_a_kernel(bias_ref, q_ref, k_ref, v_ref, o_ref, vt_ref):
    _store_vt(vt_ref, v_ref)

    def scores(i):
        return _nt_dot(k_ref[0:(i + 1) * TQ, :], q_ref[i * TQ:(i + 1) * TQ, :])

    def consume(i, s_t):
        ext = (i + 1) * TQ
        off = SEQ - ext
        parts = [s_t[r:r + PART, :] + bias_ref[off + r:off + r + PART, :] for r in range(0, ext, PART)]
        o = _softmax_pv_keymajor(_with_max(parts), vt_ref[:, 0:ext])
        o_ref[i * TQ:ext, :] = o.astype(BF16)

    return list(range(N_QT)), scores, consume


def _attn_a(h_ab, bias):
    batch, n = h_ab.shape[0], HEADS_PER_STEP
    return pl.pallas_call(
        _per_head(_attn_a_kernel, 1, 1, A_PIPE_DEPTH, interleave=A_INTERLEAVE),
        out_shape=jax.ShapeDtypeStruct((batch, A_HEADS, SEQ, HEAD_DIM), BF16),
        grid=(batch, A_HEADS // n),
        in_specs=[pl.BlockSpec((SEQ, TQ), lambda b, h: (0, 0)),
                  _heads_spec(0, n), _heads_spec(A_HEADS, n), _heads_spec(2 * A_HEADS, n)],
        out_specs=_heads_spec(0, n),
        scratch_shapes=[pltpu.VMEM((A_INTERLEAVE, HEAD_DIM + ONES_ROWS, SEQ), BF16)],
        compiler_params=pltpu.CompilerParams(
            dimension_semantics=("parallel", "parallel"),
            vmem_limit_bytes=VMEM_LIMIT),
        name="dilated_attn",
    )(bias, h_ab, h_ab, h_ab)


B_Q_OFF, B_K_OFF, B_V_OFF = 24, 32, 40


def _attn_b_kernel(lam_ref, g_ref, q_ref, k_ref, v_ref, o_ref, vcat_ref, *, lam_init):
    vcat_ref[:, :HEAD_DIM] = v_ref[0]
    vcat_ref[:, HEAD_DIM:] = v_ref[1]
    lam_v = lam_ref[...]
    lam = (jnp.exp(jnp.sum(lam_v[0:1] * lam_v[1:2], axis=-1, keepdims=True))
           - jnp.exp(jnp.sum(lam_v[2:3] * lam_v[3:4], axis=-1, keepdims=True)) + lam_init)
    query_ge_key = _key_le_query_mask(key_axis=1)

    def scores(i):
        return [_nt_dot(q_ref[t, i * TQ:(i + 1) * TQ, :], k_ref[t, 0:(i + 1) * TQ, :])
                for t in range(2)]

    def consume(i, s12):
        ext = (i + 1) * TQ
        rows = slice(i * TQ, ext)
        ps, ls = [], []
        for s in s12:
            own = jnp.where(query_ge_key, s[:, i * TQ:], NEG)
            s = own if i == 0 else jnp.concatenate([s[:, :i * TQ], own], axis=1)
            m = jnp.max(s, axis=-1, keepdims=True)
            ps.append(jnp.exp2(s - m))
            ls.append(jnp.sum(ps[-1], axis=-1, keepdims=True))
        d = ps[0] - ps[1] * (lam * ls[0] / ls[1])
        o = jnp.dot(d.astype(BF16), vcat_ref[0:ext, :], preferred_element_type=F32) * (1.0 / ls[0])
        ms = jnp.mean(o * o, axis=-1, keepdims=True)
        y = o * lax.rsqrt(ms + NORM_EPS) * g_ref[...] * (1.0 - lam_init)
        o_ref[0, rows, :] = y[:, :HEAD_DIM].astype(BF16)
        o_ref[1, rows, :] = y[:, HEAD_DIM:].astype(BF16)

    return list(range(N_QT)), scores, consume


def _attn_b(h_ab, lam_vec, subln_g, lam_init):
    batch = h_ab.shape[0]
    heads = HEADS_PER_STEP // 2
    spec = lambda off: _heads_spec(off, 2 * heads)
    return pl.pallas_call(
        _per_head(functools.partial(_attn_b_kernel, lam_init=lam_init), 2, 1, B_PIPE_DEPTH,
                  interleave=B_INTERLEAVE, slabs_per_head=2),
        out_shape=jax.ShapeDtypeStruct((batch, 2 * B_HEADS, SEQ, HEAD_DIM), BF16),
        grid=(batch, B_HEADS // heads),
        in_specs=[
            pl.BlockSpec((4, HEAD_DIM), lambda b, h: (0, 0)),
            pl.BlockSpec((1, B_VDIM), lambda b, h: (0, 0)),
            spec(B_Q_OFF), spec(B_K_OFF), spec(B_V_OFF),
        ],
        out_specs=spec(0),
        scratch_shapes=[pltpu.VMEM((B_INTERLEAVE, SEQ, B_VDIM), BF16)],
        compiler_params=pltpu.CompilerParams(
            dimension_semantics=("parallel", "parallel"),
            vmem_limit_bytes=VMEM_LIMIT),
        name="diff_attn",
    )(lam_vec, subln_g, h_ab, h_ab, h_ab)


KM_ROWS = 16


def _attn_c_kernel(q_ref, k_ref, v_ref, o_ref, vt_ref):
    assert TQ == MOBA_BLOCK
    key_blk = lax.broadcasted_iota(jnp.int32, (KM_ROWS, SEQ), 1) // MOBA_BLOCK
    pool = (lax.broadcasted_iota(jnp.int32, (KM_ROWS, SEQ), 0) == key_blk).astype(BF16)
    km = jnp.dot(pool, k_ref[...], preferred_element_type=F32) * (1.0 / MOBA_BLOCK)
    km_hi = km.astype(BF16)
    km_lo = (km - km_hi.astype(F32)).astype(BF16)
    km_hl = jnp.concatenate([km_hi, km_lo], axis=0)
    row = lax.broadcasted_iota(jnp.int32, (KM_ROWS, TQ), 0)
    _store_vt(vt_ref, v_ref)
    key_le_query = _key_le_query_mask()

    def scores(i):
        q = q_ref[i * TQ:(i + 1) * TQ, :]
        g = _nt_dot(km_hl, q) if i > 0 else None
        s_t = _nt_dot(k_ref[0:(i + 1) * TQ, :], q)
        parts = []
        if i > 0:
            past = row < i
            gate = jnp.where(past, g[:KM_ROWS] + g[KM_ROWS:], -jnp.inf)
            rank = jnp.zeros(gate.shape, F32)
            for j in range(i):
                gj = gate[j:j + 1, :]
                beats = (gj > gate) | ((gj == gate) & (j < row))
                rank = rank + beats.astype(F32)
            bias = jnp.where(past & (rank < MOBA_TOPK), 0.0, NEG)
            for n in range(i):
                for r in range(n * TQ, (n + 1) * TQ, PART):
                    parts.append(s_t[r:r + PART, :] + bias[n:n + 1, :])
        for r in range(0, TQ, PART):
            parts.append(jnp.where(key_le_query[r:r + PART, :], s_t[i * TQ + r:i * TQ + r + PART, :], NEG))
        return _with_max(parts)

    def consume(i, parts_and_max):
        ext = (i + 1) * TQ
        o_ref[i * TQ:ext, :] = _softmax_pv_keymajor(parts_and_max, vt_ref[:, 0:ext]).astype(BF16)

    return list(range(N_QT)), scores, consume


def _attn_c(h_c):
    batch, n = h_c.shape[0], HEADS_PER_STEP
    return pl.pallas_call(
        _per_head(_attn_c_kernel, 0, 1, C_PIPE_DEPTH, interleave=C_INTERLEAVE),
        out_shape=jax.ShapeDtypeStruct((batch, C_HEADS, SEQ, HEAD_DIM), BF16),
        grid=(batch, C_HEADS // n),
        in_specs=[_heads_spec(0, n), _heads_spec(C_HEADS, n), _heads_spec(2 * C_HEADS, n)],
        out_specs=_heads_spec(0, n),
        scratch_shapes=[pltpu.VMEM((C_INTERLEAVE, HEAD_DIM + ONES_ROWS, SEQ), BF16)],
        compiler_params=pltpu.CompilerParams(
            dimension_semantics=("parallel", "parallel"),
            vmem_limit_bytes=VMEM_LIMIT),
        name="moba_attn",
    )(h_c, h_c, h_c)


OUT_TM = 512
OUT_SUB_M, OUT_SUB_N = 128, 256
Z_BLOCK_OFF = 3


def _outproj_kernel(*refs, n_y, final_norm):
    y_refs = refs[:n_y]
    z_ref, x_ref, w_ref, g_ref, o_ref, ysc_ref, wb_ref = refs[n_y:]
    y_blocks = [(y_ref, hh) for y_ref in y_refs for hh in range(y_ref.shape[0])]

    @pl.when(pl.program_id(0) == 0)
    def _():
        wb_ref[...] = w_ref[...].astype(BF16)
    for r in range(OUT_TM // OUT_SUB_M):
        rows = slice(r * OUT_SUB_M, (r + 1) * OUT_SUB_M)
        for c, (y_ref, hh) in enumerate(y_blocks):
            z = z_ref[c, rows, :].astype(F32)
            gate = z / (1.0 + jnp.exp(-z))
            ysc_ref[rows, c * HEAD_DIM:(c + 1) * HEAD_DIM] = (
                y_ref[hh, rows, :].astype(F32) * gate).astype(BF16)
        ssq = jnp.zeros((OUT_SUB_M, 1), F32)
        for n in range(D_MODEL // OUT_SUB_N):
            cols = slice(n * OUT_SUB_N, (n + 1) * OUT_SUB_N)
            h = x_ref[rows, cols] + jnp.dot(ysc_ref[rows, :], wb_ref[:, cols],
                                            preferred_element_type=F32)
            o_ref[rows, cols] = h
            if final_norm:
                ssq = ssq + jnp.sum(h * h, axis=-1, keepdims=True)
        if final_norm:
            inv = lax.rsqrt(ssq * (1.0 / D_MODEL) + NORM_EPS)
            o_ref[rows, :] = o_ref[rows, :] * inv * g_ref[...]


def _outproj(ys, h_in, x2d, w, g_final, batch, final_norm):
    rows = x2d.shape[0]
    s_tiles = SEQ // OUT_TM
    n_z = D_MODEL // HEAD_DIM
    y_specs = [pl.BlockSpec((None, y.shape[1], OUT_TM, HEAD_DIM),
                            lambda i: (i // s_tiles, 0, i % s_tiles, 0)) for y in ys]
    assert sum(y.shape[1] for y in ys) == n_z
    return pl.pallas_call(
        functools.partial(_outproj_kernel, n_y=len(ys), final_norm=final_norm),
        out_shape=jax.ShapeDtypeStruct((rows, D_MODEL), F32),
        grid=(rows // OUT_TM,),
        in_specs=y_specs + [
            pl.BlockSpec((None, n_z, OUT_TM, HEAD_DIM),
                         lambda i: (i // s_tiles, Z_BLOCK_OFF, i % s_tiles, 0)),
            pl.BlockSpec((OUT_TM, D_MODEL), lambda i: (i, 0)),
            pl.BlockSpec((D_MODEL, D_MODEL), lambda i: (0, 0), pipeline_mode=pl.Buffered(1)),
            pl.BlockSpec((1, D_MODEL), lambda i: (0, 0)),
        ],
        out_specs=pl.BlockSpec((OUT_TM, D_MODEL), lambda i: (i, 0)),
        scratch_shapes=[pltpu.VMEM((OUT_TM, D_MODEL), BF16), pltpu.VMEM((D_MODEL, D_MODEL), BF16)],
        compiler_params=pltpu.CompilerParams(
            dimension_semantics=("arbitrary",),
            vmem_limit_bytes=VMEM_LIMIT),
        name="gate_outproj",
    )(*ys, h_in, x2d, w, g_final)


def _rope_tables():
    inv = 1.0 / (ROPE_THETA ** (np.arange(0, HEAD_DIM, 2, dtype=np.float64) / HEAD_DIM))
    ang = np.arange(SEQ, dtype=np.float64)[:, None] * inv[None, :]
    cos = np.concatenate([np.cos(ang), np.cos(ang)], axis=-1)
    sin = np.concatenate([-np.sin(ang), np.sin(ang)], axis=-1)
    return jnp.asarray(cos, F32), jnp.asarray(sin, F32)


def _tile_kinds(group_kinds, cols_per_group):
    kinds = []
    for kind, cols in zip(group_kinds, cols_per_group):
        assert cols % IN_TN == 0
        kinds += [kind] * (cols // IN_TN)
    assert len(kinds) == IN_WIDTH // IN_TN
    return jnp.asarray(kinds, jnp.int32)


def kernel(x, norm_ab, w_in_ab, w_out_ab, lam_ab, subln_ab, norm_c, w_in_c, w_out_c, final_norm):
    batch, seq, d = x.shape
    assert (seq, d) == (SEQ, D_MODEL) and x.dtype == F32
    assert w_in_ab.shape == (1, D_MODEL, IN_WIDTH) and w_in_c.shape == (1, D_MODEL, IN_WIDTH)
    cos, sin = _rope_tables()
    aw, bw = A_HEADS * HEAD_DIM, B_HEADS * B_VDIM
    kinds_ab = _tile_kinds(
        [KIND_ROPE_SCALED, KIND_ROPE, KIND_PLAIN, KIND_ROPE_SCALED, KIND_ROPE, KIND_PLAIN, KIND_PLAIN],
        [aw, aw, aw, bw, bw, bw, aw + bw])
    cw = C_HEADS * HEAD_DIM
    kinds_c = _tile_kinds([KIND_ROPE_SCALED, KIND_ROPE, KIND_PLAIN, KIND_PLAIN], [cw, cw, cw, cw])

    x2d = x.reshape(batch * seq, d)

    g_final = final_norm[None, :]
    h_ab = _inproj(x2d, norm_ab[0][None, :], w_in_ab[0], cos, sin, kinds_ab, batch)
    ya = _attn_a(h_ab, _dilated_bias_table())
    lam_init = 0.8 - 0.6 * math.exp(-0.3 * 0)
    yb = _attn_b(h_ab, lam_ab[0], subln_ab[0][None, :], lam_init)
    x1 = _outproj([ya, yb], h_ab, x2d, w_out_ab[0], g_final, batch, final_norm=False)

    h_c = _inproj(x1, norm_c[0][None, :], w_in_c[0], cos, sin, kinds_c, batch)
    yc = _attn_c(h_c)
    out = _outproj([yc], h_c, x1, w_out_c[0], g_final, batch, final_norm=True)
    return out.reshape(batch, seq, d)
```

```python
import functools
import math

import numpy as np
import jax
import jax.numpy as jnp
from jax import lax
from jax.experimental import pallas as pl
from jax.experimental.pallas import tpu as pltpu

D_MODEL = 2048
SEQ = 2048
HEAD_DIM = 128
ROPE_THETA = 10000.0
NORM_EPS = 1e-6
A_HEADS = 8
DIL_CONFIGS = ((128, 1), (512, 4), (2048, 16))
B_HEADS = 4
B_VDIM = 2 * HEAD_DIM
C_HEADS = 16
MOBA_BLOCK = 256
MOBA_TOPK = 3
N_MOBA_BLOCKS = SEQ // MOBA_BLOCK
IN_WIDTH = 8192
N_COL_BLOCKS = IN_WIDTH // HEAD_DIM
SCALE = HEAD_DIM ** -0.5
Q_SCALE = SCALE * math.log2(math.e)
NEG = -1e30

BF16 = jnp.bfloat16
F32 = jnp.float32

KIND_PLAIN, KIND_ROPE, KIND_ROPE_SCALED = 0, 1, 2

VMEM_LIMIT = 56 * 1024 * 1024


def _nt_dot(a, b):
    return lax.dot_general(a, b, (((1,), (1,)), ((), ())), preferred_element_type=F32)


IN_TM, IN_TN = 2048, 1024
IN_SUB_M, IN_SUB_N = 512, 256


def _inproj_kernel(kinds_ref, x_hbm, g_ref, w_ref, cos_ref, sin_ref, o_ref, xn_ref, x_ref, x_sem):
    i, j = pl.program_id(0), pl.program_id(1)

    def x_copy(row_tile):
        return pltpu.make_async_copy(x_hbm.at[pl.ds(row_tile * IN_TM, IN_TM), :], x_ref, x_sem)

    @pl.when((i == 0) & (j == 0))
    def _():
        x_copy(0).start()

    @pl.when((j == 1) & (i + 1 < pl.num_programs(0)))
    def _():
        x_copy(i + 1).start()

    kind = kinds_ref[j]
    scale = jnp.where(kind == KIND_ROPE_SCALED, Q_SCALE, 1.0).astype(F32)

    def tile(normalise):
        for nh in range(IN_TN // IN_SUB_N):
            w = w_ref[:, nh * IN_SUB_N:(nh + 1) * IN_SUB_N].astype(BF16)
            for mh in range(IN_TM // IN_SUB_M):
                rows = slice(mh * IN_SUB_M, (mh + 1) * IN_SUB_M)
                if normalise and nh == 0:
                    x = x_ref[rows, :]
                    ms = jnp.mean(x * x, axis=-1, keepdims=True)
                    xn_ref[rows, :] = (x * lax.rsqrt(ms + NORM_EPS) * g_ref[...]).astype(BF16)
                acc = jnp.dot(xn_ref[rows, :], w, preferred_element_type=F32)
                cos = jnp.where(kind == KIND_PLAIN, 1.0, cos_ref[rows, :] * scale)
                sin = jnp.where(kind == KIND_PLAIN, 0.0, sin_ref[rows, :] * scale)
                for c in range(IN_SUB_N // HEAD_DIM):
                    t = acc[:, c * HEAD_DIM:(c + 1) * HEAD_DIM]
                    o_ref[nh * (IN_SUB_N // HEAD_DIM) + c, rows, :] = (
                        t * cos + pltpu.roll(t, HEAD_DIM // 2, 1) * sin).astype(BF16)

    @pl.when(j == 0)
    def _():
        x_copy(i).wait()
        tile(True)

    pl.when(j != 0)(lambda: tile(False))


def _inproj(x2d, g, w, cos, sin_signed, kinds, batch):
    rows = x2d.shape[0]
    s_tiles = SEQ // IN_TM
    grid = (rows // IN_TM, IN_WIDTH // IN_TN)
    blk_per_tile = IN_TN // HEAD_DIM
    return pl.pallas_call(
        _inproj_kernel,
        out_shape=jax.ShapeDtypeStruct((batch, N_COL_BLOCKS, SEQ, HEAD_DIM), BF16),
        grid_spec=pltpu.PrefetchScalarGridSpec(
            num_scalar_prefetch=1,
            grid=grid,
            in_specs=[
                pl.BlockSpec(memory_space=pl.ANY),
                pl.BlockSpec((1, D_MODEL), lambda i, j, k: (0, 0)),
                pl.BlockSpec((D_MODEL, IN_TN), lambda i, j, k: (0, j)),
                pl.BlockSpec((IN_TM, HEAD_DIM), lambda i, j, k: (i % s_tiles, 0)),
                pl.BlockSpec((IN_TM, HEAD_DIM), lambda i, j, k: (i % s_tiles, 0)),
            ],
            out_specs=pl.BlockSpec((None, blk_per_tile, IN_TM, HEAD_DIM),
                                   lambda i, j, k: (i // s_tiles, j, i % s_tiles, 0)),
            scratch_shapes=[pltpu.VMEM((IN_TM, D_MODEL), BF16), pltpu.VMEM((IN_TM, D_MODEL), F32),
                            pltpu.SemaphoreType.DMA],
        ),
        compiler_params=pltpu.CompilerParams(
            dimension_semantics=("arbitrary", "arbitrary"),
            vmem_limit_bytes=VMEM_LIMIT),
        name="norm_inproj",
    )(kinds, x2d, g, w, cos, sin_signed)


TQ = 256
N_QT = SEQ // TQ
PART = 128
A_PIPE_DEPTH, B_PIPE_DEPTH, C_PIPE_DEPTH = 5, 3, 3
A_INTERLEAVE, B_INTERLEAVE, C_INTERLEAVE = 1, 1, 2


ONES_ROWS = 16


def _store_vt(vt_ref, v_ref):
    dv = v_ref.shape[1]
    vt_ref[0:dv, :] = v_ref[...].astype(F32).T.astype(BF16)
    vt_ref[dv:, :] = jnp.ones((ONES_ROWS, v_ref.shape[0]), BF16)


def _with_max(parts):
    return parts, functools.reduce(jnp.maximum, [jnp.max(s, axis=0, keepdims=True) for s in parts])


def _softmax_pv_keymajor(parts_and_max, vt):
    parts, m = parts_and_max
    p_t = jnp.concatenate([jnp.exp2(s - m).astype(BF16) for s in parts], axis=0)
    o_t = jnp.dot(vt, p_t, preferred_element_type=F32)
    dv = vt.shape[0] - ONES_ROWS
    return (o_t[:dv] * (1.0 / o_t[dv:dv + 1])).T


def _pipelined(units, scores, consume, depth):
    ready = [scores(u) for u in units[:depth]]
    for idx, u in enumerate(units):
        if idx + depth < len(units):
            ready.append(scores(units[idx + depth]))
        consume(u, ready.pop(0))


def _key_le_query_mask(key_axis=0):
    key = lax.broadcasted_iota(jnp.int32, (TQ, TQ), key_axis)
    query = lax.broadcasted_iota(jnp.int32, (TQ, TQ), 1 - key_axis)
    return key <= query


def _dilated_bias_table():
    key = np.arange(SEQ)[:, None]
    query = (SEQ - TQ) + np.arange(TQ)[None, :]
    d = query - key
    mult = np.zeros(d.shape, np.float64)
    for window, dil in DIL_CONFIGS:
        mult += (d >= 0) & (d % dil == 0) & (d <= window)
    return jnp.asarray(np.where(mult > 0, np.log2(np.maximum(mult, 1.0)), NEG), F32)


def _per_head(head_fn, n_shared, n_scratch, depth, interleave=1, slabs_per_head=1):
    def group_kernel(*refs):
        shared = refs[:n_shared]
        grouped = refs[n_shared:len(refs) - n_scratch]
        scratch = refs[len(refs) - n_scratch:]

        def view(r, g):
            return r.at[g] if slabs_per_head == 1 else r.at[pl.ds(g * slabs_per_head, slabs_per_head)]

        def body(g, carry):
            heads = [head_fn(*shared, *[view(r, g * interleave + h) for r in grouped],
                             *[s.at[h] for s in scratch]) for h in range(interleave)]
            n_units = len(heads[0][0])
            units = [(h, heads[h][0][idx]) for idx in range(n_units) for h in range(interleave)]
            _pipelined(units, lambda hu: heads[hu[0]][1](hu[1]),
                       lambda hu, s: heads[hu[0]][2](hu[1], s), depth * interleave)
            return carry

        lax.fori_loop(0, grouped[0].shape[0] // (slabs_per_head * interleave), body, 0)
    return group_kernel


def _heads_spec(off, n):
    assert off % n == 0
    return pl.BlockSpec((None, n, SEQ, HEAD_DIM), lambda b, h: (b, off // n + h, 0, 0))


HEADS_PER_STEP = 4


def _attn_a_kernel(bias_ref, q_ref, k_ref, v_ref, o_ref, vt_ref):
    _store_vt(vt_ref, v_ref)

    def scores(i):
        return _nt_dot(k_ref[0:(i + 1) * TQ, :], q_ref[i * TQ:(i + 1) * TQ, :])

    def consume(i, s_t):
        ext = (i + 1) * TQ
        off = SEQ - ext
        parts = [s_t[r:r + PART, :] + bias_ref[off + r:off + r + PART, :] for r in range(0, ext, PART)]
        o = _softmax_pv_keymajor(_with_max(parts), vt_ref[:, 0:ext])
        o_ref[i * TQ:ext, :] = o.astype(BF16)

    return list(range(N_QT)), scores, consume


def _attn_a(h_ab, bias):
    batch, n = h_ab.shape[0], HEADS_PER_STEP
    return pl.pallas_call(
        _per_head(_attn_a_kernel, 1, 1, A_PIPE_DEPTH, interleave=A_INTERLEAVE),
        out_shape=jax.ShapeDtypeStruct((batch, A_HEADS, SEQ, HEAD_DIM), BF16),
        grid=(batch, A_HEADS // n),
        in_specs=[pl.BlockSpec((SEQ, TQ), lambda b, h: (0, 0)),
                  _heads_spec(0, n), _heads_spec(A_HEADS, n), _heads_spec(2 * A_HEADS, n)],
        out_specs=_heads_spec(0, n),
        scratch_shapes=[pltpu.VMEM((A_INTERLEAVE, HEAD_DIM + ONES_ROWS, SEQ), BF16)],
        compiler_params=pltpu.CompilerParams(
            dimension_semantics=("parallel", "parallel"),
            vmem_limit_bytes=VMEM_LIMIT),
        name="dilated_attn",
    )(bias, h_ab, h_ab, h_ab)


B_Q_OFF, B_K_OFF, B_V_OFF = 24, 32, 40


def _attn_b_kernel(lam_ref, g_ref, q_ref, k_ref, v_ref, o_ref, vcat_ref, *, lam_init):
    vcat_ref[:, :HEAD_DIM] = v_ref[0]
    vcat_ref[:, HEAD_DIM:] = v_ref[1]
    lam_v = lam_ref[...]
    lam = (jnp.exp(jnp.sum(lam_v[0:1] * lam_v[1:2], axis=-1, keepdims=True))
           - jnp.exp(jnp.sum(lam_v[2:3] * lam_v[3:4], axis=-1, keepdims=True)) + lam_init)
    query_ge_key = _key_le_query_mask(key_axis=1)

    def scores(i):
        return [_nt_dot(q_ref[t, i * TQ:(i + 1) * TQ, :], k_ref[t, 0:(i + 1) * TQ, :])
                for t in range(2)]

    def consume(i, s12):
        ext = (i + 1) * TQ
        rows = slice(i * TQ, ext)
        ps, ls = [], []
        for s in s12:
            own = jnp.where(query_ge_key, s[:, i * TQ:], NEG)
            s = own if i == 0 else jnp.concatenate([s[:, :i * TQ], own], axis=1)
            m = jnp.max(s, axis=-1, keepdims=True)
            ps.append(jnp.exp2(s - m))
            ls.append(jnp.sum(ps[-1], axis=-1, keepdims=True))
        d = ps[0] - ps[1] * (lam * ls[0] / ls[1])
        o = jnp.dot(d.astype(BF16), vcat_ref[0:ext, :], preferred_element_type=F32) * (1.0 / ls[0])
        ms = jnp.mean(o * o, axis=-1, keepdims=True)
        y = o * lax.rsqrt(ms + NORM_EPS) * g_ref[...] * (1.0 - lam_init)
        o_ref[0, rows, :] = y[:, :HEAD_DIM].astype(BF16)
        o_ref[1, rows, :] = y[:, HEAD_DIM:].astype(BF16)

    return list(range(N_QT)), scores, consume


def _attn_b(h_ab, lam_vec, subln_g, lam_init):
    batch = h_ab.shape[0]
    heads = HEADS_PER_STEP // 2
    spec = lambda off: _heads_spec(off, 2 * heads)
    return pl.pallas_call(
        _per_head(functools.partial(_attn_b_kernel, lam_init=lam_init), 2, 1, B_PIPE_DEPTH,
                  interleave=B_INTERLEAVE, slabs_per_head=2),
        out_shape=jax.ShapeDtypeStruct((batch, 2 * B_HEADS, SEQ, HEAD_DIM), BF16),
        grid=(batch, B_HEADS // heads),
        in_specs=[
            pl.BlockSpec((4, HEAD_DIM), lambda b, h: (0, 0)),
            pl.BlockSpec((1, B_VDIM), lambda b, h: (0, 0)),
            spec(B_Q_OFF), spec(B_K_OFF), spec(B_V_OFF),
        ],
        out_specs=spec(0),
        scratch_shapes=[pltpu.VMEM((B_INTERLEAVE, SEQ, B_VDIM), BF16)],
        compiler_params=pltpu.CompilerParams(
            dimension_semantics=("parallel", "parallel"),
            vmem_limit_bytes=VMEM_LIMIT),
        name="diff_attn",
    )(lam_vec, subln_g, h_ab, h_ab, h_ab)


KM_ROWS = 16


def _attn_c_kernel(q_ref, k_ref, v_ref, o_ref, vt_ref):
    assert TQ == MOBA_BLOCK
    key_blk = lax.broadcasted_iota(jnp.int32, (KM_ROWS, SEQ), 1) // MOBA_BLOCK
    pool = (lax.broadcasted_iota(jnp.int32, (KM_ROWS, SEQ), 0) == key_blk).astype(BF16)
    km = jnp.dot(pool, k_ref[...], preferred_element_type=F32) * (1.0 / MOBA_BLOCK)
    km_hi = km.astype(BF16)
    km_lo = (km - km_hi.astype(F32)).astype(BF16)
    km_hl = jnp.concatenate([km_hi, km_lo], axis=0)
    row = lax.broadcasted_iota(jnp.int32, (KM_ROWS, TQ), 0)
    _store_vt(vt_ref, v_ref)
    key_le_query = _key_le_query_mask()

    def scores(i):
        q = q_ref[i * TQ:(i + 1) * TQ, :]
        g = _nt_dot(km_hl, q) if i > 0 else None
        s_t = _nt_dot(k_ref[0:(i + 1) * TQ, :], q)
        parts = []
        if i > 0:
            past = row < i
            gate = jnp.where(past, g[:KM_ROWS] + g[KM_ROWS:], -jnp.inf)
            rank = jnp.zeros(gate.shape, F32)
            for j in range(i):
                gj = gate[j:j + 1, :]
                beats = (gj > gate) | ((gj == gate) & (j < row))
                rank = rank + beats.astype(F32)
            bias = jnp.where(past & (rank < MOBA_TOPK), 0.0, NEG)
            for n in range(i):
                for r in range(n * TQ, (n + 1) * TQ, PART):
                    parts.append(s_t[r:r + PART, :] + bias[n:n + 1, :])
        for r in range(0, TQ, PART):
            parts.append(jnp.where(key_le_query[r:r + PART, :], s_t[i * TQ + r:i * TQ + r + PART, :], NEG))
        return _with_max(parts)

    def consume(i, parts_and_max):
        ext = (i + 1) * TQ
        o_ref[i * TQ:ext, :] = _softmax_pv_keymajor(parts_and_max, vt_ref[:, 0:ext]).astype(BF16)

    return list(range(N_QT)), scores, consume


def _attn_c(h_c):
    batch, n = h_c.shape[0], HEADS_PER_STEP
    return pl.pallas_call(
        _per_head(_attn_c_kernel, 0, 1, C_PIPE_DEPTH, interleave=C_INTERLEAVE),
        out_shape=jax.ShapeDtypeStruct((batch, C_HEADS, SEQ, HEAD_DIM), BF16),
        grid=(batch, C_HEADS // n),
        in_specs=[_heads_spec(0, n), _heads_spec(C_HEADS, n), _heads_spec(2 * C_HEADS, n)],
        out_specs=_heads_spec(0, n),
        scratch_shapes=[pltpu.VMEM((C_INTERLEAVE, HEAD_DIM + ONES_ROWS, SEQ), BF16)],
        compiler_params=pltpu.CompilerParams(
            dimension_semantics=("parallel", "parallel"),
            vmem_limit_bytes=VMEM_LIMIT),
        name="moba_attn",
    )(h_c, h_c, h_c)


OUT_TM = 512
OUT_SUB_M, OUT_SUB_N = 256, 256
Z_BLOCK_OFF = 3


def _outproj_kernel(*refs, n_y, final_norm):
    y_refs = refs[:n_y]
    z_ref, x_ref, w_ref, g_ref, o_ref, ysc_ref, wb_ref = refs[n_y:]
    y_blocks = [(y_ref, hh) for y_ref in y_refs for hh in range(y_ref.shape[0])]

    @pl.when(pl.program_id(0) == 0)
    def _():
        wb_ref[...] = w_ref[...].astype(BF16)
    for r in range(OUT_TM // OUT_SUB_M):
        rows = slice(r * OUT_SUB_M, (r + 1) * OUT_SUB_M)
        for c, (y_ref, hh) in enumerate(y_blocks):
            z = z_ref[c, rows, :].astype(F32)
            gate = z / (1.0 + jnp.exp(-z))
            ysc_ref[rows, c * HEAD_DIM:(c + 1) * HEAD_DIM] = (
                y_ref[hh, rows, :].astype(F32) * gate).astype(BF16)
        ssq = jnp.zeros((OUT_SUB_M, 1), F32)
        for n in range(D_MODEL // OUT_SUB_N):
            cols = slice(n * OUT_SUB_N, (n + 1) * OUT_SUB_N)
            h = x_ref[rows, cols] + jnp.dot(ysc_ref[rows, :], wb_ref[:, cols],
                                            preferred_element_type=F32)
            o_ref[rows, cols] = h
            if final_norm:
                ssq = ssq + jnp.sum(h * h, axis=-1, keepdims=True)
        if final_norm:
            inv = lax.rsqrt(ssq * (1.0 / D_MODEL) + NORM_EPS)
            o_ref[rows, :] = o_ref[rows, :] * inv * g_ref[...]


def _outproj(ys, h_in, x2d, w, g_final, batch, final_norm):
    rows = x2d.shape[0]
    s_tiles = SEQ // OUT_TM
    n_z = D_MODEL // HEAD_DIM
    y_specs = [pl.BlockSpec((None, y.shape[1], OUT_TM, HEAD_DIM),
                            lambda i: (i // s_tiles, 0, i % s_tiles, 0)) for y in ys]
    assert sum(y.shape[1] for y in ys) == n_z
    return pl.pallas_call(
        functools.partial(_outproj_kernel, n_y=len(ys), final_norm=final_norm),
        out_shape=jax.ShapeDtypeStruct((rows, D_MODEL), F32),
        grid=(rows // OUT_TM,),
        in_specs=y_specs + [
            pl.BlockSpec((None, n_z, OUT_TM, HEAD_DIM),
                         lambda i: (i // s_tiles, Z_BLOCK_OFF, i % s_tiles, 0)),
            pl.BlockSpec((OUT_TM, D_MODEL), lambda i: (i, 0)),
            pl.BlockSpec((D_MODEL, D_MODEL), lambda i: (0, 0), pipeline_mode=pl.Buffered(1)),
            pl.BlockSpec((1, D_MODEL), lambda i: (0, 0)),
        ],
        out_specs=pl.BlockSpec((OUT_TM, D_MODEL), lambda i: (i, 0)),
        scratch_shapes=[pltpu.VMEM((OUT_TM, D_MODEL), BF16), pltpu.VMEM((D_MODEL, D_MODEL), BF16)],
        compiler_params=pltpu.CompilerParams(
            dimension_semantics=("arbitrary",),
            vmem_limit_bytes=VMEM_LIMIT),
        name="gate_outproj",
    )(*ys, h_in, x2d, w, g_final)


def _rope_tables():
    inv = 1.0 / (ROPE_THETA ** (np.arange(0, HEAD_DIM, 2, dtype=np.float64) / HEAD_DIM))
    ang = np.arange(SEQ, dtype=np.float64)[:, None] * inv[None, :]
    cos = np.concatenate([np.cos(ang), np.cos(ang)], axis=-1)
    sin = np.concatenate([-np.sin(ang), np.sin(ang)], axis=-1)
    return jnp.asarray(cos, F32), jnp.asarray(sin, F32)


def _tile_kinds(group_kinds, cols_per_group):
    kinds = []
    for kind, cols in zip(group_kinds, cols_per_group):
        assert cols % IN_TN == 0
        kinds += [kind] * (cols // IN_TN)
    assert len(kinds) == IN_WIDTH // IN_TN
    return jnp.asarray(kinds, jnp.int32)


def kernel(x, norm_ab, w_in_ab, w_out_ab, lam_ab, subln_ab, norm_c, w_in_c, w_out_c, final_norm):
    batch, seq, d = x.shape
    assert (seq, d) == (SEQ, D_MODEL) and x.dtype == F32
    assert w_in_ab.shape == (1, D_MODEL, IN_WIDTH) and w_in_c.shape == (1, D_MODEL, IN_WIDTH)
    cos, sin = _rope_tables()
    aw, bw = A_HEADS * HEAD_DIM, B_HEADS * B_VDIM
    kinds_ab = _tile_kinds(
        [KIND_ROPE_SCALED, KIND_ROPE, KIND_PLAIN, KIND_ROPE_SCALED, KIND_ROPE, KIND_PLAIN, KIND_PLAIN],
        [aw, aw, aw, bw, bw, bw, aw + bw])
    cw = C_HEADS * HEAD_DIM
    kinds_c = _tile_kinds([KIND_ROPE_SCALED, KIND_ROPE, KIND_PLAIN, KIND_PLAIN], [cw, cw, cw, cw])

    x2d = x.reshape(batch * seq, d)

    g_final = final_norm[None, :]
    h_ab = _inproj(x2d, norm_ab[0][None, :], w_in_ab[0], cos, sin, kinds_ab, batch)
    ya = _attn_a(h_ab, _dilated_bias_table())
    lam_init = 0.8 - 0.6 * math.exp(-0.3 * 0)
    yb = _attn_b(h_ab, lam_ab[0], subln_ab[0][None, :], lam_init)
    x1 = _outproj([ya, yb], h_ab, x2d, w_out_ab[0], g_final, batch, final_norm=False)

    h_c = _inproj(x1, norm_c[0][None, :], w_in_c[0], cos, sin, kinds_c, batch)
    yc = _attn_c(h_c)
    out = _outproj([yc], h_c, x1, w_out_c[0], g_final, batch, final_norm=True)
    return out.reshape(batch, seq, d)
```

```python
import functools
import math

import numpy as np
import jax
import jax.numpy as jnp
from jax import lax
from jax.experimental import pallas as pl
from jax.experimental.pallas import tpu as pltpu

D_MODEL = 2048
SEQ = 2048
HEAD_DIM = 128
ROPE_THETA = 10000.0
NORM_EPS = 1e-6
A_HEADS = 8
DIL_CONFIGS = ((128, 1), (512, 4), (2048, 16))
B_HEADS = 4
B_VDIM = 2 * HEAD_DIM
C_HEADS = 16
MOBA_BLOCK = 256
MOBA_TOPK = 3
N_MOBA_BLOCKS = SEQ // MOBA_BLOCK
IN_WIDTH = 8192
N_COL_BLOCKS = IN_WIDTH // HEAD_DIM
SCALE = HEAD_DIM ** -0.5
Q_SCALE = SCALE * math.log2(math.e)
NEG = -1e30

BF16 = jnp.bfloat16
F32 = jnp.float32

KIND_PLAIN, KIND_ROPE, KIND_ROPE_SCALED = 0, 1, 2

VMEM_LIMIT = 56 * 1024 * 1024


def _nt_dot(a, b):
    return lax.dot_general(a, b, (((1,), (1,)), ((), ())), preferred_element_type=F32)


IN_TM, IN_TN = 2048, 1024
IN_SUB_M, IN_SUB_N = 512, 256


def _inproj_kernel(kinds_ref, x_hbm, g_ref, w_ref, cos_ref, sin_ref, o_ref, xn_ref, x_ref, x_sem):
    i, j = pl.program_id(0), pl.program_id(1)

    def x_copy(row_tile):
        return pltpu.make_async_copy(x_hbm.at[pl.ds(row_tile * IN_TM, IN_TM), :], x_ref, x_sem)

    @pl.when((i == 0) & (j == 0))
    def _():
        x_copy(0).start()

    @pl.when((j == 1) & (i + 1 < pl.num_programs(0)))
    def _():
        x_copy(i + 1).start()

    kind = kinds_ref[j]
    scale = jnp.where(kind == KIND_ROPE_SCALED, Q_SCALE, 1.0).astype(F32)

    def tile(normalise):
        for nh in range(IN_TN // IN_SUB_N):
            w = w_ref[:, nh * IN_SUB_N:(nh + 1) * IN_SUB_N].astype(BF16)
            for mh in range(IN_TM // IN_SUB_M):
                rows = slice(mh * IN_SUB_M, (mh + 1) * IN_SUB_M)
                if normalise and nh == 0:
                    x = x_ref[rows, :]
                    ms = jnp.mean(x * x, axis=-1, keepdims=True)
                    xn_ref[rows, :] = (x * lax.rsqrt(ms + NORM_EPS) * g_ref[...]).astype(BF16)
                acc = jnp.dot(xn_ref[rows, :], w, preferred_element_type=F32)
                cos = jnp.where(kind == KIND_PLAIN, 1.0, cos_ref[rows, :] * scale)
                sin = jnp.where(kind == KIND_PLAIN, 0.0, sin_ref[rows, :] * scale)
                for c in range(IN_SUB_N // HEAD_DIM):
                    t = acc[:, c * HEAD_DIM:(c + 1) * HEAD_DIM]
                    o_ref[nh * (IN_SUB_N // HEAD_DIM) + c, rows, :] = (
                        t * cos + pltpu.roll(t, HEAD_DIM // 2, 1) * sin).astype(BF16)

    @pl.when(j == 0)
    def _():
        x_copy(i).wait()
        tile(True)

    pl.when(j != 0)(lambda: tile(False))


def _inproj(x2d, g, w, cos, sin_signed, kinds, batch):
    rows = x2d.shape[0]
    s_tiles = SEQ // IN_TM
    grid = (rows // IN_TM, IN_WIDTH // IN_TN)
    blk_per_tile = IN_TN // HEAD_DIM
    return pl.pallas_call(
        _inproj_kernel,
        out_shape=jax.ShapeDtypeStruct((batch, N_COL_BLOCKS, SEQ, HEAD_DIM), BF16),
        grid_spec=pltpu.PrefetchScalarGridSpec(
            num_scalar_prefetch=1,
            grid=grid,
            in_specs=[
                pl.BlockSpec(memory_space=pl.ANY),
                pl.BlockSpec((1, D_MODEL), lambda i, j, k: (0, 0)),
                pl.BlockSpec((D_MODEL, IN_TN), lambda i, j, k: (0, j)),
                pl.BlockSpec((IN_TM, HEAD_DIM), lambda i, j, k: (i % s_tiles, 0)),
                pl.BlockSpec((IN_TM, HEAD_DIM), lambda i, j, k: (i % s_tiles, 0)),
            ],
            out_specs=pl.BlockSpec((None, blk_per_tile, IN_TM, HEAD_DIM),
                                   lambda i, j, k: (i // s_tiles, j, i % s_tiles, 0)),
            scratch_shapes=[pltpu.VMEM((IN_TM, D_MODEL), BF16), pltpu.VMEM((IN_TM, D_MODEL), F32),
                            pltpu.SemaphoreType.DMA],
        ),
        compiler_params=pltpu.CompilerParams(
            dimension_semantics=("arbitrary", "arbitrary"),
            vmem_limit_bytes=VMEM_LIMIT),
        name="norm_inproj",
    )(kinds, x2d, g, w, cos, sin_signed)


TQ = 256
N_QT = SEQ // TQ
PART = 256
A_PIPE_DEPTH, B_PIPE_DEPTH, C_PIPE_DEPTH = 6, 3, 2
A_INTERLEAVE, B_INTERLEAVE, C_INTERLEAVE = 1, 2, 4


ONES_ROWS = 16


def _store_vt(vt_ref, v_ref):
    dv = v_ref.shape[1]
    vt_ref[0:dv, :] = v_ref[...].astype(F32).T.astype(BF16)
    vt_ref[dv:, :] = jnp.ones((ONES_ROWS, v_ref.shape[0]), BF16)


def _with_max(parts):
    return parts, functools.reduce(jnp.maximum, [jnp.max(s, axis=0, keepdims=True) for s in parts])


def _softmax_pv_keymajor(parts_and_max, vt):
    parts, m = parts_and_max
    p_t = jnp.concatenate([jnp.exp2(s - m).astype(BF16) for s in parts], axis=0)
    o_t = jnp.dot(vt, p_t, preferred_element_type=F32)
    dv = vt.shape[0] - ONES_ROWS
    return (o_t[:dv] * (1.0 / o_t[dv:dv + 1])).T


def _pipelined(units, scores, consume, depth):
    ready = [scores(u) for u in units[:depth]]
    for idx, u in enumerate(units):
        if idx + depth < len(units):
            ready.append(scores(units[idx + depth]))
        consume(u, ready.pop(0))


def _key_le_query_mask(key_axis=0):
    key = lax.broadcasted_iota(jnp.int32, (TQ, TQ), key_axis)
    query = lax.broadcasted_iota(jnp.int32, (TQ, TQ), 1 - key_axis)
    return key <= query


def _dilated_bias_table():
    key = np.arange(SEQ)[:, None]
    query = (SEQ - TQ) + np.arange(TQ)[None, :]
    d = query - key
    mult = np.zeros(d.shape, np.float64)
    for window, dil in DIL_CONFIGS:
        mult += (d >= 0) & (d % dil == 0) & (d <= window)
    return jnp.asarray(np.where(mult > 0, np.log2(np.maximum(mult, 1.0)), NEG), F32)


def _per_head(head_fn, n_shared, n_scratch, depth, interleave=1, slabs_per_head=1):
    def group_kernel(*refs):
        shared = refs[:n_shared]
        grouped = refs[n_shared:len(refs) - n_scratch]
        scratch = refs[len(refs) - n_scratch:]

        def view(r, g):
            return r.at[g] if slabs_per_head == 1 else r.at[pl.ds(g * slabs_per_head, slabs_per_head)]

        def body(g, carry):
            heads = [head_fn(*shared, *[view(r, g * interleave + h) for r in grouped],
                             *[s.at[h] for s in scratch]) for h in range(interleave)]
            n_units = len(heads[0][0])
            units = [(h, heads[h][0][idx]) for idx in range(n_units) for h in range(interleave)]
            _pipelined(units, lambda hu: heads[hu[0]][1](hu[1]),
                       lambda hu, s: heads[hu[0]][2](hu[1], s), depth * interleave)
            return carry

        lax.fori_loop(0, grouped[0].shape[0] // (slabs_per_head * interleave), body, 0)
    return group_kernel


def _heads_spec(off, n):
    assert off % n == 0
    return pl.BlockSpec((None, n, SEQ, HEAD_DIM), lambda b, h: (b, off // n + h, 0, 0))


HEADS_PER_STEP = 4


def _attn_a_kernel(bias_ref, q_ref, k_ref, v_ref, o_ref, vt_ref):
    _store_vt(vt_ref, v_ref)

    def scores(i):
        return _nt_dot(k_ref[0:(i + 1) * TQ, :], q_ref[i * TQ:(i + 1) * TQ, :])

    def consume(i, s_t):
        ext = (i + 1) * TQ
        off = SEQ - ext
        parts = [s_t[r:r + PART, :] + bias_ref[off + r:off + r + PART, :] for r in range(0, ext, PART)]
        o = _softmax_pv_keymajor(_with_max(parts), vt_ref[:, 0:ext])
        o_ref[i * TQ:ext, :] = o.astype(BF16)

    return list(range(N_QT)), scores, consume


def _attn_a(h_ab, bias):
    batch, n = h_ab.shape[0], HEADS_PER_STEP
    return pl.pallas_call(
        _per_head(_attn_a_kernel, 1, 1, A_PIPE_DEPTH, interleave=A_INTERLEAVE),
        out_shape=jax.ShapeDtypeStruct((batch, A_HEADS, SEQ, HEAD_DIM), BF16),
        grid=(batch, A_HEADS // n),
        in_specs=[pl.BlockSpec((SEQ, TQ), lambda b, h: (0, 0)),
                  _heads_spec(0, n), _heads_spec(A_HEADS, n), _heads_spec(2 * A_HEADS, n)],
        out_specs=_heads_spec(0, n),
        scratch_shapes=[pltpu.VMEM((A_INTERLEAVE, HEAD_DIM + ONES_ROWS, SEQ), BF16)],
        compiler_params=pltpu.CompilerParams(
            dimension_semantics=("parallel", "parallel"),
            vmem_limit_bytes=VMEM_LIMIT),
        name="dilated_attn",
    )(bias, h_ab, h_ab, h_ab)


B_Q_OFF, B_K_OFF, B_V_OFF = 24, 32, 40


def _attn_b_kernel(lam_ref, g_ref, q_ref, k_ref, v_ref, o_ref, vcat_ref, *, lam_init):
    vcat_ref[:, :HEAD_DIM] = v_ref[0]
    vcat_ref[:, HEAD_DIM:] = v_ref[1]
    lam_v = lam_ref[...]
    lam = (jnp.exp(jnp.sum(lam_v[0:1] * lam_v[1:2], axis=-1, keepdims=True))
           - jnp.exp(jnp.sum(lam_v[2:3] * lam_v[3:4], axis=-1, keepdims=True)) + lam_init)
    query_ge_key = _key_le_query_mask(key_axis=1)

    def scores(i):
        return [_nt_dot(q_ref[t, i * TQ:(i + 1) * TQ, :], k_ref[t, 0:(i + 1) * TQ, :])
                for t in range(2)]

    def consume(i, s12):
        ext = (i + 1) * TQ
        rows = slice(i * TQ, ext)
        ps, ls = [], []
        for s in s12:
            own = jnp.where(query_ge_key, s[:, i * TQ:], NEG)
            s = own if i == 0 else jnp.concatenate([s[:, :i * TQ], own], axis=1)
            m = jnp.max(s, axis=-1, keepdims=True)
            ps.append(jnp.exp2(s - m))
            ls.append(jnp.sum(ps[-1], axis=-1, keepdims=True))
        d = ps[0] - ps[1] * (lam * ls[0] / ls[1])
        o = jnp.dot(d.astype(BF16), vcat_ref[0:ext, :], preferred_element_type=F32) * (1.0 / ls[0])
        ms = jnp.mean(o * o, axis=-1, keepdims=True)
        y = o * lax.rsqrt(ms + NORM_EPS) * g_ref[...] * (1.0 - lam_init)
        o_ref[0, rows, :] = y[:, :HEAD_DIM].astype(BF16)
        o_ref[1, rows, :] = y[:, HEAD_DIM:].astype(BF16)

    return list(range(N_QT)), scores, consume


def _attn_b(h_ab, lam_vec, subln_g, lam_init):
    batch = h_ab.shape[0]
    heads = HEADS_PER_STEP // 2
    spec = lambda off: _heads_spec(off, 2 * heads)
    return pl.pallas_call(
        _per_head(functools.partial(_attn_b_kernel, lam_init=lam_init), 2, 1, B_PIPE_DEPTH,
                  interleave=B_INTERLEAVE, slabs_per_head=2),
        out_shape=jax.ShapeDtypeStruct((batch, 2 * B_HEADS, SEQ, HEAD_DIM), BF16),
        grid=(batch, B_HEADS // heads),
        in_specs=[
            pl.BlockSpec((4, HEAD_DIM), lambda b, h: (0, 0)),
            pl.BlockSpec((1, B_VDIM), lambda b, h: (0, 0)),
            spec(B_Q_OFF), spec(B_K_OFF), spec(B_V_OFF),
        ],
        out_specs=spec(0),
        scratch_shapes=[pltpu.VMEM((B_INTERLEAVE, SEQ, B_VDIM), BF16)],
        compiler_params=pltpu.CompilerParams(
            dimension_semantics=("parallel", "parallel"),
            vmem_limit_bytes=VMEM_LIMIT),
        name="diff_attn",
    )(lam_vec, subln_g, h_ab, h_ab, h_ab)


KM_ROWS = 16


def _attn_c_kernel(q_ref, k_ref, v_ref, o_ref, vt_ref):
    assert TQ == MOBA_BLOCK
    key_blk = lax.broadcasted_iota(jnp.int32, (KM_ROWS, SEQ), 1) // MOBA_BLOCK
    pool = (lax.broadcasted_iota(jnp.int32, (KM_ROWS, SEQ), 0) == key_blk).astype(BF16)
    km = jnp.dot(pool, k_ref[...], preferred_element_type=F32) * (1.0 / MOBA_BLOCK)
    km_hi = km.astype(BF16)
    km_lo = (km - km_hi.astype(F32)).astype(BF16)
    km_hl = jnp.concatenate([km_hi, km_lo], axis=0)
    row = lax.broadcasted_iota(jnp.int32, (KM_ROWS, TQ), 0)
    _store_vt(vt_ref, v_ref)
    key_le_query = _key_le_query_mask()

    def scores(i):
        q = q_ref[i * TQ:(i + 1) * TQ, :]
        g = _nt_dot(km_hl, q) if i > 0 else None
        s_t = _nt_dot(k_ref[0:(i + 1) * TQ, :], q)
        parts = []
        if i > 0:
            past = row < i
            gate = jnp.where(past, g[:KM_ROWS] + g[KM_ROWS:], -jnp.inf)
            rank = jnp.zeros(gate.shape, F32)
            for j in range(i):
                gj = gate[j:j + 1, :]
                beats = (gj > gate) | ((gj == gate) & (j < row))
                rank = rank + beats.astype(F32)
            bias = jnp.where(past & (rank < MOBA_TOPK), 0.0, NEG)
            for n in range(i):
                for r in range(n * TQ, (n + 1) * TQ, PART):
                    parts.append(s_t[r:r + PART, :] + bias[n:n + 1, :])
        for r in range(0, TQ, PART):
            parts.append(jnp.where(key_le_query[r:r + PART, :], s_t[i * TQ + r:i * TQ + r + PART, :], NEG))
        return _with_max(parts)

    def consume(i, parts_and_max):
        ext = (i + 1) * TQ
        o_ref[i * TQ:ext, :] = _softmax_pv_keymajor(parts_and_max, vt_ref[:, 0:ext]).astype(BF16)

    return list(range(N_QT)), scores, consume


def _attn_c(h_c):
    batch, n = h_c.shape[0], HEADS_PER_STEP
    return pl.pallas_call(
        _per_head(_attn_c_kernel, 0, 1, C_PIPE_DEPTH, interleave=C_INTERLEAVE),
        out_shape=jax.ShapeDtypeStruct((batch, C_HEADS, SEQ, HEAD_DIM), BF16),
        grid=(batch, C_HEADS // n),
        in_specs=[_heads_spec(0, n), _heads_spec(C_HEADS, n), _heads_spec(2 * C_HEADS, n)],
        out_specs=_heads_spec(0, n),
        scratch_shapes=[pltpu.VMEM((C_INTERLEAVE, HEAD_DIM + ONES_ROWS, SEQ), BF16)],
        compiler_params=pltpu.CompilerParams(
            dimension_semantics=("parallel", "parallel"),
            vmem_limit_bytes=VMEM_LIMIT),
        name="moba_attn",
    )(h_c, h_c, h_c)


OUT_TM = 512
OUT_SUB_M, OUT_SUB_N = 256, 256
Z_BLOCK_OFF = 3


def _outproj_kernel(*refs, n_y, final_norm):
    y_refs = refs[:n_y]
    z_ref, x_ref, w_ref, g_ref, o_ref, ysc_ref, wb_ref = refs[n_y:]
    y_blocks = [(y_ref, hh) for y_ref in y_refs for hh in range(y_ref.shape[0])]

    @pl.when(pl.program_id(0) == 0)
    def _():
        wb_ref[...] = w_ref[...].astype(BF16)
    for r in range(OUT_TM // OUT_SUB_M):
        rows = slice(r * OUT_SUB_M, (r + 1) * OUT_SUB_M)
        for c, (y_ref, hh) in enumerate(y_blocks):
            z = z_ref[c, rows, :].astype(F32)
            gate = z / (1.0 + jnp.exp(-z))
            ysc_ref[rows, c * HEAD_DIM:(c + 1) * HEAD_DIM] = (
                y_ref[hh, rows, :].astype(F32) * gate).astype(BF16)
        ssq = jnp.zeros((OUT_SUB_M, 1), F32)
        for n in range(D_MODEL // OUT_SUB_N):
            cols = slice(n * OUT_SUB_N, (n + 1) * OUT_SUB_N)
            h = x_ref[rows, cols] + jnp.dot(ysc_ref[rows, :], wb_ref[:, cols],
                                            preferred_element_type=F32)
            o_ref[rows, cols] = h
            if final_norm:
                ssq = ssq + jnp.sum(h * h, axis=-1, keepdims=True)
        if final_norm:
            inv = lax.rsqrt(ssq * (1.0 / D_MODEL) + NORM_EPS)
            o_ref[rows, :] = o_ref[rows, :] * inv * g_ref[...]


def _outproj(ys, h_in, x2d, w, g_final, batch, final_norm):
    rows = x2d.shape[0]
    s_tiles = SEQ // OUT_TM
    n_z = D_MODEL // HEAD_DIM
    y_specs = [pl.BlockSpec((None, y.shape[1], OUT_TM, HEAD_DIM),
                            lambda i: (i // s_tiles, 0, i % s_tiles, 0)) for y in ys]
    assert sum(y.shape[1] for y in ys) == n_z
    return pl.pallas_call(
        functools.partial(_outproj_kernel, n_y=len(ys), final_norm=final_norm),
        out_shape=jax.ShapeDtypeStruct((rows, D_MODEL), F32),
        grid=(rows // OUT_TM,),
        in_specs=y_specs + [
            pl.BlockSpec((None, n_z, OUT_TM, HEAD_DIM),
                         lambda i: (i // s_tiles, Z_BLOCK_OFF, i % s_tiles, 0)),
            pl.BlockSpec((OUT_TM, D_MODEL), lambda i: (i, 0)),
            pl.BlockSpec((D_MODEL, D_MODEL), lambda i: (0, 0), pipeline_mode=pl.Buffered(1)),
            pl.BlockSpec((1, D_MODEL), lambda i: (0, 0)),
        ],
        out_specs=pl.BlockSpec((OUT_TM, D_MODEL), lambda i: (i, 0)),
        scratch_shapes=[pltpu.VMEM((OUT_TM, D_MODEL), BF16), pltpu.VMEM((D_MODEL, D_MODEL), BF16)],
        compiler_params=pltpu.CompilerParams(
            dimension_semantics=("arbitrary",),
            vmem_limit_bytes=VMEM_LIMIT),
        name="gate_outproj",
    )(*ys, h_in, x2d, w, g_final)


def _rope_tables():
    inv = 1.0 / (ROPE_THETA ** (np.arange(0, HEAD_DIM, 2, dtype=np.float64) / HEAD_DIM))
    ang = np.arange(SEQ, dtype=np.float64)[:, None] * inv[None, :]
    cos = np.concatenate([np.cos(ang), np.cos(ang)], axis=-1)
    sin = np.concatenate([-np.sin(ang), np.sin(ang)], axis=-1)
    return jnp.asarray(cos, F32), jnp.asarray(sin, F32)


def _tile_kinds(group_kinds, cols_per_group):
    kinds = []
    for kind, cols in zip(group_kinds, cols_per_group):
        assert cols % IN_TN == 0
        kinds += [kind] * (cols // IN_TN)
    assert len(kinds) == IN_WIDTH // IN_TN
    return jnp.asarray(kinds, jnp.int32)


def kernel(x, norm_ab, w_in_ab, w_out_ab, lam_ab, subln_ab, norm_c, w_in_c, w_out_c, final_norm):
    batch, seq, d = x.shape
    assert (seq, d) == (SEQ, D_MODEL) and x.dtype == F32
    assert w_in_ab.shape == (1, D_MODEL, IN_WIDTH) and w_in_c.shape == (1, D_MODEL, IN_WIDTH)
    cos, sin = _rope_tables()
    aw, bw = A_HEADS * HEAD_DIM, B_HEADS * B_VDIM
    kinds_ab = _tile_kinds(
        [KIND_ROPE_SCALED, KIND_ROPE, KIND_PLAIN, KIND_ROPE_SCALED, KIND_ROPE, KIND_PLAIN, KIND_PLAIN],
        [aw, aw, aw, bw, bw, bw, aw + bw])
    cw = C_HEADS * HEAD_DIM
    kinds_c = _tile_kinds([KIND_ROPE_SCALED, KIND_ROPE, KIND_PLAIN, KIND_PLAIN], [cw, cw, cw, cw])

    x2d = x.reshape(batch * seq, d)

    g_final = final_norm[None, :]
    h_ab = _inproj(x2d, norm_ab[0][None, :], w_in_ab[0], cos, sin, kinds_ab, batch)
    ya = _attn_a(h_ab, _dilated_bias_table())
    lam_init = 0.8 - 0.6 * math.exp(-0.3 * 0)
    yb = _attn_b(h_ab, lam_ab[0], subln_ab[0][None, :], lam_init)
    x1 = _outproj([ya, yb], h_ab, x2d, w_out_ab[0], g_final, batch, final_norm=False)

    h_c = _inproj(x1, norm_c[0][None, :], w_in_c[0], cos, sin, kinds_c, batch)
    yc = _attn_c(h_c)
    out = _outproj([yc], h_c, x1, w_out_c[0], g_final, batch, final_norm=True)
    return out.reshape(batch, seq, d)
```

```python
import functools
import math

import numpy as np
import jax
import jax.numpy as jnp
from jax import lax
from jax.experimental import pallas as pl
from jax.experimental.pallas import tpu as pltpu

D_MODEL = 2048
SEQ = 2048
HEAD_DIM = 128
ROPE_THETA = 10000.0
NORM_EPS = 1e-6
A_HEADS = 8
DIL_CONFIGS = ((128, 1), (512, 4), (2048, 16))
B_HEADS = 4
B_VDIM = 2 * HEAD_DIM
C_HEADS = 16
MOBA_BLOCK = 256
MOBA_TOPK = 3
N_MOBA_BLOCKS = SEQ // MOBA_BLOCK
IN_WIDTH = 8192
N_COL_BLOCKS = IN_WIDTH // HEAD_DIM
Z_SLAB_OFF = 48
SCALE = HEAD_DIM ** -0.5
Q_SCALE = SCALE * math.log2(math.e)
NEG = -1e30

BF16 = jnp.bfloat16
F32 = jnp.float32

KIND_PLAIN, KIND_ROPE, KIND_ROPE_SCALED = 0, 1, 2

VMEM_LIMIT = 56 * 1024 * 1024


def _nt_dot(a, b):
    return lax.dot_general(a, b, (((1,), (1,)), ((), ())), preferred_element_type=F32)


IN_TM, IN_TN = 2048, 1024
IN_SUB_M, IN_SUB_N = 512, 256


def _inproj_kernel(kinds_ref, x_hbm, g_ref, w_ref, cos_ref, sin_ref, o_ref, xn_ref, x_ref, x_sem):
    i, j = pl.program_id(0), pl.program_id(1)

    def x_copy(row_tile):
        return pltpu.make_async_copy(x_hbm.at[pl.ds(row_tile * IN_TM, IN_TM), :], x_ref, x_sem)

    @pl.when((i == 0) & (j == 0))
    def _():
        x_copy(0).start()

    @pl.when((j == 1) & (i + 1 < pl.num_programs(0)))
    def _():
        x_copy(i + 1).start()

    kind = kinds_ref[j]
    scale = jnp.where(kind == KIND_ROPE_SCALED, Q_SCALE, 1.0).astype(F32)

    def tile(normalise):
        for nh in range(IN_TN // IN_SUB_N):
            w = w_ref[:, nh * IN_SUB_N:(nh + 1) * IN_SUB_N].astype(BF16)
            for mh in range(IN_TM // IN_SUB_M):
                rows = slice(mh * IN_SUB_M, (mh + 1) * IN_SUB_M)
                if normalise and nh == 0:
                    x = x_ref[rows, :]
                    ms = jnp.mean(x * x, axis=-1, keepdims=True)
                    xn_ref[rows, :] = (x * lax.rsqrt(ms + NORM_EPS) * g_ref[...]).astype(BF16)
                acc = jnp.dot(xn_ref[rows, :], w, preferred_element_type=F32)
                cos = jnp.where(kind == KIND_PLAIN, 1.0, cos_ref[rows, :] * scale)
                sin = jnp.where(kind == KIND_PLAIN, 0.0, sin_ref[rows, :] * scale)
                for c in range(IN_SUB_N // HEAD_DIM):
                    t = acc[:, c * HEAD_DIM:(c + 1) * HEAD_DIM]
                    o_ref[nh * (IN_SUB_N // HEAD_DIM) + c, rows, :] = (
                        t * cos + pltpu.roll(t, HEAD_DIM // 2, 1) * sin).astype(BF16)

    @pl.when(j == 0)
    def _():
        x_copy(i).wait()
        tile(True)

    pl.when(j != 0)(lambda: tile(False))


def _inproj(x2d, g, w, cos, sin_signed, kinds, batch):
    rows = x2d.shape[0]
    s_tiles = SEQ // IN_TM
    grid = (rows // IN_TM, IN_WIDTH // IN_TN)
    blk_per_tile = IN_TN // HEAD_DIM
    return pl.pallas_call(
        _inproj_kernel,
        out_shape=jax.ShapeDtypeStruct((batch, N_COL_BLOCKS, SEQ, HEAD_DIM), BF16),
        grid_spec=pltpu.PrefetchScalarGridSpec(
            num_scalar_prefetch=1,
            grid=grid,
            in_specs=[
                pl.BlockSpec(memory_space=pl.ANY),
                pl.BlockSpec((1, D_MODEL), lambda i, j, k: (0, 0)),
                pl.BlockSpec((D_MODEL, IN_TN), lambda i, j, k: (0, j)),
                pl.BlockSpec((IN_TM, HEAD_DIM), lambda i, j, k: (i % s_tiles, 0)),
                pl.BlockSpec((IN_TM, HEAD_DIM), lambda i, j, k: (i % s_tiles, 0)),
            ],
            out_specs=pl.BlockSpec((None, blk_per_tile, IN_TM, HEAD_DIM),
                                   lambda i, j, k: (i // s_tiles, j, i % s_tiles, 0)),
            scratch_shapes=[pltpu.VMEM((IN_TM, D_MODEL), BF16), pltpu.VMEM((IN_TM, D_MODEL), F32),
                            pltpu.SemaphoreType.DMA],
        ),
        compiler_params=pltpu.CompilerParams(
            dimension_semantics=("arbitrary", "arbitrary"),
            vmem_limit_bytes=VMEM_LIMIT),
        name="norm_inproj",
    )(kinds, x2d, g, w, cos, sin_signed)


TQ = 256
N_QT = SEQ // TQ
PART = 256
A_PIPE_DEPTH, B_PIPE_DEPTH, C_PIPE_DEPTH = 6, 3, 2
A_INTERLEAVE, B_INTERLEAVE, C_INTERLEAVE = 1, 2, 4


ONES_ROWS = 16


def _store_vt(vt_ref, v_ref):
    dv = v_ref.shape[1]
    vt_ref[0:dv, :] = v_ref[...].astype(F32).T.astype(BF16)
    vt_ref[dv:, :] = jnp.ones((ONES_ROWS, v_ref.shape[0]), BF16)


def _with_max(parts):
    return parts, functools.reduce(jnp.maximum, [jnp.max(s, axis=0, keepdims=True) for s in parts])


def _softmax_pv_keymajor(parts_and_max, vt):
    parts, m = parts_and_max
    p_t = jnp.concatenate([jnp.exp2(s - m).astype(BF16) for s in parts], axis=0)
    o_t = jnp.dot(vt, p_t, preferred_element_type=F32)
    dv = vt.shape[0] - ONES_ROWS
    return (o_t[:dv] * (1.0 / o_t[dv:dv + 1])).T


def _pipelined(units, scores, consume, depth):
    ready = [scores(u) for u in units[:depth]]
    for idx, u in enumerate(units):
        if idx + depth < len(units):
            ready.append(scores(units[idx + depth]))
        consume(u, ready.pop(0))


def _key_le_query_mask(key_axis=0):
    key = lax.broadcasted_iota(jnp.int32, (TQ, TQ), key_axis)
    query = lax.broadcasted_iota(jnp.int32, (TQ, TQ), 1 - key_axis)
    return key <= query


def _dilated_bias_table():
    key = np.arange(SEQ)[:, None]
    query = (SEQ - TQ) + np.arange(TQ)[None, :]
    d = query - key
    mult = np.zeros(d.shape, np.float64)
    for window, dil in DIL_CONFIGS:
        mult += (d >= 0) & (d % dil == 0) & (d <= window)
    return jnp.asarray(np.where(mult > 0, np.log2(np.maximum(mult, 1.0)), NEG), F32)


def _per_head(head_fn, n_shared, n_scratch, depth, interleave=1, slabs_per_head=1):
    def group_kernel(*refs):
        shared = refs[:n_shared]
        grouped = refs[n_shared:len(refs) - n_scratch]
        scratch = refs[len(refs) - n_scratch:]

        def view(r, g):
            return r.at[g] if slabs_per_head == 1 else r.at[pl.ds(g * slabs_per_head, slabs_per_head)]

        def body(g, carry):
            heads = [head_fn(*shared, *[view(r, g * interleave + h) for r in grouped],
                             *[s.at[h] for s in scratch]) for h in range(interleave)]
            n_units = len(heads[0][0])
            units = [(h, heads[h][0][idx]) for idx in range(n_units) for h in range(interleave)]
            _pipelined(units, lambda hu: heads[hu[0]][1](hu[1]),
                       lambda hu, s: heads[hu[0]][2](hu[1], s), depth * interleave)
            return carry

        lax.fori_loop(0, grouped[0].shape[0] // (slabs_per_head * interleave), body, 0)
    return group_kernel


def _heads_spec(off, n):
    assert off % n == 0
    return pl.BlockSpec((None, n, SEQ, HEAD_DIM), lambda b, h: (b, off // n + h, 0, 0))


HEADS_PER_STEP = 4


def _gated(o, z):
    z = z.astype(F32)
    return (o * (z / (1.0 + jnp.exp(-z)))).astype(BF16)


def _attn_a_kernel(bias_ref, q_ref, k_ref, v_ref, z_ref, o_ref, vt_ref):
    _store_vt(vt_ref, v_ref)

    def scores(i):
        return _nt_dot(k_ref[0:(i + 1) * TQ, :], q_ref[i * TQ:(i + 1) * TQ, :])

    def consume(i, s_t):
        ext = (i + 1) * TQ
        off = SEQ - ext
        parts = [s_t[r:r + PART, :] + bias_ref[off + r:off + r + PART, :] for r in range(0, ext, PART)]
        o = _softmax_pv_keymajor(_with_max(parts), vt_ref[:, 0:ext])
        o_ref[i * TQ:ext, :] = _gated(o, z_ref[i * TQ:ext, :])

    return list(range(N_QT)), scores, consume


def _attn_a(h_ab, bias):
    batch, n = h_ab.shape[0], HEADS_PER_STEP
    return pl.pallas_call(
        _per_head(_attn_a_kernel, 1, 1, A_PIPE_DEPTH, interleave=A_INTERLEAVE),
        out_shape=jax.ShapeDtypeStruct((batch, A_HEADS, SEQ, HEAD_DIM), BF16),
        grid=(batch, A_HEADS // n),
        in_specs=[pl.BlockSpec((SEQ, TQ), lambda b, h: (0, 0)),
                  _heads_spec(0, n), _heads_spec(A_HEADS, n), _heads_spec(2 * A_HEADS, n),
                  _heads_spec(Z_SLAB_OFF, n)],
        out_specs=_heads_spec(0, n),
        scratch_shapes=[pltpu.VMEM((A_INTERLEAVE, HEAD_DIM + ONES_ROWS, SEQ), BF16)],
        compiler_params=pltpu.CompilerParams(
            dimension_semantics=("parallel", "parallel"),
            vmem_limit_bytes=VMEM_LIMIT),
        name="dilated_attn",
    )(bias, h_ab, h_ab, h_ab, h_ab)


B_Q_OFF, B_K_OFF, B_V_OFF = 24, 32, 40


def _attn_b_kernel(lam_ref, g_ref, q_ref, k_ref, v_ref, z_ref, o_ref, vcat_ref, *, lam_init):
    vcat_ref[:, :HEAD_DIM] = v_ref[0]
    vcat_ref[:, HEAD_DIM:] = v_ref[1]
    lam_v = lam_ref[...]
    lam = (jnp.exp(jnp.sum(lam_v[0:1] * lam_v[1:2], axis=-1, keepdims=True))
           - jnp.exp(jnp.sum(lam_v[2:3] * lam_v[3:4], axis=-1, keepdims=True)) + lam_init)
    query_ge_key = _key_le_query_mask(key_axis=1)

    def scores(i):
        return [_nt_dot(q_ref[t, i * TQ:(i + 1) * TQ, :], k_ref[t, 0:(i + 1) * TQ, :])
                for t in range(2)]

    def consume(i, s12):
        ext = (i + 1) * TQ
        rows = slice(i * TQ, ext)
        ps, ls = [], []
        for s in s12:
            own = jnp.where(query_ge_key, s[:, i * TQ:], NEG)
            s = own if i == 0 else jnp.concatenate([s[:, :i * TQ], own], axis=1)
            m = jnp.max(s, axis=-1, keepdims=True)
            ps.append(jnp.exp2(s - m))
            ls.append(jnp.sum(ps[-1], axis=-1, keepdims=True))
        d = ps[0] - ps[1] * (lam * ls[0] / ls[1])
        o = jnp.dot(d.astype(BF16), vcat_ref[0:ext, :], preferred_element_type=F32) * (1.0 / ls[0])
        ms = jnp.mean(o * o, axis=-1, keepdims=True)
        y = o * lax.rsqrt(ms + NORM_EPS) * g_ref[...] * (1.0 - lam_init)
        o_ref[0, rows, :] = _gated(y[:, :HEAD_DIM], z_ref[0, rows, :])
        o_ref[1, rows, :] = _gated(y[:, HEAD_DIM:], z_ref[1, rows, :])

    return list(range(N_QT)), scores, consume


def _attn_b(h_ab, lam_vec, subln_g, lam_init):
    batch = h_ab.shape[0]
    heads = HEADS_PER_STEP // 2
    spec = lambda off: _heads_spec(off, 2 * heads)
    return pl.pallas_call(
        _per_head(functools.partial(_attn_b_kernel, lam_init=lam_init), 2, 1, B_PIPE_DEPTH,
                  interleave=B_INTERLEAVE, slabs_per_head=2),
        out_shape=jax.ShapeDtypeStruct((batch, 2 * B_HEADS, SEQ, HEAD_DIM), BF16),
        grid=(batch, B_HEADS // heads),
        in_specs=[
            pl.BlockSpec((4, HEAD_DIM), lambda b, h: (0, 0)),
            pl.BlockSpec((1, B_VDIM), lambda b, h: (0, 0)),
            spec(B_Q_OFF), spec(B_K_OFF), spec(B_V_OFF), spec(Z_SLAB_OFF + A_HEADS),
        ],
        out_specs=spec(0),
        scratch_shapes=[pltpu.VMEM((B_INTERLEAVE, SEQ, B_VDIM), BF16)],
        compiler_params=pltpu.CompilerParams(
            dimension_semantics=("parallel", "parallel"),
            vmem_limit_bytes=VMEM_LIMIT),
        name="diff_attn",
    )(lam_vec, subln_g, h_ab, h_ab, h_ab, h_ab)


KM_ROWS = 16


def _attn_c_kernel(q_ref, k_ref, v_ref, z_ref, o_ref, vt_ref):
    assert TQ == MOBA_BLOCK
    key_blk = lax.broadcasted_iota(jnp.int32, (KM_ROWS, SEQ), 1) // MOBA_BLOCK
    pool = (lax.broadcasted_iota(jnp.int32, (KM_ROWS, SEQ), 0) == key_blk).astype(BF16)
    km = jnp.dot(pool, k_ref[...], preferred_element_type=F32) * (1.0 / MOBA_BLOCK)
    km_hi = km.astype(BF16)
    km_lo = (km - km_hi.astype(F32)).astype(BF16)
    km_hl = jnp.concatenate([km_hi, km_lo], axis=0)
    row = lax.broadcasted_iota(jnp.int32, (KM_ROWS, TQ), 0)
    _store_vt(vt_ref, v_ref)
    key_le_query = _key_le_query_mask()

    def scores(i):
        q = q_ref[i * TQ:(i + 1) * TQ, :]
        g = _nt_dot(km_hl, q) if i > 0 else None
        s_t = _nt_dot(k_ref[0:(i + 1) * TQ, :], q)
        parts = []
        if i > 0:
            past = row < i
            gate = jnp.where(past, g[:KM_ROWS] + g[KM_ROWS:], -jnp.inf)
            rank = jnp.zeros(gate.shape, F32)
            for j in range(i):
                gj = gate[j:j + 1, :]
                beats = (gj > gate) | ((gj == gate) & (j < row))
                rank = rank + beats.astype(F32)
            bias = jnp.where(past & (rank < MOBA_TOPK), 0.0, NEG)
            for n in range(i):
                for r in range(n * TQ, (n + 1) * TQ, PART):
                    parts.append(s_t[r:r + PART, :] + bias[n:n + 1, :])
        for r in range(0, TQ, PART):
            parts.append(jnp.where(key_le_query[r:r + PART, :], s_t[i * TQ + r:i * TQ + r + PART, :], NEG))
        return _with_max(parts)

    def consume(i, parts_and_max):
        ext = (i + 1) * TQ
        o = _softmax_pv_keymajor(parts_and_max, vt_ref[:, 0:ext])
        o_ref[i * TQ:ext, :] = _gated(o, z_ref[i * TQ:ext, :])

    return list(range(N_QT)), scores, consume


def _attn_c(h_c):
    batch, n = h_c.shape[0], HEADS_PER_STEP
    return pl.pallas_call(
        _per_head(_attn_c_kernel, 0, 1, C_PIPE_DEPTH, interleave=C_INTERLEAVE),
        out_shape=jax.ShapeDtypeStruct((batch, C_HEADS, SEQ, HEAD_DIM), BF16),
        grid=(batch, C_HEADS // n),
        in_specs=[_heads_spec(0, n), _heads_spec(C_HEADS, n), _heads_spec(2 * C_HEADS, n),
                  _heads_spec(Z_SLAB_OFF, n)],
        out_specs=_heads_spec(0, n),
        scratch_shapes=[pltpu.VMEM((C_INTERLEAVE, HEAD_DIM + ONES_ROWS, SEQ), BF16)],
        compiler_params=pltpu.CompilerParams(
            dimension_semantics=("parallel", "parallel"),
            vmem_limit_bytes=VMEM_LIMIT),
        name="moba_attn",
    )(h_c, h_c, h_c, h_c)


OUT_TM = 512
OUT_SUB_M, OUT_SUB_N = 256, 256


def _outproj_kernel(*refs, n_y, final_norm):
    y_refs = refs[:n_y]
    x_ref, w_ref, g_ref, o_ref, ysc_ref, wb_ref = refs[n_y:]
    y_blocks = [(y_ref, hh) for y_ref in y_refs for hh in range(y_ref.shape[0])]

    @pl.when(pl.program_id(0) == 0)
    def _():
        wb_ref[...] = w_ref[...].astype(BF16)
    for r in range(OUT_TM // OUT_SUB_M):
        rows = slice(r * OUT_SUB_M, (r + 1) * OUT_SUB_M)
        for c, (y_ref, hh) in enumerate(y_blocks):
            ysc_ref[rows, c * HEAD_DIM:(c + 1) * HEAD_DIM] = y_ref[hh, rows, :]
        ssq = jnp.zeros((OUT_SUB_M, 1), F32)
        for n in range(D_MODEL // OUT_SUB_N):
            cols = slice(n * OUT_SUB_N, (n + 1) * OUT_SUB_N)
            h = x_ref[rows, cols] + jnp.dot(ysc_ref[rows, :], wb_ref[:, cols],
                                            preferred_element_type=F32)
            o_ref[rows, cols] = h
            if final_norm:
                ssq = ssq + jnp.sum(h * h, axis=-1, keepdims=True)
        if final_norm:
            inv = lax.rsqrt(ssq * (1.0 / D_MODEL) + NORM_EPS)
            o_ref[rows, :] = o_ref[rows, :] * inv * g_ref[...]


def _outproj(ys, x2d, w, g_final, batch, final_norm):
    rows = x2d.shape[0]
    s_tiles = SEQ // OUT_TM
    y_specs = [pl.BlockSpec((None, y.shape[1], OUT_TM, HEAD_DIM),
                            lambda i: (i // s_tiles, 0, i % s_tiles, 0)) for y in ys]
    assert sum(y.shape[1] for y in ys) == D_MODEL // HEAD_DIM
    return pl.pallas_call(
        functools.partial(_outproj_kernel, n_y=len(ys), final_norm=final_norm),
        out_shape=jax.ShapeDtypeStruct((rows, D_MODEL), F32),
        grid=(rows // OUT_TM,),
        in_specs=y_specs + [
            pl.BlockSpec((OUT_TM, D_MODEL), lambda i: (i, 0)),
            pl.BlockSpec((D_MODEL, D_MODEL), lambda i: (0, 0), pipeline_mode=pl.Buffered(1)),
            pl.BlockSpec((1, D_MODEL), lambda i: (0, 0)),
        ],
        out_specs=pl.BlockSpec((OUT_TM, D_MODEL), lambda i: (i, 0)),
        scratch_shapes=[pltpu.VMEM((OUT_TM, D_MODEL), BF16), pltpu.VMEM((D_MODEL, D_MODEL), BF16)],
        compiler_params=pltpu.CompilerParams(
            dimension_semantics=("arbitrary",),
            vmem_limit_bytes=VMEM_LIMIT),
        name="gate_outproj",
    )(*ys, x2d, w, g_final)


def _rope_tables():
    inv = 1.0 / (ROPE_THETA ** (np.arange(0, HEAD_DIM, 2, dtype=np.float64) / HEAD_DIM))
    ang = np.arange(SEQ, dtype=np.float64)[:, None] * inv[None, :]
    cos = np.concatenate([np.cos(ang), np.cos(ang)], axis=-1)
    sin = np.concatenate([-np.sin(ang), np.sin(ang)], axis=-1)
    return jnp.asarray(cos, F32), jnp.asarray(sin, F32)


def _tile_kinds(group_kinds, cols_per_group):
    kinds = []
    for kind, cols in zip(group_kinds, cols_per_group):
        assert cols % IN_TN == 0
        kinds += [kind] * (cols // IN_TN)
    assert len(kinds) == IN_WIDTH // IN_TN
    return jnp.asarray(kinds, jnp.int32)


def kernel(x, norm_ab, w_in_ab, w_out_ab, lam_ab, subln_ab, norm_c, w_in_c, w_out_c, final_norm):
    batch, seq, d = x.shape
    assert (seq, d) == (SEQ, D_MODEL) and x.dtype == F32
    assert w_in_ab.shape == (1, D_MODEL, IN_WIDTH) and w_in_c.shape == (1, D_MODEL, IN_WIDTH)
    cos, sin = _rope_tables()
    aw, bw = A_HEADS * HEAD_DIM, B_HEADS * B_VDIM
    kinds_ab = _tile_kinds(
        [KIND_ROPE_SCALED, KIND_ROPE, KIND_PLAIN, KIND_ROPE_SCALED, KIND_ROPE, KIND_PLAIN, KIND_PLAIN],
        [aw, aw, aw, bw, bw, bw, aw + bw])
    cw = C_HEADS * HEAD_DIM
    kinds_c = _tile_kinds([KIND_ROPE_SCALED, KIND_ROPE, KIND_PLAIN, KIND_PLAIN], [cw, cw, cw, cw])

    x2d = x.reshape(batch * seq, d)

    g_final = final_norm[None, :]
    h_ab = _inproj(x2d, norm_ab[0][None, :], w_in_ab[0], cos, sin, kinds_ab, batch)
    ya = _attn_a(h_ab, _dilated_bias_table())
    lam_init = 0.8 - 0.6 * math.exp(-0.3 * 0)
    yb = _attn_b(h_ab, lam_ab[0], subln_ab[0][None, :], lam_init)
    x1 = _outproj([ya, yb], x2d, w_out_ab[0], g_final, batch, final_norm=False)

    h_c = _inproj(x1, norm_c[0][None, :], w_in_c[0], cos, sin, kinds_c, batch)
    yc = _attn_c(h_c)
    out = _outproj([yc], x1, w_out_c[0], g_final, batch, final_norm=True)
    return out.reshape(batch, seq, d)
```

```python
import functools
import math

import numpy as np
import jax
import jax.numpy as jnp
from jax import lax
from jax.experimental import pallas as pl
from jax.experimental.pallas import tpu as pltpu

D_MODEL = 2048
SEQ = 2048
HEAD_DIM = 128
ROPE_THETA = 10000.0
NORM_EPS = 1e-6
A_HEADS = 8
DIL_CONFIGS = ((128, 1), (512, 4), (2048, 16))
B_HEADS = 4
B_VDIM = 2 * HEAD_DIM
C_HEADS = 16
MOBA_BLOCK = 256
MOBA_TOPK = 3
N_MOBA_BLOCKS = SEQ // MOBA_BLOCK
IN_WIDTH = 8192
N_COL_BLOCKS = IN_WIDTH // HEAD_DIM
Z_SLAB_OFF = 48
SCALE = HEAD_DIM ** -0.5
Q_SCALE = SCALE * math.log2(math.e)
NEG = -1e30

BF16 = jnp.bfloat16
F32 = jnp.float32

KIND_PLAIN, KIND_ROPE, KIND_ROPE_SCALED, KIND_GATE = 0, 1, 2, 3

VMEM_LIMIT = 56 * 1024 * 1024


def _nt_dot(a, b):
    return lax.dot_general(a, b, (((1,), (1,)), ((), ())), preferred_element_type=F32)


IN_TM, IN_TN = 2048, 1024
IN_SUB_M, IN_SUB_N = 512, 256


def _inproj_kernel(kinds_ref, x_hbm, g_ref, w_ref, cos_ref, sin_ref, o_ref, xn_ref, x_ref, x_sem):
    i, j = pl.program_id(0), pl.program_id(1)

    def x_copy(row_tile):
        return pltpu.make_async_copy(x_hbm.at[pl.ds(row_tile * IN_TM, IN_TM), :], x_ref, x_sem)

    @pl.when((i == 0) & (j == 0))
    def _():
        x_copy(0).start()

    @pl.when((j == 1) & (i + 1 < pl.num_programs(0)))
    def _():
        x_copy(i + 1).start()

    kind = kinds_ref[j]
    scale = jnp.where(kind == KIND_ROPE_SCALED, Q_SCALE, 1.0).astype(F32)
    is_rope = (kind == KIND_ROPE) | (kind == KIND_ROPE_SCALED)

    def tile(normalise):
        for nh in range(IN_TN // IN_SUB_N):
            w = w_ref[:, nh * IN_SUB_N:(nh + 1) * IN_SUB_N].astype(BF16)
            for mh in range(IN_TM // IN_SUB_M):
                rows = slice(mh * IN_SUB_M, (mh + 1) * IN_SUB_M)
                if normalise and nh == 0:
                    x = x_ref[rows, :]
                    ms = jnp.mean(x * x, axis=-1, keepdims=True)
                    xn_ref[rows, :] = (x * lax.rsqrt(ms + NORM_EPS) * g_ref[...]).astype(BF16)
                acc = jnp.dot(xn_ref[rows, :], w, preferred_element_type=F32)
                cos = jnp.where(is_rope, cos_ref[rows, :] * scale, 1.0)
                sin = jnp.where(is_rope, sin_ref[rows, :] * scale, 0.0)
                for c in range(IN_SUB_N // HEAD_DIM):
                    t = acc[:, c * HEAD_DIM:(c + 1) * HEAD_DIM]
                    r = t * cos + pltpu.roll(t, HEAD_DIM // 2, 1) * sin
                    r = jnp.where(kind == KIND_GATE, r * (1.0 / (1.0 + jnp.exp(-r))), r)
                    o_ref[nh * (IN_SUB_N // HEAD_DIM) + c, rows, :] = r.astype(BF16)

    @pl.when(j == 0)
    def _():
        x_copy(i).wait()
        tile(True)

    pl.when(j != 0)(lambda: tile(False))


def _inproj(x2d, g, w, cos, sin_signed, kinds, batch):
    rows = x2d.shape[0]
    s_tiles = SEQ // IN_TM
    grid = (rows // IN_TM, IN_WIDTH // IN_TN)
    blk_per_tile = IN_TN // HEAD_DIM
    return pl.pallas_call(
        _inproj_kernel,
        out_shape=jax.ShapeDtypeStruct((batch, N_COL_BLOCKS, SEQ, HEAD_DIM), BF16),
        grid_spec=pltpu.PrefetchScalarGridSpec(
            num_scalar_prefetch=1,
            grid=grid,
            in_specs=[
                pl.BlockSpec(memory_space=pl.ANY),
                pl.BlockSpec((1, D_MODEL), lambda i, j, k: (0, 0)),
                pl.BlockSpec((D_MODEL, IN_TN), lambda i, j, k: (0, j)),
                pl.BlockSpec((IN_TM, HEAD_DIM), lambda i, j, k: (i % s_tiles, 0)),
                pl.BlockSpec((IN_TM, HEAD_DIM), lambda i, j, k: (i % s_tiles, 0)),
            ],
            out_specs=pl.BlockSpec((None, blk_per_tile, IN_TM, HEAD_DIM),
                                   lambda i, j, k: (i // s_tiles, j, i % s_tiles, 0)),
            scratch_shapes=[pltpu.VMEM((IN_TM, D_MODEL), BF16), pltpu.VMEM((IN_TM, D_MODEL), F32),
                            pltpu.SemaphoreType.DMA],
        ),
        compiler_params=pltpu.CompilerParams(
            dimension_semantics=("arbitrary", "arbitrary"),
            vmem_limit_bytes=VMEM_LIMIT),
        name="norm_inproj",
    )(kinds, x2d, g, w, cos, sin_signed)


TQ = 256
N_QT = SEQ // TQ
PART = 256
A_PIPE_DEPTH, B_PIPE_DEPTH, C_PIPE_DEPTH = 6, 3, 2
A_INTERLEAVE, B_INTERLEAVE, C_INTERLEAVE = 1, 2, 4


ONES_ROWS = 16


def _store_vt(vt_ref, v_ref):
    dv = v_ref.shape[1]
    vt_ref[0:dv, :] = v_ref[...].astype(F32).T.astype(BF16)
    vt_ref[dv:, :] = jnp.ones((ONES_ROWS, v_ref.shape[0]), BF16)


def _with_max(parts):
    return parts, functools.reduce(jnp.maximum, [jnp.max(s, axis=0, keepdims=True) for s in parts])


def _softmax_pv_keymajor(parts_and_max, vt):
    parts, m = parts_and_max
    p_t = jnp.concatenate([jnp.exp2(s - m).astype(BF16) for s in parts], axis=0)
    o_t = jnp.dot(vt, p_t, preferred_element_type=F32)
    dv = vt.shape[0] - ONES_ROWS
    return (o_t[:dv] * (1.0 / o_t[dv:dv + 1])).T


def _pipelined(units, scores, consume, depth):
    ready = [scores(u) for u in units[:depth]]
    for idx, u in enumerate(units):
        if idx + depth < len(units):
            ready.append(scores(units[idx + depth]))
        consume(u, ready.pop(0))


def _key_le_query_mask(key_axis=0):
    key = lax.broadcasted_iota(jnp.int32, (TQ, TQ), key_axis)
    query = lax.broadcasted_iota(jnp.int32, (TQ, TQ), 1 - key_axis)
    return key <= query


def _dilated_bias_table():
    key = np.arange(SEQ)[:, None]
    query = (SEQ - TQ) + np.arange(TQ)[None, :]
    d = query - key
    mult = np.zeros(d.shape, np.float64)
    for window, dil in DIL_CONFIGS:
        mult += (d >= 0) & (d % dil == 0) & (d <= window)
    return jnp.asarray(np.where(mult > 0, np.log2(np.maximum(mult, 1.0)), NEG), F32)


def _per_head(head_fn, n_shared, n_scratch, depth, interleave=1, slabs_per_head=1):
    def group_kernel(*refs):
        shared = refs[:n_shared]
        grouped = refs[n_shared:len(refs) - n_scratch]
        scratch = refs[len(refs) - n_scratch:]

        def view(r, g):
            return r.at[g] if slabs_per_head == 1 else r.at[pl.ds(g * slabs_per_head, slabs_per_head)]

        def body(g, carry):
            heads = [head_fn(*shared, *[view(r, g * interleave + h) for r in grouped],
                             *[s.at[h] for s in scratch]) for h in range(interleave)]
            n_units = len(heads[0][0])
            units = [(h, heads[h][0][idx]) for idx in range(n_units) for h in range(interleave)]
            _pipelined(units, lambda hu: heads[hu[0]][1](hu[1]),
                       lambda hu, s: heads[hu[0]][2](hu[1], s), depth * interleave)
            return carry

        lax.fori_loop(0, grouped[0].shape[0] // (slabs_per_head * interleave), body, 0)
    return group_kernel


def _heads_spec(off, n):
    assert off % n == 0
    return pl.BlockSpec((None, n, SEQ, HEAD_DIM), lambda b, h: (b, off // n + h, 0, 0))


HEADS_PER_STEP = 4


def _gated(o, silu_z):
    return (o * silu_z.astype(F32)).astype(BF16)


def _attn_a_kernel(bias_ref, q_ref, k_ref, v_ref, z_ref, o_ref, vt_ref):
    _store_vt(vt_ref, v_ref)

    def scores(i):
        return _nt_dot(k_ref[0:(i + 1) * TQ, :], q_ref[i * TQ:(i + 1) * TQ, :])

    def consume(i, s_t):
        ext = (i + 1) * TQ
        off = SEQ - ext
        parts = [s_t[r:r + PART, :] + bias_ref[off + r:off + r + PART, :] for r in range(0, ext, PART)]
        o = _softmax_pv_keymajor(_with_max(parts), vt_ref[:, 0:ext])
        o_ref[i * TQ:ext, :] = _gated(o, z_ref[i * TQ:ext, :])

    return list(range(N_QT)), scores, consume


def _attn_a(h_ab, bias):
    batch, n = h_ab.shape[0], HEADS_PER_STEP
    return pl.pallas_call(
        _per_head(_attn_a_kernel, 1, 1, A_PIPE_DEPTH, interleave=A_INTERLEAVE),
        out_shape=jax.ShapeDtypeStruct((batch, A_HEADS, SEQ, HEAD_DIM), BF16),
        grid=(batch, A_HEADS // n),
        in_specs=[pl.BlockSpec((SEQ, TQ), lambda b, h: (0, 0)),
                  _heads_spec(0, n), _heads_spec(A_HEADS, n), _heads_spec(2 * A_HEADS, n),
                  _heads_spec(Z_SLAB_OFF, n)],
        out_specs=_heads_spec(0, n),
        scratch_shapes=[pltpu.VMEM((A_INTERLEAVE, HEAD_DIM + ONES_ROWS, SEQ), BF16)],
        compiler_params=pltpu.CompilerParams(
            dimension_semantics=("parallel", "parallel"),
            vmem_limit_bytes=VMEM_LIMIT),
        name="dilated_attn",
    )(bias, h_ab, h_ab, h_ab, h_ab)


B_Q_OFF, B_K_OFF, B_V_OFF = 24, 32, 40


def _attn_b_kernel(lam_ref, g_ref, q_ref, k_ref, v_ref, z_ref, o_ref, vcat_ref, *, lam_init):
    vcat_ref[:, :HEAD_DIM] = v_ref[0]
    vcat_ref[:, HEAD_DIM:] = v_ref[1]
    lam_v = lam_ref[...]
    lam = (jnp.exp(jnp.sum(lam_v[0:1] * lam_v[1:2], axis=-1, keepdims=True))
           - jnp.exp(jnp.sum(lam_v[2:3] * lam_v[3:4], axis=-1, keepdims=True)) + lam_init)
    query_ge_key = _key_le_query_mask(key_axis=1)

    def scores(i):
        return [_nt_dot(q_ref[t, i * TQ:(i + 1) * TQ, :], k_ref[t, 0:(i + 1) * TQ, :])
                for t in range(2)]

    def consume(i, s12):
        ext = (i + 1) * TQ
        rows = slice(i * TQ, ext)
        ps, ls = [], []
        for s in s12:
            own = jnp.where(query_ge_key, s[:, i * TQ:], NEG)
            s = own if i == 0 else jnp.concatenate([s[:, :i * TQ], own], axis=1)
            m = jnp.max(s, axis=-1, keepdims=True)
            ps.append(jnp.exp2(s - m))
            ls.append(jnp.sum(ps[-1], axis=-1, keepdims=True))
        d = ps[0] - ps[1] * (lam * ls[0] / ls[1])
        o = jnp.dot(d.astype(BF16), vcat_ref[0:ext, :], preferred_element_type=F32) * (1.0 / ls[0])
        ms = jnp.mean(o * o, axis=-1, keepdims=True)
        y = o * lax.rsqrt(ms + NORM_EPS) * g_ref[...] * (1.0 - lam_init)
        o_ref[0, rows, :] = _gated(y[:, :HEAD_DIM], z_ref[0, rows, :])
        o_ref[1, rows, :] = _gated(y[:, HEAD_DIM:], z_ref[1, rows, :])

    return list(range(N_QT)), scores, consume


def _attn_b(h_ab, lam_vec, subln_g, lam_init):
    batch = h_ab.shape[0]
    heads = HEADS_PER_STEP // 2
    spec = lambda off: _heads_spec(off, 2 * heads)
    return pl.pallas_call(
        _per_head(functools.partial(_attn_b_kernel, lam_init=lam_init), 2, 1, B_PIPE_DEPTH,
                  interleave=B_INTERLEAVE, slabs_per_head=2),
        out_shape=jax.ShapeDtypeStruct((batch, 2 * B_HEADS, SEQ, HEAD_DIM), BF16),
        grid=(batch, B_HEADS // heads),
        in_specs=[
            pl.BlockSpec((4, HEAD_DIM), lambda b, h: (0, 0)),
            pl.BlockSpec((1, B_VDIM), lambda b, h: (0, 0)),
            spec(B_Q_OFF), spec(B_K_OFF), spec(B_V_OFF), spec(Z_SLAB_OFF + A_HEADS),
        ],
        out_specs=spec(0),
        scratch_shapes=[pltpu.VMEM((B_INTERLEAVE, SEQ, B_VDIM), BF16)],
        compiler_params=pltpu.CompilerParams(
            dimension_semantics=("parallel", "parallel"),
            vmem_limit_bytes=VMEM_LIMIT),
        name="diff_attn",
    )(lam_vec, subln_g, h_ab, h_ab, h_ab, h_ab)


KM_ROWS = 16


def _attn_c_kernel(q_ref, k_ref, v_ref, z_ref, o_ref, vt_ref):
    assert TQ == MOBA_BLOCK
    key_blk = lax.broadcasted_iota(jnp.int32, (KM_ROWS, SEQ), 1) // MOBA_BLOCK
    pool = (lax.broadcasted_iota(jnp.int32, (KM_ROWS, SEQ), 0) == key_blk).astype(BF16)
    km = jnp.dot(pool, k_ref[...], preferred_element_type=F32) * (1.0 / MOBA_BLOCK)
    km_hi = km.astype(BF16)
    km_lo = (km - km_hi.astype(F32)).astype(BF16)
    km_hl = jnp.concatenate([km_hi, km_lo], axis=0)
    row = lax.broadcasted_iota(jnp.int32, (KM_ROWS, TQ), 0)
    _store_vt(vt_ref, v_ref)
    key_le_query = _key_le_query_mask()

    def scores(i):
        q = q_ref[i * TQ:(i + 1) * TQ, :]
        g = _nt_dot(km_hl, q) if i > 0 else None
        s_t = _nt_dot(k_ref[0:(i + 1) * TQ, :], q)
        parts = []
        if i > 0:
            past = row < i
            gate = jnp.where(past, g[:KM_ROWS] + g[KM_ROWS:], -jnp.inf)
            rank = jnp.zeros(gate.shape, F32)
            for j in range(i):
                gj = gate[j:j + 1, :]
                beats = (gj > gate) | ((gj == gate) & (j < row))
                rank = rank + beats.astype(F32)
            bias = jnp.where(past & (rank < MOBA_TOPK), 0.0, NEG)
            for n in range(i):
                for r in range(n * TQ, (n + 1) * TQ, PART):
                    parts.append(s_t[r:r + PART, :] + bias[n:n + 1, :])
        for r in range(0, TQ, PART):
            parts.append(jnp.where(key_le_query[r:r + PART, :], s_t[i * TQ + r:i * TQ + r + PART, :], NEG))
        return _with_max(parts)

    def consume(i, parts_and_max):
        ext = (i + 1) * TQ
        o = _softmax_pv_keymajor(parts_and_max, vt_ref[:, 0:ext])
        o_ref[i * TQ:ext, :] = _gated(o, z_ref[i * TQ:ext, :])

    return list(range(N_QT)), scores, consume


def _attn_c(h_c):
    batch, n = h_c.shape[0], HEADS_PER_STEP
    return pl.pallas_call(
        _per_head(_attn_c_kernel, 0, 1, C_PIPE_DEPTH, interleave=C_INTERLEAVE),
        out_shape=jax.ShapeDtypeStruct((batch, C_HEADS, SEQ, HEAD_DIM), BF16),
        grid=(batch, C_HEADS // n),
        in_specs=[_heads_spec(0, n), _heads_spec(C_HEADS, n), _heads_spec(2 * C_HEADS, n),
                  _heads_spec(Z_SLAB_OFF, n)],
        out_specs=_heads_spec(0, n),
        scratch_shapes=[pltpu.VMEM((C_INTERLEAVE, HEAD_DIM + ONES_ROWS, SEQ), BF16)],
        compiler_params=pltpu.CompilerParams(
            dimension_semantics=("parallel", "parallel"),
            vmem_limit_bytes=VMEM_LIMIT),
        name="moba_attn",
    )(h_c, h_c, h_c, h_c)


OUT_TM = 512
OUT_SUB_M, OUT_SUB_N = 256, 256


def _outproj_kernel(*refs, n_y, final_norm):
    y_refs = refs[:n_y]
    x_ref, w_ref, g_ref, o_ref, ysc_ref, wb_ref = refs[n_y:]
    y_blocks = [(y_ref, hh) for y_ref in y_refs for hh in range(y_ref.shape[0])]

    @pl.when(pl.program_id(0) == 0)
    def _():
        wb_ref[...] = w_ref[...].astype(BF16)
    for r in range(OUT_TM // OUT_SUB_M):
        rows = slice(r * OUT_SUB_M, (r + 1) * OUT_SUB_M)
        for c, (y_ref, hh) in enumerate(y_blocks):
            ysc_ref[rows, c * HEAD_DIM:(c + 1) * HEAD_DIM] = y_ref[hh, rows, :]
        ssq = jnp.zeros((OUT_SUB_M, 1), F32)
        for n in range(D_MODEL // OUT_SUB_N):
            cols = slice(n * OUT_SUB_N, (n + 1) * OUT_SUB_N)
            h = x_ref[rows, cols] + jnp.dot(ysc_ref[rows, :], wb_ref[:, cols],
                                            preferred_element_type=F32)
            o_ref[rows, cols] = h
            if final_norm:
                ssq = ssq + jnp.sum(h * h, axis=-1, keepdims=True)
        if final_norm:
            inv = lax.rsqrt(ssq * (1.0 / D_MODEL) + NORM_EPS)
            o_ref[rows, :] = o_ref[rows, :] * inv * g_ref[...]


def _outproj(ys, x2d, w, g_final, batch, final_norm):
    rows = x2d.shape[0]
    s_tiles = SEQ // OUT_TM
    y_specs = [pl.BlockSpec((None, y.shape[1], OUT_TM, HEAD_DIM),
                            lambda i: (i // s_tiles, 0, i % s_tiles, 0)) for y in ys]
    assert sum(y.shape[1] for y in ys) == D_MODEL // HEAD_DIM
    return pl.pallas_call(
        functools.partial(_outproj_kernel, n_y=len(ys), final_norm=final_norm),
        out_shape=jax.ShapeDtypeStruct((rows, D_MODEL), F32),
        grid=(rows // OUT_TM,),
        in_specs=y_specs + [
            pl.BlockSpec((OUT_TM, D_MODEL), lambda i: (i, 0)),
            pl.BlockSpec((D_MODEL, D_MODEL), lambda i: (0, 0), pipeline_mode=pl.Buffered(1)),
            pl.BlockSpec((1, D_MODEL), lambda i: (0, 0)),
        ],
        out_specs=pl.BlockSpec((OUT_TM, D_MODEL), lambda i: (i, 0)),
        scratch_shapes=[pltpu.VMEM((OUT_TM, D_MODEL), BF16), pltpu.VMEM((D_MODEL, D_MODEL), BF16)],
        compiler_params=pltpu.CompilerParams(
            dimension_semantics=("arbitrary",),
            vmem_limit_bytes=VMEM_LIMIT),
        name="gate_outproj",
    )(*ys, x2d, w, g_final)


def _rope_tables():
    inv = 1.0 / (ROPE_THETA ** (np.arange(0, HEAD_DIM, 2, dtype=np.float64) / HEAD_DIM))
    ang = np.arange(SEQ, dtype=np.float64)[:, None] * inv[None, :]
    cos = np.concatenate([np.cos(ang), np.cos(ang)], axis=-1)
    sin = np.concatenate([-np.sin(ang), np.sin(ang)], axis=-1)
    return jnp.asarray(cos, F32), jnp.asarray(sin, F32)


def _tile_kinds(group_kinds, cols_per_group):
    kinds = []
    for kind, cols in zip(group_kinds, cols_per_group):
        assert cols % IN_TN == 0
        kinds += [kind] * (cols // IN_TN)
    assert len(kinds) == IN_WIDTH // IN_TN
    return jnp.asarray(kinds, jnp.int32)


def kernel(x, norm_ab, w_in_ab, w_out_ab, lam_ab, subln_ab, norm_c, w_in_c, w_out_c, final_norm):
    batch, seq, d = x.shape
    assert (seq, d) == (SEQ, D_MODEL) and x.dtype == F32
    assert w_in_ab.shape == (1, D_MODEL, IN_WIDTH) and w_in_c.shape == (1, D_MODEL, IN_WIDTH)
    cos, sin = _rope_tables()
    aw, bw = A_HEADS * HEAD_DIM, B_HEADS * B_VDIM
    kinds_ab = _tile_kinds(
        [KIND_ROPE_SCALED, KIND_ROPE, KIND_PLAIN, KIND_ROPE_SCALED, KIND_ROPE, KIND_PLAIN, KIND_GATE],
        [aw, aw, aw, bw, bw, bw, aw + bw])
    cw = C_HEADS * HEAD_DIM
    kinds_c = _tile_kinds([KIND_ROPE_SCALED, KIND_ROPE, KIND_PLAIN, KIND_GATE], [cw, cw, cw, cw])

    x2d = x.reshape(batch * seq, d)

    g_final = final_norm[None, :]
    h_ab = _inproj(x2d, norm_ab[0][None, :], w_in_ab[0], cos, sin, kinds_ab, batch)
    ya = _attn_a(h_ab, _dilated_bias_table())
    lam_init = 0.8 - 0.6 * math.exp(-0.3 * 0)
    yb = _attn_b(h_ab, lam_ab[0], subln_ab[0][None, :], lam_init)
    x1 = _outproj([ya, yb], x2d, w_out_ab[0], g_final, batch, final_norm=False)

    h_c = _inproj(x1, norm_c[0][None, :], w_in_c[0], cos, sin, kinds_c, batch)
    yc = _attn_c(h_c)
    out = _outproj([yc], x1, w_out_c[0], g_final, batch, final_norm=True)
    return out.reshape(batch, seq, d)
```
